```python
import jax, jax.numpy as jnp
from jax import lax
import numpy as np

D_MODEL = 1024
BATCH = 8
SEQ = 4096
DEPTH = 1
DEC_BATCH = 32
DEC_SEQ = 4
PAST_LEN = 16384
PAGE_SIZE = 128

N_META = 16
MIX_WIDTH = D_MODEL
ATT_DIM = 64
ATT_WIDTH = MIX_WIDTH // 2
ATT_HEADS = ATT_WIDTH // ATT_DIM
IDX_HEADS = 8
IDX_DIM = 64
TOPK_MAX = 256
RET_DK = 128
RET_DV = 128
RET_WIDTH = MIX_WIDTH - ATT_WIDTH
RET_HEADS = RET_WIDTH // RET_DV
CHUNK = 128
Q_BLOCK = 128
D_FF = 2816
ROPE_BASE = 10000.0
LN_EPS = 1e-5
GN_EPS = 1e-5
ALPHA = (2.0 * DEPTH) ** 0.25
BETA = (8.0 * DEPTH) ** -0.25
IN_SPLITS = (ATT_WIDTH, ATT_WIDTH, ATT_WIDTH, IDX_HEADS * IDX_DIM, IDX_DIM, IDX_HEADS,
             RET_HEADS * RET_DK, RET_HEADS * RET_DK, RET_WIDTH, RET_WIDTH)
IN_COLS = sum(IN_SPLITS)

kernel_name = "hymba_dsa_retention_macaron_step"


def _layernorm(x, g, b):
    xf = x.astype(jnp.float32)
    mu = jnp.mean(xf, axis=-1, keepdims=True)
    var = jnp.mean(jnp.square(xf - mu), axis=-1, keepdims=True)
    return ((xf - mu) * lax.rsqrt(var + LN_EPS) * g.astype(jnp.float32) + b.astype(jnp.float32)).astype(x.dtype)


def _swiglu(x, wg, wu, wd):
    return (jax.nn.silu(x @ wg) * (x @ wu)) @ wd


def _rope(x, pos):
    half = x.shape[-1] // 2
    inv = ROPE_BASE ** (-jnp.arange(half, dtype=jnp.float32) / half)
    ang = pos.astype(jnp.float32)[:, None] * inv[None, :]
    cos = jnp.cos(ang)[None, :, None, :]
    sin = jnp.sin(ang)[None, :, None, :]
    xf = x.astype(jnp.float32)
    x1, x2 = xf[..., :half], xf[..., half:]
    return jnp.concatenate([x1 * cos - x2 * sin, x1 * sin + x2 * cos], axis=-1)


def _log_gamma():
    return jnp.log(1.0 - 2.0 ** (-5.0 - jnp.arange(RET_HEADS, dtype=jnp.float32)))


def _project(h, w_in, pos):
    B, T, _ = h.shape
    z = h @ w_in
    offs = np.cumsum(IN_SPLITS)[:-1].tolist()
    q, k, v, qi, ki, wi, rq, rk, rv, g = jnp.split(z, offs, axis=-1)
    q = q.reshape(B, T, ATT_HEADS, ATT_DIM)
    k = k.reshape(B, T, ATT_HEADS, ATT_DIM)
    v = v.reshape(B, T, ATT_HEADS, ATT_DIM)
    qi = qi.reshape(B, T, IDX_HEADS, IDX_DIM)
    rq = _rope(rq.reshape(B, T, RET_HEADS, RET_DK), pos)
    rk = _rope(rk.reshape(B, T, RET_HEADS, RET_DK), pos) * (RET_DK ** -0.5)
    rv = rv.reshape(B, T, RET_HEADS, RET_DV).astype(jnp.float32)
    return q, k, v, qi, ki, wi, rq, rk, rv, g


def _indexer_topk(qi, wi, q_pos, ki, topk):
    L = ki.shape[1]
    s = jnp.einsum('bqhd,bld->bqhl', qi, ki).astype(jnp.float32) * (IDX_DIM ** -0.5)
    score = jnp.einsum('bqhl,bqh->bql', jax.nn.relu(s), wi.astype(jnp.float32)) * (IDX_HEADS ** -0.5)
    visible = jnp.arange(L, dtype=jnp.int32)[None, None, :] <= q_pos[None, :, None]
    score = jnp.where(visible, score, -jnp.inf)
    _, idx = lax.top_k(score, topk)
    valid = idx <= q_pos[None, :, None]
    return idx, valid


def _sparse_attend(q, ks, vs, valid):
    logits = jnp.einsum('bqhd,bqkhd->bqhk', q, ks).astype(jnp.float32) * (ATT_DIM ** -0.5)
    logits = jnp.where(valid[:, :, None, :], logits, -jnp.inf)
    p = jax.nn.softmax(logits, axis=-1).astype(vs.dtype)
    return jnp.einsum('bqhk,bqkhd->bqhd', p, vs)


def _retention_chunk(q, k, v, s_prev, lg):
    L = q.shape[1]
    n = jnp.arange(L, dtype=jnp.float32)
    diff = n[:, None] - n[None, :]
    decay = jnp.where(diff[None] >= 0, jnp.exp(jnp.maximum(diff, 0.0)[None] * lg[:, None, None]), 0.0)
    inner = jnp.einsum('bnhd,bmhd->bhnm', q, k) * decay[None]
    intra = jnp.einsum('bhnm,bmhe->bnhe', inner, v)
    xi = jnp.exp((n[:, None] + 1.0) * lg[None, :])
    sp = s_prev.astype(jnp.float32)
    cross = jnp.einsum('bnhd,bhde->bnhe', q, sp) * xi[None, :, :, None]
    zeta = jnp.exp((L - 1.0 - n)[:, None] * lg[None, :])
    s_new = jnp.exp(L * lg)[None, :, None, None] * sp + jnp.einsum('bmhd,bmhe->bhde', k * zeta[None, :, :, None], v)
    return intra + cross, s_new


def _mixer_out(att, ret, g, w_out):
    B, T = att.shape[:2]
    mu = jnp.mean(ret, axis=-1, keepdims=True)
    var = jnp.mean(jnp.square(ret - mu), axis=-1, keepdims=True)
    retn = ((ret - mu) * lax.rsqrt(var + GN_EPS)).reshape(B, T, RET_WIDTH).astype(g.dtype)
    y = jnp.concatenate([att.reshape(B, T, ATT_WIDTH), jax.nn.silu(g) * retn], axis=-1)
    return y @ w_out


def _prompt_mixer(h, w_in, w_out):
    B, T, _ = h.shape
    pos = jnp.arange(T, dtype=jnp.int32)
    q, k, v, qi, ki, wi, rq, rk, rv, g = _project(h, w_in, pos)
    topk = min(TOPK_MAX, SEQ // 4)
    bidx = jnp.arange(B)[:, None, None]

    def attend(qb, qib, wb, posb):
        idx, valid = _indexer_topk(qib, wb, posb, ki, topk)
        return _sparse_attend(qb, k[bidx, idx], v[bidx, idx], valid)

    att_meta = attend(q[:, :N_META], qi[:, :N_META], wi[:, :N_META], pos[:N_META])
    nb = SEQ // Q_BLOCK
    blk = lambda a: jnp.moveaxis(a[:, N_META:].reshape((B, nb, Q_BLOCK) + a.shape[2:]), 1, 0)
    att_real = lax.map(lambda xs: attend(*xs), (blk(q), blk(qi), blk(wi), pos[N_META:].reshape(nb, Q_BLOCK)))
    att_real = jnp.moveaxis(att_real, 0, 1).reshape(B, SEQ, ATT_HEADS, ATT_DIM)
    att = jnp.concatenate([att_meta, att_real], axis=1)

    lg = _log_gamma()
    s0 = jnp.zeros((B, RET_HEADS, RET_DK, RET_DV), jnp.float32)
    ret_meta, s = _retention_chunk(rq[:, :N_META], rk[:, :N_META], rv[:, :N_META], s0, lg)
    nc = SEQ // CHUNK
    chk = lambda a: jnp.moveaxis(a[:, N_META:].reshape((B, nc, CHUNK) + a.shape[2:]), 1, 0)

    def step(carry, xs):
        qc, kc, vc = xs
        o, carry = _retention_chunk(qc, kc, vc, carry, lg)
        return carry, o

    s_fin, ret_real = lax.scan(step, s, (chk(rq), chk(rk), chk(rv)))
    ret_real = jnp.moveaxis(ret_real, 0, 1).reshape(B, SEQ, RET_HEADS, RET_DV)
    ret = jnp.concatenate([ret_meta, ret_real], axis=1)
    out = _mixer_out(att, ret, g, w_out)
    return out, (k, v, ki, s_fin.astype(h.dtype))


def _sample_mixer(h, cache_k, cache_v, cache_idx_k, state_ret, page_table, layer, w_in, w_out):
    B = h.shape[0]
    n_pages = PAST_LEN // PAGE_SIZE
    pos = PAST_LEN + jnp.arange(DEC_SEQ, dtype=jnp.int32)
    q, k, v, qi, ki, wi, rq, rk, rv, g = _project(h, w_in, pos)
    ki_past = cache_idx_k[layer, page_table].reshape(B, PAST_LEN, IDX_DIM).astype(ki.dtype)
    ki_all = jnp.concatenate([ki_past, ki], axis=1)
    topk = min(TOPK_MAX, (PAST_LEN + DEC_SEQ) // 4)
    idx, valid = _indexer_topk(qi, wi, pos, ki_all, topk)
    bidx = jnp.arange(B)[:, None, None]
    in_past = idx < PAST_LEN
    phys = page_table[bidx, jnp.minimum(idx // PAGE_SIZE, n_pages - 1)]
    off = idx % PAGE_SIZE
    new_i = jnp.clip(idx - PAST_LEN, 0, DEC_SEQ - 1)

    def sel(cache, new):
        return jnp.where(in_past[..., None, None], cache[layer, phys, off].astype(new.dtype), new[bidx, new_i])

    att = _sparse_attend(q, sel(cache_k, k), sel(cache_v, v), valid)
    ret, s_new = _retention_chunk(rq, rk, rv, state_ret[layer], _log_gamma())
    out = _mixer_out(att, ret, g, w_out)
    return out, (k, v, ki, s_new.astype(state_ret.dtype))


def _macaron_layer(x, mixer, f1g, f1u, f1d, ln1g, ln1b, ln2g, ln2b, f2g, f2u, f2d, ln3g, ln3b):
    x = _layernorm(ALPHA * x + 0.5 * _swiglu(x, f1g, f1u, f1d), ln1g, ln1b)
    m, st = mixer(x)
    x = _layernorm(ALPHA * x + m, ln2g, ln2b)
    x = _layernorm(ALPHA * x + 0.5 * _swiglu(x, f2g, f2u, f2d), ln3g, ln3b)
    return x, st


def setup_inputs(seed: int = 0) -> dict:
    key = jax.random.key(seed)
    ks = jax.random.split(key, 24)
    f32 = jnp.float32
    n_pages = PAST_LEN // PAGE_SIZE
    n_used = DEC_BATCH * n_pages
    n_pool = n_used + max(1, n_used // 4)

    def nrm(k, shape, scale):
        return jax.random.normal(k, shape, f32) * scale

    page_table = jax.random.permutation(ks[6], n_pool)[:n_used].reshape(DEC_BATCH, n_pages).astype(jnp.int32)
    return {
        "x_prompt": nrm(ks[0], (BATCH, SEQ, D_MODEL), 1.0),
        "x_sample": nrm(ks[1], (DEC_BATCH, DEC_SEQ, D_MODEL), 1.0),
        "cache_k": nrm(ks[2], (DEPTH, n_pool, PAGE_SIZE, ATT_HEADS, ATT_DIM), 1.0),
        "cache_v": nrm(ks[3], (DEPTH, n_pool, PAGE_SIZE, ATT_HEADS, ATT_DIM), 1.0),
        "cache_idx_k": nrm(ks[4], (DEPTH, n_pool, PAGE_SIZE, IDX_DIM), 1.0),
        "state_ret": nrm(ks[5], (DEPTH, DEC_BATCH, RET_HEADS, RET_DK, RET_DV), 0.5),
        "page_table": page_table,
        "meta_tokens": nrm(ks[7], (N_META, D_MODEL), 1.0),
        "ffn1_w_gate": nrm(ks[8], (DEPTH, D_MODEL, D_FF), D_MODEL ** -0.5),
        "ffn1_w_up": nrm(ks[9], (DEPTH, D_MODEL, D_FF), D_MODEL ** -0.5),
        "ffn1_w_down": nrm(ks[10], (DEPTH, D_FF, D_MODEL), BETA * D_FF ** -0.5),
        "ln1_g": 1.0 + nrm(ks[11], (DEPTH, D_MODEL), 0.02),
        "ln1_b": nrm(ks[12], (DEPTH, D_MODEL), 0.02),
        "w_in": nrm(ks[13], (DEPTH, D_MODEL, IN_COLS), D_MODEL ** -0.5),
        "w_out": nrm(ks[14], (DEPTH, MIX_WIDTH, D_MODEL), BETA * MIX_WIDTH ** -0.5),
        "ln2_g": 1.0 + nrm(ks[15], (DEPTH, D_MODEL), 0.02),
        "ln2_b": nrm(ks[16], (DEPTH, D_MODEL), 0.02),
        "ffn2_w_gate": nrm(ks[17], (DEPTH, D_MODEL, D_FF), D_MODEL ** -0.5),
        "ffn2_w_up": nrm(ks[18], (DEPTH, D_MODEL, D_FF), D_MODEL ** -0.5),
        "ffn2_w_down": nrm(ks[19], (DEPTH, D_FF, D_MODEL), BETA * D_FF ** -0.5),
        "ln3_g": 1.0 + nrm(ks[20], (DEPTH, D_MODEL), 0.02),
        "ln3_b": nrm(ks[21], (DEPTH, D_MODEL), 0.02),
    }


def reference(x_prompt, x_sample, cache_k, cache_v, cache_idx_k, state_ret, page_table, meta_tokens,
              ffn1_w_gate, ffn1_w_up, ffn1_w_down, ln1_g, ln1_b, w_in, w_out, ln2_g, ln2_b,
              ffn2_w_gate, ffn2_w_up, ffn2_w_down, ln3_g, ln3_b):
    meta = jnp.broadcast_to(meta_tokens[None].astype(x_prompt.dtype), (x_prompt.shape[0], N_META, D_MODEL))
    xp = jnp.concatenate([meta, x_prompt], axis=1)
    xs = x_sample
    new_p, new_s = [], []
    for l in range(DEPTH):
        lw = (ffn1_w_gate[l], ffn1_w_up[l], ffn1_w_down[l], ln1_g[l], ln1_b[l], ln2_g[l], ln2_b[l],
              ffn2_w_gate[l], ffn2_w_up[l], ffn2_w_down[l], ln3_g[l], ln3_b[l])
        xp, stp = _macaron_layer(xp, lambda h: _prompt_mixer(h, w_in[l], w_out[l]), *lw)
        xs, sts = _macaron_layer(xs, lambda h: _sample_mixer(h, cache_k, cache_v, cache_idx_k, state_ret,
                                                             page_table, l, w_in[l], w_out[l]), *lw)
        new_p.append(stp)
        new_s.append(sts)
    k_prompt = jnp.stack([s[0] for s in new_p])
    v_prompt = jnp.stack([s[1] for s in new_p])
    idxk_prompt = jnp.stack([s[2] for s in new_p])
    ret_prompt = jnp.stack([s[3] for s in new_p])
    k_sample = jnp.stack([s[0] for s in new_s])
    v_sample = jnp.stack([s[1] for s in new_s])
    idxk_sample = jnp.stack([s[2] for s in new_s])
    ret_sample = jnp.stack([s[3] for s in new_s])
    y_prompt = xp[:, N_META:]
    return (y_prompt, xs, k_prompt, v_prompt, idxk_prompt, ret_prompt, k_sample, v_sample, idxk_sample, ret_sample)
```

```python
import functools

import jax
import jax.numpy as jnp
import numpy as np
from jax import lax
from jax.experimental import pallas as pl
from jax.experimental.pallas import tpu as pltpu

F32 = jnp.float32
BF16 = jnp.bfloat16
I32 = jnp.int32

D_MODEL = 1024
N_META = 16
ATT_DIM = 64
ATT_HEADS = 8
ATT_WIDTH = ATT_HEADS * ATT_DIM
IDX_HEADS = 8
IDX_DIM = 64
TOPK_MAX = 256
RET_HEADS = 4
RET_DK = 128
RET_DV = 128
RET_WIDTH = RET_HEADS * RET_DV
PAGE_SIZE = 128
ROPE_BASE = 10000.0
LN_EPS = 1e-5
GN_EPS = 1e-5
DEPTH = 1
ALPHA = (2.0 * DEPTH) ** 0.25

LANES = 128
ROW_TILE = 256
Q_TILE = 256
K_TILE = 256
RET_CHUNK = 128
PAGES_PER_STEP = 8
NEG_BIAS = -1e30
INT_MIN = -2147483648
VMEM_LIMIT = 56 * 1024 * 1024


def _dot(a, b):
    return jnp.dot(a, b, preferred_element_type=F32)


def _dot_nt(a, b):
    return lax.dot_general(a, b, (((1,), (1,)), ((), ())), preferred_element_type=F32)


def _layernorm(y, g, b):
    mu = jnp.mean(y, axis=-1, keepdims=True)
    d = y - mu
    var = jnp.mean(d * d, axis=-1, keepdims=True)
    return d * lax.rsqrt(var + LN_EPS) * g + b


def _swiglu(xb, wg_ref, wu_ref, wd_ref):
    hg = _dot(xb, wg_ref[...])
    hu = _dot(xb, wu_ref[...])
    act = (hg * jax.nn.sigmoid(hg)) * hu
    return _dot(act.astype(BF16), wd_ref[...])


def _ffn_ln_kernel(x_ref, wg_ref, wu_ref, wd_ref, g_ref, b_ref, o_ref):
    x = x_ref[...]
    y = ALPHA * x + 0.5 * _swiglu(x.astype(BF16), wg_ref, wu_ref, wd_ref)
    o_ref[...] = _layernorm(y, g_ref[...], b_ref[...])


def _const_spec(shape):
    return pl.BlockSpec(shape, lambda *_: (0,) * len(shape))


def _ffn_ln(x, wg, wu, wd, g, b):
    rows, d = x.shape
    dff = wg.shape[1]
    return pl.pallas_call(
        _ffn_ln_kernel,
        grid=(rows // ROW_TILE,),
        in_specs=[pl.BlockSpec((ROW_TILE, d), lambda i: (i, 0)),
                  _const_spec((d, dff)), _const_spec((d, dff)), _const_spec((dff, d)),
                  _const_spec((1, d)), _const_spec((1, d))],
        out_specs=pl.BlockSpec((ROW_TILE, d), lambda i: (i, 0)),
        out_shape=jax.ShapeDtypeStruct((rows, d), F32),
        compiler_params=pltpu.CompilerParams(
            dimension_semantics=("arbitrary",), vmem_limit_bytes=VMEM_LIMIT),
        name="ffn_ln",
    )(x, wg, wu, wd, g, b)


def _out_ffn_kernel(x1_ref, att_ref, yret_ref, woa_ref, wor_ref, g2_ref, b2_ref,
                    wg_ref, wu_ref, wd_ref, g3_ref, b3_ref, o_ref):
    m = _dot(att_ref[...], woa_ref[...]) + _dot(yret_ref[...], wor_ref[...])
    x2 = _layernorm(ALPHA * x1_ref[...] + m, g2_ref[...], b2_ref[...])
    y = ALPHA * x2 + 0.5 * _swiglu(x2.astype(BF16), wg_ref, wu_ref, wd_ref)
    o_ref[...] = _layernorm(y, g3_ref[...], b3_ref[...])


def _out_ffn(x1, att, yret, woa, wor, g2, b2, wg, wu, wd, g3, b3):
    rows, d = x1.shape
    dff = wg.shape[1]
    row = lambda w: pl.BlockSpec((ROW_TILE, w), lambda i: (i, 0))
    return pl.pallas_call(
        _out_ffn_kernel,
        grid=(rows // ROW_TILE,),
        in_specs=[row(d), row(ATT_WIDTH), row(RET_WIDTH),
                  _const_spec((ATT_WIDTH, d)), _const_spec((RET_WIDTH, d)),
                  _const_spec((1, d)), _const_spec((1, d)),
                  _const_spec((d, dff)), _const_spec((d, dff)), _const_spec((dff, d)),
                  _const_spec((1, d)), _const_spec((1, d))],
        out_specs=row(d),
        out_shape=jax.ShapeDtypeStruct((rows, d), F32),
        compiler_params=pltpu.CompilerParams(
            dimension_semantics=("arbitrary",), vmem_limit_bytes=VMEM_LIMIT),
        name="out_ffn",
    )(x1, att, yret, woa, wor, g2, b2, wg, wu, wd, g3, b3)


W_ROW_COLS = 2 * ATT_WIDTH + LANES + 4 * RET_WIDTH
W_T_ROWS = 3 * ATT_WIDTH + LANES


def _rope(x, cos2, sin2):
    return x * cos2 + pltpu.roll(x, RET_DK // 2, 1) * sin2


def _proj_kernel(x1_ref, wrow_ref, wt_ref, cos_ref, sin_ref,
                 kf_ref, vf_ref, kiw_ref, kb_ref, kib_ref, rq_ref, rk_ref, rv_ref, g_ref,
                 qt_ref, vt_ref, qit_ref, wit_ref):
    xb = x1_ref[...].astype(BF16)
    z = _dot(xb, wrow_ref[...])
    k = z[:, 0:ATT_WIDTH]
    v = z[:, ATT_WIDTH:2 * ATT_WIDTH]
    kiw = z[:, 2 * ATT_WIDTH:2 * ATT_WIDTH + LANES]
    o = 2 * ATT_WIDTH + LANES
    kf_ref[...] = k
    vf_ref[...] = v
    kiw_ref[...] = kiw
    kb_ref[...] = k.astype(BF16)
    kib_ref[...] = kiw[:, 0:IDX_DIM].astype(BF16)
    cos2 = cos_ref[...]
    sin2 = sin_ref[...]
    for h in range(RET_HEADS):
        sl = slice(h * RET_DK, (h + 1) * RET_DK)
        rq = z[:, o + h * RET_DK:o + (h + 1) * RET_DK]
        rk = z[:, o + RET_WIDTH + h * RET_DK:o + RET_WIDTH + (h + 1) * RET_DK]
        rq_ref[:, sl] = _rope(rq, cos2, sin2).astype(BF16)
        rk_ref[:, sl] = (_rope(rk, cos2, sin2) * (RET_DK ** -0.5)).astype(BF16)
    rv_ref[...] = z[:, o + 2 * RET_WIDTH:o + 3 * RET_WIDTH].astype(BF16)
    g_ref[...] = z[:, o + 3 * RET_WIDTH:o + 4 * RET_WIDTH]

    zt = _dot_nt(wt_ref[...], xb)
    tm = xb.shape[0]
    zero = jnp.zeros((ATT_DIM, tm), BF16)
    for h in range(ATT_HEADS):
        qh = (zt[h * ATT_DIM:(h + 1) * ATT_DIM, :] * (ATT_DIM ** -0.5)).astype(BF16)
        qt_ref[0, h] = jnp.concatenate([qh, zero] if h % 2 == 0 else [zero, qh], axis=0)
    vt_ref[0] = zt[ATT_WIDTH:2 * ATT_WIDTH, :].astype(BF16)
    for h in range(IDX_HEADS):
        qih = zt[2 * ATT_WIDTH + h * IDX_DIM:2 * ATT_WIDTH + (h + 1) * IDX_DIM, :]
        qit_ref[0, :, h * tm:(h + 1) * tm] = (qih * (IDX_DIM ** -0.5)).astype(BF16)
    wit_ref[0] = zt[3 * ATT_WIDTH + IDX_DIM:3 * ATT_WIDTH + IDX_DIM + IDX_HEADS, :]


def _proj(x1, wrow, wt, cos2, sin2, tiles_per_seq):
    rows, d = x1.shape
    tm = ROW_TILE
    nt = rows // tm
    row = lambda w: pl.BlockSpec((tm, w), lambda i: (i, 0))
    tab = pl.BlockSpec((tm, LANES), lambda i: (i % tiles_per_seq, 0))
    out_shape = (
        jax.ShapeDtypeStruct((rows, ATT_WIDTH), F32),
        jax.ShapeDtypeStruct((rows, ATT_WIDTH), F32),
        jax.ShapeDtypeStruct((rows, LANES), F32),
        jax.ShapeDtypeStruct((rows, ATT_WIDTH), BF16),
        jax.ShapeDtypeStruct((rows, IDX_DIM), BF16),
        jax.ShapeDtypeStruct((rows, RET_WIDTH), BF16),
        jax.ShapeDtypeStruct((rows, RET_WIDTH), BF16),
        jax.ShapeDtypeStruct((rows, RET_WIDTH), BF16),
        jax.ShapeDtypeStruct((rows, RET_WIDTH), F32),
        jax.ShapeDtypeStruct((nt, ATT_HEADS, 2 * ATT_DIM, tm), BF16),
        jax.ShapeDtypeStruct((nt, ATT_WIDTH, tm), BF16),
        jax.ShapeDtypeStruct((nt, IDX_DIM, IDX_HEADS * tm), BF16),
        jax.ShapeDtypeStruct((nt, IDX_HEADS, tm), F32),
    )
    out_specs = (
        row(ATT_WIDTH), row(ATT_WIDTH), row(LANES), row(ATT_WIDTH), row(IDX_DIM),
        row(RET_WIDTH), row(RET_WIDTH), row(RET_WIDTH), row(RET_WIDTH),
        pl.BlockSpec((1, ATT_HEADS, 2 * ATT_DIM, tm), lambda i: (i, 0, 0, 0)),
        pl.BlockSpec((1, ATT_WIDTH, tm), lambda i: (i, 0, 0)),
        pl.BlockSpec((1, IDX_DIM, IDX_HEADS * tm), lambda i: (i, 0, 0)),
        pl.BlockSpec((1, IDX_HEADS, tm), lambda i: (i, 0, 0)),
    )
    return pl.pallas_call(
        _proj_kernel,
        grid=(nt,),
        in_specs=[row(d), _const_spec((d, W_ROW_COLS)), _const_spec((W_T_ROWS, d)), tab, tab],
        out_specs=out_specs,
        out_shape=out_shape,
        compiler_params=pltpu.CompilerParams(
            dimension_semantics=("arbitrary",), vmem_limit_bytes=VMEM_LIMIT),
        name="proj",
    )(x1, wrow, wt, cos2, sin2)


def _sortable(score):
    s = jnp.where(score == 0.0, 0.0, score)
    b = lax.bitcast_convert_type(s, I32)
    return b ^ ((b >> 31) & 0x7FFFFFFF)


def _prompt_att_kernel(qt_ref, qit_ref, wit_ref, k_ref, vt_ref, ki_ref,
                       km_ref, vtm_ref, kim_ref, o_ref,
                       keys_ref, bias_ref, ot_ref, *, topk, nq):
    j = pl.program_id(1)
    tq, tk = Q_TILE, K_TILE
    n_real = j + 1
    col = lax.broadcasted_iota(I32, (1, tq), 1)

    def idx_scores(ki_chunk):
        acc = None
        for h in range(IDX_HEADS):
            s = _dot(ki_chunk, qit_ref[0, :, h * tq:(h + 1) * tq])
            t = jnp.maximum(s, 0.0) * wit_ref[0, h:h + 1, :]
            acc = t if acc is None else acc + t
        return acc

    rows_m = lax.broadcasted_iota(I32, (LANES, tq), 0)
    keys_ref[0:LANES, :] = jnp.where(rows_m < N_META, _sortable(idx_scores(kim_ref[...])), INT_MIN)
    rows_k = lax.broadcasted_iota(I32, (tk, tq), 0)

    def p1(c, carry):
        key = _sortable(idx_scores(ki_ref[0, c]))
        vis = (c * tk + rows_k) <= (j * tq + col)
        keys_ref[pl.ds(pl.multiple_of(LANES + c * tk, LANES), tk), :] = jnp.where(vis, key, INT_MIN)
        return carry

    lax.fori_loop(0, n_real, p1, 0)

    def count(ind_fn):
        part = jnp.sum(ind_fn(keys_ref[0:LANES, :], 0).reshape(LANES // 8, 8, tq), axis=0)

        def body(c, part):
            base = pl.multiple_of(LANES + c * tk, LANES)
            m = ind_fn(keys_ref[pl.ds(base, tk), :], base)
            return part + jnp.sum(m.reshape(tk // 8, 8, tq), axis=0)

        part = lax.fori_loop(0, n_real, body, part)
        return jnp.sum(part, axis=0, keepdims=True)

    def bit_step(i, t_u):
        cand = t_u | (1 << (31 - i))
        cand_s = cand ^ INT_MIN
        cnt = count(lambda kk, base: jnp.where(kk >= cand_s, 1, 0))
        return jnp.where(cnt >= topk, cand, t_u)

    t_u = lax.fori_loop(0, 32, bit_step, jnp.zeros((1, tq), I32))
    thr = t_u ^ INT_MIN
    cnt_gt = count(lambda kk, base: jnp.where(kk > thr, 1, 0))
    cnt_eq = count(lambda kk, base: jnp.where(kk == thr, 1, 0))
    need = topk - cnt_gt
    short = thr == INT_MIN
    nbits = int(LANES + K_TILE * nq - 1).bit_length()

    def tie_search(_):
        def step(i, p):
            cand = p | (1 << (nbits - 1 - i))
            cnt = count(lambda kk, base: jnp.where(
                kk == thr, jnp.where((base + lax.broadcasted_iota(I32, kk.shape, 0)) < cand, 1, 0), 0))
            return jnp.where(cnt < need, cand, p)
        return lax.fori_loop(0, nbits, step, jnp.zeros((1, tq), I32))

    has_ties = jnp.max(jnp.where(short, 0, cnt_eq - need)) > 0
    last = lax.cond(has_ties, tie_search, lambda _: jnp.full((1, tq), (1 << nbits) - 1, I32), 0)
    last = jnp.where(short, -1, last)

    def bias_of(kk, base):
        slot = base + lax.broadcasted_iota(I32, kk.shape, 0)
        keep_eq = jnp.where(slot <= last, 0.0, NEG_BIAS)
        return jnp.where(kk > thr, 0.0, jnp.where(kk == thr, keep_eq, NEG_BIAS))

    bias_ref[0:LANES, :] = bias_of(keys_ref[0:LANES, :], 0)

    def p3(c, carry):
        base = pl.multiple_of(LANES + c * tk, LANES)
        bias_ref[pl.ds(base, tk), :] = bias_of(keys_ref[pl.ds(base, tk), :], base)
        return carry

    lax.fori_loop(0, n_real, p3, 0)

    for h in range(ATT_HEADS):
        qh = qt_ref[0, h]
        pair = slice((h // 2) * LANES, (h // 2 + 1) * LANES)
        hv = slice(h * ATT_DIM, (h + 1) * ATT_DIM)
        s = _dot(km_ref[:, pair], qh) + bias_ref[0:LANES, :]
        m = jnp.max(s, axis=0, keepdims=True)
        p = jnp.exp(s - m)
        l = jnp.sum(p, axis=0, keepdims=True)
        acc = _dot(vtm_ref[hv, :], p.astype(BF16))

        def p4(c, carry):
            m, l, acc = carry
            base = pl.multiple_of(LANES + c * tk, LANES)
            s = _dot(k_ref[0, c, :, pair], qh) + bias_ref[pl.ds(base, tk), :]
            m_new = jnp.maximum(m, jnp.max(s, axis=0, keepdims=True))
            a = jnp.exp(m - m_new)
            p = jnp.exp(s - m_new)
            l = a * l + jnp.sum(p, axis=0, keepdims=True)
            acc = a * acc + _dot(vt_ref[0, c, hv, :], p.astype(BF16))
            return m_new, l, acc

        m, l, acc = lax.fori_loop(0, n_real, p4, (m, l, acc))
        ot_ref[hv, :] = acc / l

    o_ref[0] = ot_ref[...].T.astype(BF16)


def _prompt_att(qt, qit, wit, k4, vt4, ki4, km, vtm, kim, batch, nq, topk):
    tq = Q_TILE
    ltot = LANES + nq * K_TILE
    per_batch = lambda shape: pl.BlockSpec((1,) + shape, lambda b, j: (b,) + (0,) * len(shape))
    return pl.pallas_call(
        functools.partial(_prompt_att_kernel, topk=topk, nq=nq),
        grid=(batch, nq),
        in_specs=[
            pl.BlockSpec((1, ATT_HEADS, 2 * ATT_DIM, tq), lambda b, j: (b * nq + j, 0, 0, 0)),
            pl.BlockSpec((1, IDX_DIM, IDX_HEADS * tq), lambda b, j: (b * nq + j, 0, 0)),
            pl.BlockSpec((1, IDX_HEADS, tq), lambda b, j: (b * nq + j, 0, 0)),
            per_batch((nq, K_TILE, ATT_WIDTH)),
            per_batch((nq, ATT_WIDTH, K_TILE)),
            per_batch((nq, K_TILE, IDX_DIM)),
            _const_spec((LANES, ATT_WIDTH)), _const_spec((ATT_WIDTH, LANES)), _const_spec((LANES, IDX_DIM)),
        ],
        out_specs=pl.BlockSpec((1, tq, ATT_WIDTH), lambda b, j: (b * nq + j, 0, 0)),
        out_shape=jax.ShapeDtypeStruct((batch * nq, tq, ATT_WIDTH), BF16),
        scratch_shapes=[pltpu.VMEM((ltot, tq), I32), pltpu.VMEM((ltot, tq), F32),
                        pltpu.VMEM((ATT_WIDTH, tq), F32)],
        compiler_params=pltpu.CompilerParams(
            dimension_semantics=("arbitrary", "arbitrary"), vmem_limit_bytes=VMEM_LIMIT),
        name="prompt_att",
    )(qt, qit, wit, k4, vt4, ki4, km, vtm, kim)


def _ret_kernel(rq_ref, rk_ref, rv_ref, g_ref, s0_ref, decay_ref, xi_ref, zeta_ref, gl_ref,
                y_ref, sfin_ref, state_ref):
    t = pl.program_id(1)

    @pl.when(t == 0)
    def _():
        state_ref[...] = s0_ref[0]

    for h in range(RET_HEADS):
        sl = slice(h * RET_DK, (h + 1) * RET_DK)
        q = rq_ref[0, :, sl]
        k = rk_ref[0, :, sl]
        v = rv_ref[0, :, sl]
        sp = state_ref[h]
        inner = _dot_nt(q, k) * decay_ref[h]
        ret = _dot(inner.astype(BF16), v) + _dot(q, sp.astype(BF16)) * xi_ref[h]
        kz = (k.astype(F32) * zeta_ref[h]).T.astype(BF16)
        state_ref[h] = gl_ref[h] * sp + _dot(kz, v)
        mu = jnp.mean(ret, axis=-1, keepdims=True)
        d = ret - mu
        var = jnp.mean(d * d, axis=-1, keepdims=True)
        retn = d * lax.rsqrt(var + GN_EPS)
        g = g_ref[0, :, sl]
        y_ref[0, :, sl] = ((g * jax.nn.sigmoid(g)) * retn).astype(BF16)

    @pl.when(t == pl.num_programs(1) - 1)
    def _():
        sfin_ref[0] = state_ref[...]


def _retention(rq, rk, rv, g, s0, tables, shared_s0):
    batch, tokens, _ = rq.shape
    nchunk = tokens // RET_CHUNK
    decay, xi, zeta, gl = tables
    tok = pl.BlockSpec((1, RET_CHUNK, RET_WIDTH), lambda b, t: (b, t, 0))
    hh = (RET_HEADS, RET_DK, RET_DV)
    s0_spec = pl.BlockSpec((1,) + hh, (lambda b, t: (0, 0, 0, 0)) if shared_s0 else (lambda b, t: (b, 0, 0, 0)))
    return pl.pallas_call(
        _ret_kernel,
        grid=(batch, nchunk),
        in_specs=[tok, tok, tok, tok, s0_spec,
                  _const_spec(hh), _const_spec(hh), _const_spec(hh), _const_spec((RET_HEADS, 1, RET_DV))],
        out_specs=(tok, pl.BlockSpec((1,) + hh, lambda b, t: (b, 0, 0, 0))),
        out_shape=(jax.ShapeDtypeStruct((batch, tokens, RET_WIDTH), BF16),
                   jax.ShapeDtypeStruct((batch,) + hh, F32)),
        scratch_shapes=[pltpu.VMEM(hh, F32)],
        compiler_params=pltpu.CompilerParams(dimension_semantics=("arbitrary", "arbitrary")),
        name="retention",
    )(rq, rk, rv, g, s0, decay, xi, zeta, gl)


def _ret_tables(length):
    lg = jnp.log(1.0 - 2.0 ** (-5.0 - jnp.arange(RET_HEADS, dtype=F32)))
    n = jnp.arange(RET_CHUNK, dtype=F32)
    live = n < length
    diff = n[:, None] - n[None, :]
    ok = (diff >= 0) & live[:, None] & live[None, :]
    decay = jnp.where(ok[None], jnp.exp(jnp.maximum(diff, 0.0)[None] * lg[:, None, None]), 0.0)
    xi = jnp.exp((n[None, :] + 1.0) * lg[:, None])
    zeta = jnp.where(live[None, :], jnp.exp((length - 1.0 - n)[None, :] * lg[:, None]), 0.0)
    bc = lambda a: jnp.broadcast_to(a[:, :, None], (RET_HEADS, RET_CHUNK, RET_DV)).astype(F32)
    gl = jnp.broadcast_to(jnp.exp(length * lg)[:, None, None], (RET_HEADS, 1, RET_DV)).astype(F32)
    return decay.astype(F32), bc(xi), bc(zeta), gl


def _sample_idx_kernel(pt_ref, qi_ref, wb_ref, kin_ref, *rest, n_steps, topk, dseq):
    pages = rest[:PAGES_PER_STEP]
    bias_ref = rest[PAGES_PER_STEP]
    keys_ref = rest[PAGES_PER_STEP + 1]
    pc = pl.program_id(1)
    n_chunks = n_steps * PAGES_PER_STEP + 1
    qi = qi_ref[0]
    wb = wb_ref[0]

    def scores(ki_chunk):
        s = jnp.maximum(_dot_nt(qi, ki_chunk), 0.0) * wb
        return jnp.sum(s.reshape(IDX_HEADS, 8, LANES), axis=0)

    for p in range(PAGES_PER_STEP):
        keys_ref[pc * PAGES_PER_STEP + p] = _sortable(scores(pages[p][0, 0].astype(BF16)))

    @pl.when(pc == n_steps - 1)
    def _():
        row = lax.broadcasted_iota(I32, (8, LANES), 0)
        lane = lax.broadcasted_iota(I32, (8, LANES), 1)
        vis = lane <= row
        keys_ref[n_chunks - 1] = jnp.where(vis, _sortable(scores(kin_ref[0])), INT_MIN)

        def count(ind_fn):
            def body(c, part):
                return part + ind_fn(keys_ref[c], c * LANES)
            part = lax.fori_loop(0, n_chunks, body, jnp.zeros((8, LANES), I32))
            return jnp.sum(part, axis=1, keepdims=True)

        def bit_step(i, t_u):
            cand = t_u | (1 << (31 - i))
            cand_s = cand ^ INT_MIN
            cnt = count(lambda kk, base: jnp.where(kk >= cand_s, 1, 0))
            return jnp.where(cnt >= topk, cand, t_u)

        t_u = lax.fori_loop(0, 32, bit_step, jnp.zeros((8, 1), I32))
        thr = t_u ^ INT_MIN
        cnt_gt = count(lambda kk, base: jnp.where(kk > thr, 1, 0))
        cnt_eq = count(lambda kk, base: jnp.where(kk == thr, 1, 0))
        need = topk - cnt_gt
        short = thr == INT_MIN
        nbits = int(n_chunks * LANES - 1).bit_length()

        def tie_search(_):
            def step(i, p):
                cand = p | (1 << (nbits - 1 - i))
                cnt = count(lambda kk, base: jnp.where(
                    kk == thr, jnp.where((base + lax.broadcasted_iota(I32, kk.shape, 1)) < cand, 1, 0), 0))
                return jnp.where(cnt < need, cand, p)
            return lax.fori_loop(0, nbits, step, jnp.zeros((8, 1), I32))

        real_query = lax.broadcasted_iota(I32, (8, 1), 0) < dseq
        has_ties = jnp.max(jnp.where(real_query, jnp.where(short, 0, cnt_eq - need), 0)) > 0
        last = lax.cond(has_ties, tie_search, lambda _: jnp.full((8, 1), (1 << nbits) - 1, I32), 0)
        last = jnp.where(short, -1, last)

        def p3(c, carry):
            kk = keys_ref[c]
            slot = c * LANES + lax.broadcasted_iota(I32, kk.shape, 1)
            keep_eq = jnp.where(slot <= last, 0.0, NEG_BIAS)
            bias_ref[0, c] = jnp.where(kk > thr, 0.0, jnp.where(kk == thr, keep_eq, NEG_BIAS))
            return carry

        lax.fori_loop(0, n_chunks, p3, 0)


def _page_specs(n, width, page_table_cols):
    def spec(p):
        return pl.BlockSpec((1, 1, PAGE_SIZE, width),
                            lambda b, pc, pt: (0, pt[b * page_table_cols + pc * PAGES_PER_STEP + p], 0, 0))
    return [spec(p) for p in range(n)]


def _sample_idx(pt_flat, qi_blk, wb, ki_new, cache_idx_k, n_pages, topk, dseq):
    nb = qi_blk.shape[0]
    n_steps = n_pages // PAGES_PER_STEP
    n_chunks = n_pages + 1
    grid_spec = pltpu.PrefetchScalarGridSpec(
        num_scalar_prefetch=1,
        grid=(nb, n_steps),
        in_specs=[pl.BlockSpec((1, 64, IDX_DIM), lambda b, pc, pt: (b, 0, 0)),
                  pl.BlockSpec((1, 64, LANES), lambda b, pc, pt: (b, 0, 0)),
                  pl.BlockSpec((1, LANES, IDX_DIM), lambda b, pc, pt: (b, 0, 0)),
                  *_page_specs(PAGES_PER_STEP, IDX_DIM, n_pages)],
        out_specs=pl.BlockSpec((1, n_chunks, 8, LANES), lambda b, pc, pt: (b, 0, 0, 0)),
        scratch_shapes=[pltpu.VMEM((n_chunks, 8, LANES), I32)],
    )
    return pl.pallas_call(
        functools.partial(_sample_idx_kernel, n_steps=n_steps, topk=topk, dseq=dseq),
        grid_spec=grid_spec,
        out_shape=jax.ShapeDtypeStruct((nb, n_chunks, 8, LANES), F32),
        compiler_params=pltpu.CompilerParams(dimension_semantics=("arbitrary", "arbitrary")),
        name="sample_idx",
    )(pt_flat, qi_blk, wb, ki_new, *([cache_idx_k] * PAGES_PER_STEP))


def _sample_att_kernel(pt_ref, qb_ref, bias_ref, kn_ref, vn_ref, *rest, n_steps):
    kpages = rest[:PAGES_PER_STEP]
    vpages = rest[PAGES_PER_STEP:2 * PAGES_PER_STEP]
    o_ref, m_ref, l_ref, acc_ref, kbuf, vbuf = rest[2 * PAGES_PER_STEP:]
    pc = pl.program_id(1)
    qb = qb_ref[0]
    n_chunks = n_steps * PAGES_PER_STEP + 1

    def rows64(b8):
        return jnp.concatenate([b8] * ATT_HEADS, axis=0)

    @pl.when(pc == 0)
    def _():
        s = _dot_nt(qb, kn_ref[0]) + rows64(bias_ref[0, n_chunks - 1])
        m = jnp.max(s, axis=1, keepdims=True)
        p = jnp.exp(s - m)
        m_ref[...] = m
        l_ref[...] = jnp.sum(p, axis=1, keepdims=True)
        acc_ref[...] = _dot(p.astype(BF16), vn_ref[0])

    for p in range(PAGES_PER_STEP):
        kbuf[p * PAGE_SIZE:(p + 1) * PAGE_SIZE, :] = kpages[p][0, 0].astype(BF16)
        vbuf[p * PAGE_SIZE:(p + 1) * PAGE_SIZE, :] = vpages[p][0, 0].astype(BF16)
    bias = jnp.concatenate([bias_ref[0, pc * PAGES_PER_STEP + p] for p in range(PAGES_PER_STEP)], axis=1)
    s = _dot_nt(qb, kbuf[...]) + rows64(bias)
    m_old = m_ref[...]
    m_new = jnp.maximum(m_old, jnp.max(s, axis=1, keepdims=True))
    a = jnp.exp(m_old - m_new)
    p = jnp.exp(s - m_new)
    m_ref[...] = m_new
    l_ref[...] = a * l_ref[...] + jnp.sum(p, axis=1, keepdims=True)
    acc_ref[...] = a * acc_ref[...] + _dot(p.astype(BF16), vbuf[...])

    @pl.when(pc == n_steps - 1)
    def _():
        o_ref[0] = acc_ref[...] / l_ref[...]


def _sample_att(pt_flat, qb, bias, k_new, v_new, cache_k4, cache_v4, n_pages):
    nb = qb.shape[0]
    n_steps = n_pages // PAGES_PER_STEP
    n_chunks = n_pages + 1
    rows = ATT_HEADS * 8
    grid_spec = pltpu.PrefetchScalarGridSpec(
        num_scalar_prefetch=1,
        grid=(nb, n_steps),
        in_specs=[pl.BlockSpec((1, rows, ATT_WIDTH), lambda b, pc, pt: (b, 0, 0)),
                  pl.BlockSpec((1, n_chunks, 8, LANES), lambda b, pc, pt: (b, 0, 0, 0)),
                  pl.BlockSpec((1, LANES, ATT_WIDTH), lambda b, pc, pt: (b, 0, 0)),
                  pl.BlockSpec((1, LANES, ATT_WIDTH), lambda b, pc, pt: (b, 0, 0)),
                  *_page_specs(PAGES_PER_STEP, ATT_WIDTH, n_pages),
                  *_page_specs(PAGES_PER_STEP, ATT_WIDTH, n_pages)],
        out_specs=pl.BlockSpec((1, rows, ATT_WIDTH), lambda b, pc, pt: (b, 0, 0)),
        scratch_shapes=[pltpu.VMEM((rows, 1), F32), pltpu.VMEM((rows, 1), F32),
                        pltpu.VMEM((rows, ATT_WIDTH), F32),
                        pltpu.VMEM((PAGES_PER_STEP * PAGE_SIZE, ATT_WIDTH), BF16),
                        pltpu.VMEM((PAGES_PER_STEP * PAGE_SIZE, ATT_WIDTH), BF16)],
    )
    return pl.pallas_call(
        functools.partial(_sample_att_kernel, n_steps=n_steps),
        grid_spec=grid_spec,
        out_shape=jax.ShapeDtypeStruct((nb, rows, ATT_WIDTH), F32),
        compiler_params=pltpu.CompilerParams(dimension_semantics=("arbitrary", "arbitrary")),
        name="sample_att",
    )(pt_flat, qb, bias, k_new, v_new, *([cache_k4] * PAGES_PER_STEP), *([cache_v4] * PAGES_PER_STEP))


def _rope_tables(pos):
    half = RET_DK // 2
    inv = ROPE_BASE ** (-jnp.arange(half, dtype=F32) / half)
    ang = pos.astype(F32)[:, None] * inv[None, :]
    cos, sin = jnp.cos(ang), jnp.sin(ang)
    return jnp.concatenate([cos, cos], axis=1), jnp.concatenate([-sin, sin], axis=1)


def _pad_rows(a, rows):
    return jnp.pad(a, ((0, rows - a.shape[0]),) + ((0, 0),) * (a.ndim - 1))


def kernel(x_prompt, x_sample, cache_k, cache_v, cache_idx_k, state_ret, page_table, meta_tokens,
           ffn1_w_gate, ffn1_w_up, ffn1_w_down, ln1_g, ln1_b, w_in, w_out, ln2_g, ln2_b,
           ffn2_w_gate, ffn2_w_up, ffn2_w_down, ln3_g, ln3_b):
    batch, seq, d = x_prompt.shape
    nb, dseq, _ = x_sample.shape
    n_pages = page_table.shape[1]
    past = n_pages * PAGE_SIZE
    nq = seq // Q_TILE
    assert d == D_MODEL and seq % Q_TILE == 0 and Q_TILE == ROW_TILE == K_TILE
    assert dseq <= 8 and n_pages % PAGES_PER_STEP == 0
    n_small = nb * dseq + N_META
    assert n_small <= ROW_TILE

    bf = lambda a: a.astype(BF16)
    l = 0
    f1 = (bf(ffn1_w_gate[l]), bf(ffn1_w_up[l]), bf(ffn1_w_down[l]))
    f2 = (bf(ffn2_w_gate[l]), bf(ffn2_w_up[l]), bf(ffn2_w_down[l]))
    vec = lambda a: a[l][None, :].astype(F32)
    w = w_in[l]
    offs = np.cumsum([0, 512, 512, 512, 512, 64, 8, 512, 512, 512, 512])
    wq, wk, wv, wqi, wki, wwi, wrq, wrk, wrv, wg = [w[:, offs[i]:offs[i + 1]] for i in range(10)]
    kiw = jnp.concatenate([wki, wwi, jnp.zeros((d, LANES - IDX_DIM - IDX_HEADS), w.dtype)], axis=1)
    wrow = bf(jnp.concatenate([wk, wv, kiw, wrq, wrk, wrv, wg], axis=1))
    wt = bf(jnp.concatenate([wq, wv, wqi, kiw], axis=1).T)
    woa, wor = bf(w_out[l][:ATT_WIDTH]), bf(w_out[l][ATT_WIDTH:])

    xs = _pad_rows(jnp.concatenate([x_sample.reshape(nb * dseq, d), meta_tokens.astype(x_prompt.dtype)], axis=0),
                   ROW_TILE)
    xp = x_prompt.reshape(batch * seq, d)
    pos_small = jnp.concatenate([jnp.tile(past + jnp.arange(dseq, dtype=I32), nb),
                                 jnp.arange(N_META, dtype=I32),
                                 jnp.zeros((ROW_TILE - n_small,), I32)])
    cos_s, sin_s = _rope_tables(pos_small)
    cos_p, sin_p = _rope_tables(N_META + jnp.arange(seq, dtype=I32))

    x1p = _ffn_ln(xp, *f1, vec(ln1_g), vec(ln1_b))
    x1s = _ffn_ln(xs, *f1, vec(ln1_g), vec(ln1_b))
    pp = _proj(x1p, wrow, wt, cos_p, sin_p, nq)
    ps = _proj(x1s, wrow, wt, cos_s, sin_s, 1)
    (kf_p, vf_p, kiw_p, kb_p, kib_p, rq_p, rk_p, rv_p, g_p, qt_p, vt_p, qit_p, wit_p) = pp
    (kf_s, vf_s, kiw_s, kb_s, kib_s, rq_s, rk_s, rv_s, g_s, qt_s, vt_s, qit_s, wit_s) = ps
    ns = nb * dseq
    meta = slice(ns, ns + N_META)

    topk_p = min(TOPK_MAX, seq // 4)
    km = _pad_rows(kb_s[meta], LANES)
    kim = _pad_rows(kib_s[meta], LANES)
    vtm = jnp.pad(vt_s[0][:, meta], ((0, 0), (0, LANES - N_META)))
    att_p = _prompt_att(qt_p, qit_p, wit_p,
                        kb_p.reshape(batch, nq, K_TILE, ATT_WIDTH), vt_p.reshape(batch, nq, ATT_WIDTH, K_TILE),
                        kib_p.reshape(batch, nq, K_TILE, IDX_DIM), km, vtm, kim, batch, nq, topk_p)
    att_p = att_p.reshape(batch * seq, ATT_WIDTH)

    pad_tok = lambda a, n: jnp.pad(a, ((0, 0), (0, RET_CHUNK - n), (0, 0)))
    m3 = lambda a: pad_tok(a[meta][None], N_META)
    zero_state = jnp.zeros((1, RET_HEADS, RET_DK, RET_DV), F32)
    _, s_meta = _retention(m3(rq_s), m3(rk_s), m3(rv_s), m3(g_s), zero_state, _ret_tables(N_META), True)
    b3 = lambda a: a.reshape(batch, seq, RET_WIDTH)
    yret_p, ret_prompt = _retention(b3(rq_p), b3(rk_p), b3(rv_p), b3(g_p), s_meta, _ret_tables(RET_CHUNK), True)
    s3 = lambda a: pad_tok(a[:ns].reshape(nb, dseq, RET_WIDTH), dseq)
    yret_s, ret_sample = _retention(s3(rq_s), s3(rk_s), s3(rv_s), s3(g_s), state_ret[l], _ret_tables(dseq), False)

    topk_s = min(TOPK_MAX, (past + dseq) // 4)
    pt_flat = page_table.reshape(-1).astype(I32)
    q_rows = jnp.concatenate([qt_s[0, h, (h % 2) * ATT_DIM:(h % 2 + 1) * ATT_DIM, :] for h in range(ATT_HEADS)],
                             axis=0).T[:ns]
    qi_rows = qit_s[0].reshape(IDX_DIM, IDX_HEADS, ROW_TILE)[:, :, :ns]
    qi_blk = jnp.pad(qi_rows.transpose(2, 1, 0).reshape(nb, dseq, IDX_HEADS, IDX_DIM).transpose(0, 2, 1, 3),
                     ((0, 0), (0, 0), (0, 8 - dseq), (0, 0))).reshape(nb, IDX_HEADS * 8, IDX_DIM)
    wi_s = kiw_s[:ns, IDX_DIM:IDX_DIM + IDX_HEADS].reshape(nb, dseq, IDX_HEADS)
    wb = jnp.pad(wi_s.transpose(0, 2, 1), ((0, 0), (0, 0), (0, 8 - dseq))).reshape(nb, IDX_HEADS * 8, 1)
    wb = jnp.broadcast_to(wb, (nb, IDX_HEADS * 8, LANES)).astype(F32)
    new_blk = lambda a: jnp.pad(a[:ns].reshape(nb, dseq, -1), ((0, 0), (0, LANES - dseq), (0, 0)))
    bias_s = _sample_idx(pt_flat, qi_blk, wb, new_blk(kib_s), cache_idx_k, n_pages, topk_s, dseq)
    head_mask = (jnp.arange(ATT_WIDTH)[None, :] // ATT_DIM == jnp.arange(ATT_HEADS)[:, None])
    q3 = jnp.pad(q_rows.reshape(nb, dseq, ATT_WIDTH), ((0, 0), (0, 8 - dseq), (0, 0)))
    qb = jnp.where(head_mask[None, :, None, :], q3[:, None, :, :], 0).astype(BF16).reshape(nb, ATT_HEADS * 8, ATT_WIDTH)
    att_s64 = _sample_att(pt_flat, qb, bias_s, new_blk(kb_s), new_blk(bf(vf_s)),
                          cache_k.reshape(cache_k.shape[0], cache_k.shape[1], PAGE_SIZE, ATT_WIDTH),
                          cache_v.reshape(cache_v.shape[0], cache_v.shape[1], PAGE_SIZE, ATT_WIDTH), n_pages)
    a5 = att_s64.reshape(nb, ATT_HEADS, 8, ATT_HEADS, ATT_DIM)
    att_s = jnp.stack([a5[:, h, :dseq, h, :] for h in range(ATT_HEADS)], axis=2).reshape(ns, ATT_WIDTH)

    lnw = (vec(ln2_g), vec(ln2_b), *f2, vec(ln3_g), vec(ln3_b))
    y_p = _out_ffn(x1p, att_p, yret_p.reshape(batch * seq, RET_WIDTH), woa, wor, *lnw)
    att_small = _pad_rows(bf(att_s), ROW_TILE)
    yret_small = _pad_rows(yret_s[:, :dseq].reshape(ns, RET_WIDTH), ROW_TILE)
    y_s = _out_ffn(x1s, att_small, yret_small, woa, wor, *lnw)

    def with_meta(real, small, width):
        m = jnp.broadcast_to(small[meta][None], (batch, N_META, width))
        return jnp.concatenate([m, real.reshape(batch, seq, width)], axis=1)

    k_prompt = with_meta(kf_p, kf_s, ATT_WIDTH).reshape(1, batch, seq + N_META, ATT_HEADS, ATT_DIM)
    v_prompt = with_meta(vf_p, vf_s, ATT_WIDTH).reshape(1, batch, seq + N_META, ATT_HEADS, ATT_DIM)
    idxk_prompt = with_meta(kiw_p[:, :IDX_DIM], kiw_s[:, :IDX_DIM], IDX_DIM)[None]
    y_prompt = y_p.reshape(batch, seq, d)
    y_sample = y_s[:ns].reshape(nb, dseq, d)
    k_sample = kf_s[:ns].reshape(1, nb, dseq, ATT_HEADS, ATT_DIM)
    v_sample = vf_s[:ns].reshape(1, nb, dseq, ATT_HEADS, ATT_DIM)
    idxk_sample = kiw_s[:ns, :IDX_DIM].reshape(1, nb, dseq, IDX_DIM)
    return (y_prompt, y_sample, k_prompt, v_prompt, idxk_prompt, ret_prompt[None],
            k_sample, v_sample, idxk_sample, ret_sample[None])
```

```python
import functools

import jax
import jax.numpy as jnp
import numpy as np
from jax import lax
from jax.experimental import pallas as pl
from jax.experimental.pallas import tpu as pltpu

F32 = jnp.float32
BF16 = jnp.bfloat16
I32 = jnp.int32

D_MODEL = 1024
N_META = 16
ATT_DIM = 64
ATT_HEADS = 8
ATT_WIDTH = ATT_HEADS * ATT_DIM
IDX_HEADS = 8
IDX_DIM = 64
TOPK_MAX = 256
RET_HEADS = 4
RET_DK = 128
RET_DV = 128
RET_WIDTH = RET_HEADS * RET_DV
PAGE_SIZE = 128
ROPE_BASE = 10000.0
LN_EPS = 1e-5
GN_EPS = 1e-5
DEPTH = 1
ALPHA = (2.0 * DEPTH) ** 0.25

LANES = 128
ROW_TILE = 256
Q_TILE = 256
K_TILE = 256
RET_CHUNK = 128
PAGES_PER_STEP = 8
NEG_BIAS = -1e30
INT_MIN = -2147483648
VMEM_LIMIT = 56 * 1024 * 1024


def _dot(a, b):
    return jnp.dot(a, b, preferred_element_type=F32)


def _dot_nt(a, b):
    return lax.dot_general(a, b, (((1,), (1,)), ((), ())), preferred_element_type=F32)


def _layernorm(y, g, b):
    mu = jnp.mean(y, axis=-1, keepdims=True)
    d = y - mu
    var = jnp.mean(d * d, axis=-1, keepdims=True)
    return d * lax.rsqrt(var + LN_EPS) * g + b


def _swiglu(xb, wg_ref, wu_ref, wd_ref):
    hg = _dot(xb, wg_ref[...])
    hu = _dot(xb, wu_ref[...])
    act = (hg * jax.nn.sigmoid(hg)) * hu
    return _dot(act.astype(BF16), wd_ref[...])


def _ffn_ln_kernel(x_ref, wg_ref, wu_ref, wd_ref, g_ref, b_ref, o_ref):
    x = x_ref[...]
    y = ALPHA * x + 0.5 * _swiglu(x.astype(BF16), wg_ref, wu_ref, wd_ref)
    o_ref[...] = _layernorm(y, g_ref[...], b_ref[...])


def _const_spec(shape):
    return pl.BlockSpec(shape, lambda *_: (0,) * len(shape))


def _ffn_ln(x, wg, wu, wd, g, b):
    rows, d = x.shape
    dff = wg.shape[1]
    return pl.pallas_call(
        _ffn_ln_kernel,
        grid=(rows // ROW_TILE,),
        in_specs=[pl.BlockSpec((ROW_TILE, d), lambda i: (i, 0)),
                  _const_spec((d, dff)), _const_spec((d, dff)), _const_spec((dff, d)),
                  _const_spec((1, d)), _const_spec((1, d))],
        out_specs=pl.BlockSpec((ROW_TILE, d), lambda i: (i, 0)),
        out_shape=jax.ShapeDtypeStruct((rows, d), F32),
        compiler_params=pltpu.CompilerParams(
            dimension_semantics=("arbitrary",), vmem_limit_bytes=VMEM_LIMIT),
        name="ffn_ln",
    )(x, wg, wu, wd, g, b)


def _out_ffn_kernel(x1_ref, att_ref, yret_ref, woa_ref, wor_ref, g2_ref, b2_ref,
                    wg_ref, wu_ref, wd_ref, g3_ref, b3_ref, o_ref):
    m = _dot(att_ref[...], woa_ref[...]) + _dot(yret_ref[...], wor_ref[...])
    x2 = _layernorm(ALPHA * x1_ref[...] + m, g2_ref[...], b2_ref[...])
    y = ALPHA * x2 + 0.5 * _swiglu(x2.astype(BF16), wg_ref, wu_ref, wd_ref)
    o_ref[...] = _layernorm(y, g3_ref[...], b3_ref[...])


def _out_ffn(x1, att, yret, woa, wor, g2, b2, wg, wu, wd, g3, b3):
    rows, d = x1.shape
    dff = wg.shape[1]
    row = lambda w: pl.BlockSpec((ROW_TILE, w), lambda i: (i, 0))
    return pl.pallas_call(
        _out_ffn_kernel,
        grid=(rows // ROW_TILE,),
        in_specs=[row(d), row(ATT_WIDTH), row(RET_WIDTH),
                  _const_spec((ATT_WIDTH, d)), _const_spec((RET_WIDTH, d)),
                  _const_spec((1, d)), _const_spec((1, d)),
                  _const_spec((d, dff)), _const_spec((d, dff)), _const_spec((dff, d)),
                  _const_spec((1, d)), _const_spec((1, d))],
        out_specs=row(d),
        out_shape=jax.ShapeDtypeStruct((rows, d), F32),
        compiler_params=pltpu.CompilerParams(
            dimension_semantics=("arbitrary",), vmem_limit_bytes=VMEM_LIMIT),
        name="out_ffn",
    )(x1, att, yret, woa, wor, g2, b2, wg, wu, wd, g3, b3)


W_ROW_COLS = ATT_WIDTH + LANES + 4 * RET_WIDTH
W_T_ROWS = 4 * ATT_WIDTH + LANES


def _rope(x, cos2, sin2):
    return x * cos2 + pltpu.roll(x, RET_DK // 2, 1) * sin2


def _proj_kernel(x1_ref, wrow_ref, wt_ref, cos_ref, sin_ref,
                 kb_ref, kib_ref, rq_ref, rk_ref, rv_ref, g_ref,
                 qt_ref, ktf_ref, vtf_ref, vt_ref, qit_ref, kiwt_ref):
    xb = x1_ref[...].astype(BF16)
    z = _dot(xb, wrow_ref[...])
    kb_ref[...] = z[:, 0:ATT_WIDTH].astype(BF16)
    kib_ref[...] = z[:, ATT_WIDTH:ATT_WIDTH + IDX_DIM].astype(BF16)
    o = ATT_WIDTH + LANES
    cos2 = cos_ref[...]
    sin2 = sin_ref[...]
    for h in range(RET_HEADS):
        sl = slice(h * RET_DK, (h + 1) * RET_DK)
        rq = z[:, o + h * RET_DK:o + (h + 1) * RET_DK]
        rk = z[:, o + RET_WIDTH + h * RET_DK:o + RET_WIDTH + (h + 1) * RET_DK]
        rq_ref[:, sl] = _rope(rq, cos2, sin2).astype(BF16)
        rk_ref[:, sl] = (_rope(rk, cos2, sin2) * (RET_DK ** -0.5)).astype(BF16)
    rv_ref[...] = z[:, o + 2 * RET_WIDTH:o + 3 * RET_WIDTH].astype(BF16)
    g_ref[...] = z[:, o + 3 * RET_WIDTH:o + 4 * RET_WIDTH]

    zt = _dot_nt(wt_ref[...], xb)
    tm = xb.shape[0]
    zero = jnp.zeros((ATT_DIM, tm), BF16)
    for h in range(ATT_HEADS):
        qh = (zt[h * ATT_DIM:(h + 1) * ATT_DIM, :] * (ATT_DIM ** -0.5)).astype(BF16)
        qt_ref[0, h] = jnp.concatenate([qh, zero] if h % 2 == 0 else [zero, qh], axis=0)
    ktf_ref[0] = zt[ATT_WIDTH:2 * ATT_WIDTH, :]
    vt = zt[2 * ATT_WIDTH:3 * ATT_WIDTH, :]
    vtf_ref[0] = vt
    vt_ref[0] = vt.astype(BF16)
    for h in range(IDX_HEADS):
        qih = zt[3 * ATT_WIDTH + h * IDX_DIM:3 * ATT_WIDTH + (h + 1) * IDX_DIM, :]
        qit_ref[0, :, h * tm:(h + 1) * tm] = (qih * (IDX_DIM ** -0.5)).astype(BF16)
    kiwt_ref[0] = zt[4 * ATT_WIDTH:4 * ATT_WIDTH + LANES, :]


def _proj(x1, wrow, wt, cos2, sin2, batch, tiles_per_seq):
    rows, d = x1.shape
    tm = ROW_TILE
    nt = rows // tm
    seq = tiles_per_seq * tm
    row = lambda w: pl.BlockSpec((tm, w), lambda i: (i, 0))
    tab = pl.BlockSpec((tm, LANES), lambda i: (i % tiles_per_seq, 0))
    tcol = lambda r: pl.BlockSpec((1, r, tm), lambda i: (i // tiles_per_seq, 0, i % tiles_per_seq))
    out_shape = (
        jax.ShapeDtypeStruct((rows, ATT_WIDTH), BF16),
        jax.ShapeDtypeStruct((rows, IDX_DIM), BF16),
        jax.ShapeDtypeStruct((rows, RET_WIDTH), BF16),
        jax.ShapeDtypeStruct((rows, RET_WIDTH), BF16),
        jax.ShapeDtypeStruct((rows, RET_WIDTH), BF16),
        jax.ShapeDtypeStruct((rows, RET_WIDTH), F32),
        jax.ShapeDtypeStruct((nt, ATT_HEADS, 2 * ATT_DIM, tm), BF16),
        jax.ShapeDtypeStruct((batch, ATT_WIDTH, seq), F32),
        jax.ShapeDtypeStruct((batch, ATT_WIDTH, seq), F32),
        jax.ShapeDtypeStruct((nt, ATT_WIDTH, tm), BF16),
        jax.ShapeDtypeStruct((nt, IDX_DIM, IDX_HEADS * tm), BF16),
        jax.ShapeDtypeStruct((batch, LANES, seq), F32),
    )
    out_specs = (
        row(ATT_WIDTH), row(IDX_DIM), row(RET_WIDTH), row(RET_WIDTH), row(RET_WIDTH), row(RET_WIDTH),
        pl.BlockSpec((1, ATT_HEADS, 2 * ATT_DIM, tm), lambda i: (i, 0, 0, 0)),
        tcol(ATT_WIDTH), tcol(ATT_WIDTH),
        pl.BlockSpec((1, ATT_WIDTH, tm), lambda i: (i, 0, 0)),
        pl.BlockSpec((1, IDX_DIM, IDX_HEADS * tm), lambda i: (i, 0, 0)),
        tcol(LANES),
    )
    return pl.pallas_call(
        _proj_kernel,
        grid=(nt,),
        in_specs=[row(d), _const_spec((d, W_ROW_COLS)), _const_spec((W_T_ROWS, d)), tab, tab],
        out_specs=out_specs,
        out_shape=out_shape,
        compiler_params=pltpu.CompilerParams(
            dimension_semantics=("arbitrary",), vmem_limit_bytes=VMEM_LIMIT),
        name="proj",
    )(x1, wrow, wt, cos2, sin2)


def _pattern_to_float(u):
    key = u ^ INT_MIN
    bits = jnp.where(key >= 0, key, key ^ 0x7FFFFFFF)
    return lax.bitcast_convert_type(bits, F32)


def _kth_largest(count, topk, shape):
    def step(i, t_u):
        cand = t_u | (1 << (31 - i))
        cand_f = _pattern_to_float(cand)
        cnt = count(lambda s, base: jnp.where(s >= cand_f, 1, 0))
        return jnp.where(cnt >= topk, cand, t_u)

    return lax.fori_loop(0, 32, step, jnp.zeros(shape, I32))


def _tie_break(count, thr, need, nbits, slot_axis, shape):
    def search(_):
        def step(i, p):
            cand = p | (1 << (nbits - 1 - i))
            cnt = count(lambda s, base: jnp.where(
                s == thr, jnp.where((base + lax.broadcasted_iota(I32, s.shape, slot_axis)) < cand, 1, 0), 0))
            return jnp.where(cnt < need, cand, p)
        return lax.fori_loop(0, nbits, step, jnp.zeros(shape, I32))

    return search


SPARE_CHUNKS = 2


def _prompt_att_kernel(qt_ref, qit_ref, wit_ref, k_ref, vt_ref, ki_ref, o_ref,
                       sc_ref, ot_ref, m_ref, l_ref, s0_ref, s1_ref, c0_ref, c1_ref, *, topk, nq):
    j = pl.program_id(1)
    tq, tk = Q_TILE, K_TILE
    n_chunks = j + 2
    n_pairs = lax.shift_right_logical(n_chunks + 1, 1)
    col = lax.broadcasted_iota(I32, (1, tq), 1)
    rows_k = lax.broadcasted_iota(I32, (tk, tq), 0)

    def rows_of(c, n=1):
        return pl.ds(pl.multiple_of(c * tk, tk), n * tk)

    def p1(c, carry):
        acc = None
        for h in range(IDX_HEADS):
            s = _dot(ki_ref[0, c], qit_ref[0, :, h * tq:(h + 1) * tq])
            t = jnp.maximum(s, 0.0) * wit_ref[0, h:h + 1, :]
            acc = t if acc is None else acc + t
        last_row = jnp.where(c == 0, N_META - 1, j * tq + col - (c - 1) * tk)
        sc_ref[rows_of(c), :] = jnp.where(rows_k <= last_row, acc, -jnp.inf)
        return carry

    lax.fori_loop(0, n_chunks, p1, 0)
    sc_ref[rows_of(n_chunks, SPARE_CHUNKS), :] = jnp.full((SPARE_CHUNKS * tk, tq), -jnp.inf, F32)

    def count(ind_fn):
        def body(c2, part):
            base = pl.multiple_of(2 * c2 * tk, 2 * tk)
            return part + jnp.sum(ind_fn(sc_ref[pl.ds(base, 2 * tk), :], base).reshape(tk // 4, 8, tq), axis=0)

        return jnp.sum(lax.fori_loop(0, n_pairs, body, jnp.zeros((8, tq), I32)), axis=0, keepdims=True)

    t_u = _kth_largest(count, topk, (1, tq))
    short = t_u == 0
    thr = jnp.where(short, -jnp.inf, _pattern_to_float(t_u))
    cnt_gt = count(lambda s, base: jnp.where(s > thr, 1, 0))
    cnt_eq = count(lambda s, base: jnp.where(s == thr, 1, 0))
    need = topk - cnt_gt
    nbits = int(tk * (nq + 1 + SPARE_CHUNKS) - 1).bit_length()
    has_ties = jnp.max(jnp.where(short, 0, cnt_eq - need)) > 0
    last = lax.cond(has_ties, _tie_break(count, thr, need, nbits, 0, (1, tq)),
                    lambda _: jnp.full((1, tq), (1 << nbits) - 1, I32), 0)
    last = jnp.where(short, -1, last)

    def p3(c, carry):
        s = sc_ref[rows_of(c), :]
        keep_eq = jnp.where(c * tk + rows_k <= last, 0.0, NEG_BIAS)
        sc_ref[rows_of(c), :] = jnp.where(s > thr, 0.0, jnp.where(s == thr, keep_eq, NEG_BIAS))
        return carry

    lax.fori_loop(0, n_chunks + SPARE_CHUNKS, p3, 0)

    def head_slices(h):
        return slice((h // 2) * LANES, (h // 2 + 1) * LANES), slice(h * ATT_DIM, (h + 1) * ATT_DIM)

    rep = lambda x: jnp.broadcast_to(x, (8, tq))

    def logits_stage(c, s_ref, cmax_ref):
        for h in range(ATT_HEADS):
            pair, _ = head_slices(h)
            s = _dot(k_ref[0, c, :, pair], qt_ref[0, h]) + sc_ref[rows_of(c), :]
            s_ref[h] = s
            cmax_ref[h] = rep(jnp.max(s, axis=0, keepdims=True))

    def softmax_stage(c, s_ref, cmax_ref):
        for h in range(ATT_HEADS):
            _, hv = head_slices(h)
            m_old = m_ref[h]
            m_new = jnp.maximum(m_old, cmax_ref[h])
            a = jnp.exp(m_old - m_new)
            p = jnp.exp(s_ref[h] - m_new[0:1, :])
            m_ref[h] = m_new
            l_ref[h] = a * l_ref[h] + rep(jnp.sum(p, axis=0, keepdims=True))
            ot_ref[hv, :] = a[0:1, :] * ot_ref[hv, :] + _dot(vt_ref[0, c, hv, :], p.astype(BF16))

    m_ref[...] = jnp.full(m_ref.shape, NEG_BIAS, F32)
    l_ref[...] = jnp.zeros(l_ref.shape, F32)
    ot_ref[...] = jnp.zeros(ot_ref.shape, F32)
    logits_stage(0, s0_ref, c0_ref)

    def p4(i, carry):
        logits_stage(2 * i + 1, s1_ref, c1_ref)
        softmax_stage(2 * i, s0_ref, c0_ref)
        logits_stage(2 * i + 2, s0_ref, c0_ref)
        softmax_stage(2 * i + 1, s1_ref, c1_ref)
        return carry

    lax.fori_loop(0, n_pairs, p4, 0)
    for h in range(ATT_HEADS):
        _, hv = head_slices(h)
        ot_ref[hv, :] = ot_ref[hv, :] / l_ref[h][0:1, :]
    o_ref[0] = ot_ref[...].T.astype(BF16)


def _prompt_att(qt, qit, wit, k4, vt4, ki4, batch, nq, topk):
    tq = Q_TILE
    nc = nq + 1 + SPARE_CHUNKS
    per_batch = lambda shape: pl.BlockSpec((1,) + shape, lambda b, j: (b,) + (0,) * len(shape))
    hs = (ATT_HEADS, K_TILE, tq)
    return pl.pallas_call(
        functools.partial(_prompt_att_kernel, topk=topk, nq=nq),
        grid=(batch, nq),
        in_specs=[
            pl.BlockSpec((1, ATT_HEADS, 2 * ATT_DIM, tq), lambda b, j: (b * nq + j, 0, 0, 0)),
            pl.BlockSpec((1, IDX_DIM, IDX_HEADS * tq), lambda b, j: (b * nq + j, 0, 0)),
            pl.BlockSpec((1, IDX_HEADS, tq), lambda b, j: (b, 0, j)),
            per_batch((nc, K_TILE, ATT_WIDTH)),
            per_batch((nc, ATT_WIDTH, K_TILE)),
            per_batch((nc, K_TILE, IDX_DIM)),
        ],
        out_specs=pl.BlockSpec((1, tq, ATT_WIDTH), lambda b, j: (b * nq + j, 0, 0)),
        out_shape=jax.ShapeDtypeStruct((batch * nq, tq, ATT_WIDTH), BF16),
        scratch_shapes=[pltpu.VMEM((nc * K_TILE, tq), F32), pltpu.VMEM((ATT_WIDTH, tq), F32),
                        pltpu.VMEM((ATT_HEADS, 8, tq), F32), pltpu.VMEM((ATT_HEADS, 8, tq), F32),
                        pltpu.VMEM(hs, F32), pltpu.VMEM(hs, F32),
                        pltpu.VMEM((ATT_HEADS, 8, tq), F32), pltpu.VMEM((ATT_HEADS, 8, tq), F32)],
        compiler_params=pltpu.CompilerParams(
            dimension_semantics=("arbitrary", "arbitrary"), vmem_limit_bytes=VMEM_LIMIT),
        name="prompt_att",
    )(qt, qit, wit, k4, vt4, ki4)


def _ret_kernel(rq_ref, rk_ref, rv_ref, g_ref, s0_ref, decay_ref, xi_ref, zeta_ref, gl_ref,
                y_ref, sfin_ref, state_ref):
    t = pl.program_id(1)

    @pl.when(t == 0)
    def _():
        state_ref[...] = s0_ref[0]

    for h in range(RET_HEADS):
        sl = slice(h * RET_DK, (h + 1) * RET_DK)
        q = rq_ref[0, :, sl]
        k = rk_ref[0, :, sl]
        v = rv_ref[0, :, sl]
        sp = state_ref[h]
        inner = _dot_nt(q, k) * decay_ref[h]
        ret = _dot(inner.astype(BF16), v) + _dot(q, sp.astype(BF16)) * xi_ref[h]
        kz = (k.astype(F32) * zeta_ref[h]).T.astype(BF16)
        state_ref[h] = gl_ref[h] * sp + _dot(kz, v)
        mu = jnp.mean(ret, axis=-1, keepdims=True)
        d = ret - mu
        var = jnp.mean(d * d, axis=-1, keepdims=True)
        retn = d * lax.rsqrt(var + GN_EPS)
        g = g_ref[0, :, sl]
        y_ref[0, :, sl] = ((g * jax.nn.sigmoid(g)) * retn).astype(BF16)

    @pl.when(t == pl.num_programs(1) - 1)
    def _():
        sfin_ref[0] = state_ref[...]


def _retention(rq, rk, rv, g, s0, tables, shared_s0):
    batch, tokens, _ = rq.shape
    nchunk = tokens // RET_CHUNK
    decay, xi, zeta, gl = tables
    tok = pl.BlockSpec((1, RET_CHUNK, RET_WIDTH), lambda b, t: (b, t, 0))
    hh = (RET_HEADS, RET_DK, RET_DV)
    s0_spec = pl.BlockSpec((1,) + hh, (lambda b, t: (0, 0, 0, 0)) if shared_s0 else (lambda b, t: (b, 0, 0, 0)))
    return pl.pallas_call(
        _ret_kernel,
        grid=(batch, nchunk),
        in_specs=[tok, tok, tok, tok, s0_spec,
                  _const_spec(hh), _const_spec(hh), _const_spec(hh), _const_spec((RET_HEADS, 1, RET_DV))],
        out_specs=(tok, pl.BlockSpec((1,) + hh, lambda b, t: (b, 0, 0, 0))),
        out_shape=(jax.ShapeDtypeStruct((batch, tokens, RET_WIDTH), BF16),
                   jax.ShapeDtypeStruct((batch,) + hh, F32)),
        scratch_shapes=[pltpu.VMEM(hh, F32)],
        compiler_params=pltpu.CompilerParams(dimension_semantics=("arbitrary", "arbitrary")),
        name="retention",
    )(rq, rk, rv, g, s0, decay, xi, zeta, gl)


def _ret_tables(length):
    lg = jnp.log(1.0 - 2.0 ** (-5.0 - jnp.arange(RET_HEADS, dtype=F32)))
    n = jnp.arange(RET_CHUNK, dtype=F32)
    live = n < length
    diff = n[:, None] - n[None, :]
    ok = (diff >= 0) & live[:, None] & live[None, :]
    decay = jnp.where(ok[None], jnp.exp(jnp.maximum(diff, 0.0)[None] * lg[:, None, None]), 0.0)
    xi = jnp.exp((n[None, :] + 1.0) * lg[:, None])
    zeta = jnp.where(live[None, :], jnp.exp((length - 1.0 - n)[None, :] * lg[:, None]), 0.0)
    bc = lambda a: jnp.broadcast_to(a[:, :, None], (RET_HEADS, RET_CHUNK, RET_DV)).astype(F32)
    gl = jnp.broadcast_to(jnp.exp(length * lg)[:, None, None], (RET_HEADS, 1, RET_DV)).astype(F32)
    return decay.astype(F32), bc(xi), bc(zeta), gl


def _sample_idx_kernel(pt_ref, qi_ref, wb_ref, kin_ref, *rest, n_steps, topk, dseq):
    pages = rest[:PAGES_PER_STEP]
    bias_ref = rest[PAGES_PER_STEP]
    sc_ref = rest[PAGES_PER_STEP + 1]
    pc = pl.program_id(1)
    n_chunks = n_steps * PAGES_PER_STEP + 1
    qi = qi_ref[0]
    wb = wb_ref[0]

    def scores(ki_t):
        s = jnp.maximum(_dot(qi, ki_t), 0.0) * wb
        return jnp.sum(s.reshape(IDX_HEADS, 8, LANES), axis=0)

    for p in range(PAGES_PER_STEP):
        sc_ref[pc * PAGES_PER_STEP + p] = scores(pages[p][0, 0].astype(BF16))

    @pl.when(pc == n_steps - 1)
    def _():
        row = lax.broadcasted_iota(I32, (8, LANES), 0)
        lane = lax.broadcasted_iota(I32, (8, LANES), 1)
        vis = (lane <= row) & (lane < dseq)
        sc_ref[n_chunks - 1] = jnp.where(vis, scores(kin_ref[0]), -jnp.inf)
        slot0 = lax.broadcasted_iota(I32, (n_chunks, 8, LANES), 0) * LANES

        def count(ind_fn):
            return jnp.sum(jnp.sum(ind_fn(sc_ref[...], slot0), axis=0), axis=1, keepdims=True)

        real_query = lax.broadcasted_iota(I32, (8, 1), 0) < dseq
        t_u = _kth_largest(count, topk, (8, 1))
        short = t_u == 0
        thr = jnp.where(short, -jnp.inf, _pattern_to_float(t_u))
        cnt_gt = count(lambda s, base: jnp.where(s > thr, 1, 0))
        cnt_eq = count(lambda s, base: jnp.where(s == thr, 1, 0))
        need = topk - cnt_gt
        nbits = int(n_chunks * LANES - 1).bit_length()
        has_ties = jnp.max(jnp.where(real_query, jnp.where(short, 0, cnt_eq - need), 0)) > 0
        last = lax.cond(has_ties, _tie_break(count, thr, need, nbits, 2, (8, 1)),
                        lambda _: jnp.full((8, 1), (1 << nbits) - 1, I32), 0)
        last = jnp.where(short, -1, last)
        s = sc_ref[...]
        slot = slot0 + lax.broadcasted_iota(I32, s.shape, 2)
        keep_eq = jnp.where(slot <= last, 0.0, NEG_BIAS)
        bias_ref[0] = jnp.where(s > thr, 0.0, jnp.where(s == thr, keep_eq, NEG_BIAS))


def _page_specs(n, rows, page_table_cols):
    def spec(p):
        return pl.BlockSpec((1, 1, rows, PAGE_SIZE),
                            lambda b, pc, pt: (0, pt[b * page_table_cols + pc * PAGES_PER_STEP + p], 0, 0))
    return [spec(p) for p in range(n)]


def _sample_idx(pt_flat, qi_blk, wb, ki_new_t, cache_idx_t, n_pages, topk, dseq):
    nb = qi_blk.shape[0]
    n_steps = n_pages // PAGES_PER_STEP
    n_chunks = n_pages + 1
    grid_spec = pltpu.PrefetchScalarGridSpec(
        num_scalar_prefetch=1,
        grid=(nb, n_steps),
        in_specs=[pl.BlockSpec((1, 64, IDX_DIM), lambda b, pc, pt: (b, 0, 0)),
                  pl.BlockSpec((1, 64, LANES), lambda b, pc, pt: (b, 0, 0)),
                  pl.BlockSpec((1, IDX_DIM, LANES), lambda b, pc, pt: (b, 0, 0)),
                  *_page_specs(PAGES_PER_STEP, IDX_DIM, n_pages)],
        out_specs=pl.BlockSpec((1, n_chunks, 8, LANES), lambda b, pc, pt: (b, 0, 0, 0)),
        scratch_shapes=[pltpu.VMEM((n_chunks, 8, LANES), F32)],
    )
    return pl.pallas_call(
        functools.partial(_sample_idx_kernel, n_steps=n_steps, topk=topk, dseq=dseq),
        grid_spec=grid_spec,
        out_shape=jax.ShapeDtypeStruct((nb, n_chunks, 8, LANES), F32),
        compiler_params=pltpu.CompilerParams(dimension_semantics=("arbitrary", "arbitrary")),
        name="sample_idx",
    )(pt_flat, qi_blk, wb, ki_new_t, *([cache_idx_t] * PAGES_PER_STEP))


def _sample_att_kernel(pt_ref, qb_ref, bias_ref, kn_ref, vn_ref, *rest, n_steps):
    kpages = rest[:PAGES_PER_STEP]
    vpages = rest[PAGES_PER_STEP:2 * PAGES_PER_STEP]
    o_ref, m_ref, l_ref, acc_ref, kbuf, vbuf = rest[2 * PAGES_PER_STEP:]
    pc = pl.program_id(1)
    qb = qb_ref[0]
    n_chunks = n_steps * PAGES_PER_STEP + 1

    def rows64(b8):
        return jnp.concatenate([b8] * ATT_HEADS, axis=0)

    @pl.when(pc == 0)
    def _():
        s = _dot(qb, kn_ref[0]) + rows64(bias_ref[0, n_chunks - 1])
        m = jnp.max(s, axis=1, keepdims=True)
        p = jnp.exp(s - m)
        m_ref[...] = m
        l_ref[...] = jnp.sum(p, axis=1, keepdims=True)
        acc_ref[...] = _dot_nt(p.astype(BF16), vn_ref[0])

    for p in range(PAGES_PER_STEP):
        kbuf[:, p * PAGE_SIZE:(p + 1) * PAGE_SIZE] = kpages[p][0, 0].astype(BF16)
        vbuf[:, p * PAGE_SIZE:(p + 1) * PAGE_SIZE] = vpages[p][0, 0].astype(BF16)
    bias = jnp.concatenate([bias_ref[0, pc * PAGES_PER_STEP + p] for p in range(PAGES_PER_STEP)], axis=1)
    s = _dot(qb, kbuf[...]) + rows64(bias)
    m_old = m_ref[...]
    m_new = jnp.maximum(m_old, jnp.max(s, axis=1, keepdims=True))
    a = jnp.exp(m_old - m_new)
    p = jnp.exp(s - m_new)
    m_ref[...] = m_new
    l_ref[...] = a * l_ref[...] + jnp.sum(p, axis=1, keepdims=True)
    acc_ref[...] = a * acc_ref[...] + _dot_nt(p.astype(BF16), vbuf[...])

    @pl.when(pc == n_steps - 1)
    def _():
        o_ref[0] = acc_ref[...] / l_ref[...]


def _sample_att(pt_flat, qb, bias, k_new_t, v_new_t, cache_k_t, cache_v_t, n_pages):
    nb = qb.shape[0]
    n_steps = n_pages // PAGES_PER_STEP
    n_chunks = n_pages + 1
    rows = ATT_HEADS * 8
    grid_spec = pltpu.PrefetchScalarGridSpec(
        num_scalar_prefetch=1,
        grid=(nb, n_steps),
        in_specs=[pl.BlockSpec((1, rows, ATT_WIDTH), lambda b, pc, pt: (b, 0, 0)),
                  pl.BlockSpec((1, n_chunks, 8, LANES), lambda b, pc, pt: (b, 0, 0, 0)),
                  pl.BlockSpec((1, ATT_WIDTH, LANES), lambda b, pc, pt: (b, 0, 0)),
                  pl.BlockSpec((1, ATT_WIDTH, LANES), lambda b, pc, pt: (b, 0, 0)),
                  *_page_specs(PAGES_PER_STEP, ATT_WIDTH, n_pages),
                  *_page_specs(PAGES_PER_STEP, ATT_WIDTH, n_pages)],
        out_specs=pl.BlockSpec((1, rows, ATT_WIDTH), lambda b, pc, pt: (b, 0, 0)),
        scratch_shapes=[pltpu.VMEM((rows, 1), F32), pltpu.VMEM((rows, 1), F32),
                        pltpu.VMEM((rows, ATT_WIDTH), F32),
                        pltpu.VMEM((ATT_WIDTH, PAGES_PER_STEP * PAGE_SIZE), BF16),
                        pltpu.VMEM((ATT_WIDTH, PAGES_PER_STEP * PAGE_SIZE), BF16)],
    )
    return pl.pallas_call(
        functools.partial(_sample_att_kernel, n_steps=n_steps),
        grid_spec=grid_spec,
        out_shape=jax.ShapeDtypeStruct((nb, rows, ATT_WIDTH), F32),
        compiler_params=pltpu.CompilerParams(dimension_semantics=("arbitrary", "arbitrary")),
        name="sample_att",
    )(pt_flat, qb, bias, k_new_t, v_new_t, *([cache_k_t] * PAGES_PER_STEP), *([cache_v_t] * PAGES_PER_STEP))


def _rope_tables(pos):
    half = RET_DK // 2
    inv = ROPE_BASE ** (-jnp.arange(half, dtype=F32) / half)
    ang = pos.astype(F32)[:, None] * inv[None, :]
    cos, sin = jnp.cos(ang), jnp.sin(ang)
    return jnp.concatenate([cos, cos], axis=1), jnp.concatenate([-sin, sin], axis=1)


def _pad_rows(a, rows):
    return jnp.pad(a, ((0, rows - a.shape[0]),) + ((0, 0),) * (a.ndim - 1))


def _pad_last(a, n):
    return jnp.pad(a, ((0, 0),) * (a.ndim - 1) + ((0, n - a.shape[-1]),))


def kernel(x_prompt, x_sample, cache_k, cache_v, cache_idx_k, state_ret, page_table, meta_tokens,
           ffn1_w_gate, ffn1_w_up, ffn1_w_down, ln1_g, ln1_b, w_in, w_out, ln2_g, ln2_b,
           ffn2_w_gate, ffn2_w_up, ffn2_w_down, ln3_g, ln3_b):
    batch, seq, d = x_prompt.shape
    nb, dseq, _ = x_sample.shape
    n_pages = page_table.shape[1]
    n_pool = cache_k.shape[1]
    past = n_pages * PAGE_SIZE
    nq = seq // Q_TILE
    assert d == D_MODEL and seq % Q_TILE == 0 and Q_TILE == ROW_TILE == K_TILE
    assert dseq <= 8 and n_pages % PAGES_PER_STEP == 0
    ns = nb * dseq
    n_small = ns + N_META
    assert n_small <= ROW_TILE
    meta = slice(ns, n_small)

    bf = lambda a: a.astype(BF16)
    l = 0
    f1 = (bf(ffn1_w_gate[l]), bf(ffn1_w_up[l]), bf(ffn1_w_down[l]))
    f2 = (bf(ffn2_w_gate[l]), bf(ffn2_w_up[l]), bf(ffn2_w_down[l]))
    vec = lambda a: a[l][None, :].astype(F32)
    w = w_in[l]
    offs = np.cumsum([0, 512, 512, 512, 512, 64, 8, 512, 512, 512, 512])
    wq, wk, wv, wqi, wki, wwi, wrq, wrk, wrv, wg = [w[:, offs[i]:offs[i + 1]] for i in range(10)]
    kiw = jnp.concatenate([wki, wwi, jnp.zeros((d, LANES - IDX_DIM - IDX_HEADS), w.dtype)], axis=1)
    wrow = bf(jnp.concatenate([wk, kiw, wrq, wrk, wrv, wg], axis=1))
    wt = bf(jnp.concatenate([wq, wk, wv, wqi, kiw], axis=1).T)
    woa, wor = bf(w_out[l][:ATT_WIDTH]), bf(w_out[l][ATT_WIDTH:])

    xs = _pad_rows(jnp.concatenate([x_sample.reshape(ns, d), meta_tokens.astype(x_prompt.dtype)], axis=0), ROW_TILE)
    xp = x_prompt.reshape(batch * seq, d)
    pos_small = jnp.concatenate([jnp.tile(past + jnp.arange(dseq, dtype=I32), nb),
                                 jnp.arange(N_META, dtype=I32),
                                 jnp.zeros((ROW_TILE - n_small,), I32)])
    cos_s, sin_s = _rope_tables(pos_small)
    cos_p, sin_p = _rope_tables(N_META + jnp.arange(seq, dtype=I32))

    x1p = _ffn_ln(xp, *f1, vec(ln1_g), vec(ln1_b))
    x1s = _ffn_ln(xs, *f1, vec(ln1_g), vec(ln1_b))
    (kb_p, kib_p, rq_p, rk_p, rv_p, g_p, qt_p, ktf_p, vtf_p, vt_p, qit_p, kiwt_p) = _proj(
        x1p, wrow, wt, cos_p, sin_p, batch, nq)
    (kb_s, kib_s, rq_s, rk_s, rv_s, g_s, qt_s, ktf_s, vtf_s, vt_s, qit_s, kiwt_s) = _proj(
        x1s, wrow, wt, cos_s, sin_s, 1, 1)
    ktf_s, vtf_s, kiwt_s = ktf_s[0], vtf_s[0], kiwt_s[0]

    topk_p = min(TOPK_MAX, seq // 4)
    def key_chunks(meta_chunk, real):
        m = jnp.broadcast_to(meta_chunk[None, None], (batch, 1) + meta_chunk.shape)
        z = jnp.zeros((batch, SPARE_CHUNKS) + meta_chunk.shape, meta_chunk.dtype)
        return jnp.concatenate([m, real.reshape((batch, nq) + meta_chunk.shape), z], axis=1)

    k4 = key_chunks(_pad_rows(kb_s[meta], K_TILE), kb_p)
    ki4 = key_chunks(_pad_rows(kib_s[meta], K_TILE), kib_p)
    vt4 = key_chunks(_pad_last(vt_s[0][:, meta], K_TILE), vt_p)
    att_p = _prompt_att(qt_p, qit_p, kiwt_p[:, IDX_DIM:IDX_DIM + IDX_HEADS, :], k4, vt4, ki4, batch, nq, topk_p)
    att_p = att_p.reshape(batch * seq, ATT_WIDTH)

    pad_tok = lambda a, n: jnp.pad(a, ((0, 0), (0, RET_CHUNK - n), (0, 0)))
    m3 = lambda a: pad_tok(a[meta][None], N_META)
    zero_state = jnp.zeros((1, RET_HEADS, RET_DK, RET_DV), F32)
    _, s_meta = _retention(m3(rq_s), m3(rk_s), m3(rv_s), m3(g_s), zero_state, _ret_tables(N_META), True)
    b3 = lambda a: a.reshape(batch, seq, RET_WIDTH)
    yret_p, ret_prompt = _retention(b3(rq_p), b3(rk_p), b3(rv_p), b3(g_p), s_meta, _ret_tables(RET_CHUNK), True)
    s3 = lambda a: pad_tok(a[:ns].reshape(nb, dseq, RET_WIDTH), dseq)
    yret_s, ret_sample = _retention(s3(rq_s), s3(rk_s), s3(rv_s), s3(g_s), state_ret[l], _ret_tables(dseq), False)

    topk_s = min(TOPK_MAX, (past + dseq) // 4)
    pt_flat = page_table.reshape(-1).astype(I32)
    cache_k_t = jnp.transpose(cache_k[l], (0, 2, 3, 1)).reshape(1, n_pool, ATT_WIDTH, PAGE_SIZE)
    cache_v_t = jnp.transpose(cache_v[l], (0, 2, 3, 1)).reshape(1, n_pool, ATT_WIDTH, PAGE_SIZE)
    cache_idx_t = jnp.transpose(cache_idx_k[l], (0, 2, 1))[None]
    q_rows = jnp.concatenate([qt_s[0, h, (h % 2) * ATT_DIM:(h % 2 + 1) * ATT_DIM, :] for h in range(ATT_HEADS)],
                             axis=0).T[:ns]
    qi_rows = qit_s[0].reshape(IDX_DIM, IDX_HEADS, ROW_TILE)[:, :, :ns]
    qi_blk = jnp.pad(qi_rows.transpose(2, 1, 0).reshape(nb, dseq, IDX_HEADS, IDX_DIM).transpose(0, 2, 1, 3),
                     ((0, 0), (0, 0), (0, 8 - dseq), (0, 0))).reshape(nb, IDX_HEADS * 8, IDX_DIM)
    wi_s = kiwt_s[IDX_DIM:IDX_DIM + IDX_HEADS, :ns].reshape(IDX_HEADS, nb, dseq)
    wb = jnp.pad(wi_s.transpose(1, 0, 2), ((0, 0), (0, 0), (0, 8 - dseq))).reshape(nb, IDX_HEADS * 8, 1)
    wb = jnp.broadcast_to(wb, (nb, IDX_HEADS * 8, LANES)).astype(F32)
    new_t = lambda a_t: _pad_last(a_t[:, :ns].reshape(a_t.shape[0], nb, dseq).transpose(1, 0, 2), LANES)
    bias_s = _sample_idx(pt_flat, qi_blk, wb, bf(new_t(kiwt_s[:IDX_DIM])), cache_idx_t, n_pages, topk_s, dseq)
    head_mask = (jnp.arange(ATT_WIDTH)[None, :] // ATT_DIM == jnp.arange(ATT_HEADS)[:, None])
    q3 = jnp.pad(q_rows.reshape(nb, dseq, ATT_WIDTH), ((0, 0), (0, 8 - dseq), (0, 0)))
    qb = jnp.where(head_mask[None, :, None, :], q3[:, None, :, :], 0).astype(BF16).reshape(nb, ATT_HEADS * 8, ATT_WIDTH)
    att_s64 = _sample_att(pt_flat, qb, bias_s, bf(new_t(ktf_s)), bf(new_t(vtf_s)), cache_k_t, cache_v_t, n_pages)
    a5 = att_s64.reshape(nb, ATT_HEADS, 8, ATT_HEADS, ATT_DIM)
    att_s = jnp.stack([a5[:, h, :dseq, h, :] for h in range(ATT_HEADS)], axis=2).reshape(ns, ATT_WIDTH)

    lnw = (vec(ln2_g), vec(ln2_b), *f2, vec(ln3_g), vec(ln3_b))
    y_p = _out_ffn(x1p, att_p, yret_p.reshape(batch * seq, RET_WIDTH), woa, wor, *lnw)
    att_small = _pad_rows(bf(att_s), ROW_TILE)
    yret_small = _pad_rows(yret_s[:, :dseq].reshape(ns, RET_WIDTH), ROW_TILE)
    y_s = _out_ffn(x1s, att_small, yret_small, woa, wor, *lnw)

    def with_meta(real_t, small_t, feat_shape):
        m = jnp.broadcast_to(small_t[None, :, meta], (batch, small_t.shape[0], N_META))
        full = jnp.concatenate([m, real_t], axis=2)
        full = full.reshape((batch,) + feat_shape + (seq + N_META,))
        return jnp.moveaxis(full, -1, 1)[None]

    k_prompt = with_meta(ktf_p, ktf_s, (ATT_HEADS, ATT_DIM))
    v_prompt = with_meta(vtf_p, vtf_s, (ATT_HEADS, ATT_DIM))
    idxk_prompt = with_meta(kiwt_p[:, :IDX_DIM], kiwt_s[:IDX_DIM], (IDX_DIM,))
    y_prompt = y_p.reshape(batch, seq, d)
    y_sample = y_s[:ns].reshape(nb, dseq, d)
    k_sample = ktf_s[:, :ns].T.reshape(1, nb, dseq, ATT_HEADS, ATT_DIM)
    v_sample = vtf_s[:, :ns].T.reshape(1, nb, dseq, ATT_HEADS, ATT_DIM)
    idxk_sample = kiwt_s[:IDX_DIM, :ns].T.reshape(1, nb, dseq, IDX_DIM)
    return (y_prompt, y_sample, k_prompt, v_prompt, idxk_prompt, ret_prompt[None],
            k_sample, v_sample, idxk_sample, ret_sample[None])
```

```python
import functools

import jax
import jax.numpy as jnp
import numpy as np
from jax import lax
from jax.experimental import pallas as pl
from jax.experimental.pallas import tpu as pltpu

F32 = jnp.float32
BF16 = jnp.bfloat16
I32 = jnp.int32

D_MODEL = 1024
N_META = 16
ATT_DIM = 64
ATT_HEADS = 8
ATT_WIDTH = ATT_HEADS * ATT_DIM
IDX_HEADS = 8
IDX_DIM = 64
TOPK_MAX = 256
RET_HEADS = 4
RET_DK = 128
RET_DV = 128
RET_WIDTH = RET_HEADS * RET_DV
PAGE_SIZE = 128
ROPE_BASE = 10000.0
LN_EPS = 1e-5
GN_EPS = 1e-5
DEPTH = 1
ALPHA = (2.0 * DEPTH) ** 0.25

LANES = 128
ROW_TILE = 256
FFN_TILE = 512
Q_TILE = 256
K_TILE = 256
RET_CHUNK = 128
RET_SEQS_PER_STEP = 2
IDX_PAGES_PER_STEP = 32
ATT_PAGES_PER_STEP = 16
LOG2E = 1.4426950408889634
NEG_BIAS = -1e30
INT_MIN = -2147483648
VMEM_LIMIT = 56 * 1024 * 1024


def _dot(a, b):
    return jnp.dot(a, b, preferred_element_type=F32)


def _dot_nt(a, b):
    return lax.dot_general(a, b, (((1,), (1,)), ((), ())), preferred_element_type=F32)


def _layernorm(y, g, b):
    mu = jnp.mean(y, axis=-1, keepdims=True)
    d = y - mu
    var = jnp.mean(d * d, axis=-1, keepdims=True)
    return d * lax.rsqrt(var + LN_EPS) * g + b


def _swiglu(xb, wg_ref, wu_ref, wd_ref):
    hg = _dot(xb, wg_ref[...])
    hu = _dot(xb, wu_ref[...])
    act = (hg * jax.nn.sigmoid(hg)) * hu
    return _dot(act.astype(BF16), wd_ref[...])


def _ffn_ln_kernel(x_ref, wg_ref, wu_ref, wd_ref, g_ref, b_ref, o_ref):
    x = x_ref[...]
    y = ALPHA * x + 0.5 * _swiglu(x.astype(BF16), wg_ref, wu_ref, wd_ref)
    o_ref[...] = _layernorm(y, g_ref[...], b_ref[...])


def _const_spec(shape):
    return pl.BlockSpec(shape, lambda *_: (0,) * len(shape))


def _weight_spec(shape):
    return pl.BlockSpec(shape, lambda *_: (0,) * len(shape), pipeline_mode=pl.Buffered(1))


def _ffn_ln(x, wg, wu, wd, g, b):
    rows, d = x.shape
    dff = wg.shape[1]
    tm = min(FFN_TILE, rows)
    return pl.pallas_call(
        _ffn_ln_kernel,
        grid=(rows // tm,),
        in_specs=[pl.BlockSpec((tm, d), lambda i: (i, 0)),
                  _weight_spec((d, dff)), _weight_spec((d, dff)), _weight_spec((dff, d)),
                  _const_spec((1, d)), _const_spec((1, d))],
        out_specs=pl.BlockSpec((tm, d), lambda i: (i, 0)),
        out_shape=jax.ShapeDtypeStruct((rows, d), F32),
        compiler_params=pltpu.CompilerParams(
            dimension_semantics=("arbitrary",), vmem_limit_bytes=VMEM_LIMIT),
        name="ffn_ln",
    )(x, wg, wu, wd, g, b)


def _out_ffn_kernel(x1_ref, att_ref, yret_ref, woa_ref, wor_ref, g2_ref, b2_ref,
                    wg_ref, wu_ref, wd_ref, g3_ref, b3_ref, o_ref):
    m = _dot(att_ref[...], woa_ref[...]) + _dot(yret_ref[...], wor_ref[...])
    x2 = _layernorm(ALPHA * x1_ref[...] + m, g2_ref[...], b2_ref[...])
    y = ALPHA * x2 + 0.5 * _swiglu(x2.astype(BF16), wg_ref, wu_ref, wd_ref)
    o_ref[...] = _layernorm(y, g3_ref[...], b3_ref[...])


def _out_ffn(x1, att, yret, woa, wor, g2, b2, wg, wu, wd, g3, b3):
    rows, d = x1.shape
    dff = wg.shape[1]
    tm = min(FFN_TILE, rows)
    row = lambda w: pl.BlockSpec((tm, w), lambda i: (i, 0))
    return pl.pallas_call(
        _out_ffn_kernel,
        grid=(rows // tm,),
        in_specs=[row(d), row(ATT_WIDTH), row(RET_WIDTH),
                  _weight_spec((ATT_WIDTH, d)), _weight_spec((RET_WIDTH, d)),
                  _const_spec((1, d)), _const_spec((1, d)),
                  _weight_spec((d, dff)), _weight_spec((d, dff)), _weight_spec((dff, d)),
                  _const_spec((1, d)), _const_spec((1, d))],
        out_specs=row(d),
        out_shape=jax.ShapeDtypeStruct((rows, d), F32),
        compiler_params=pltpu.CompilerParams(
            dimension_semantics=("arbitrary",), vmem_limit_bytes=VMEM_LIMIT),
        name="out_ffn",
    )(x1, att, yret, woa, wor, g2, b2, wg, wu, wd, g3, b3)


W_ROW_COLS = ATT_WIDTH + LANES + 4 * RET_WIDTH
W_T_ROWS = 4 * ATT_WIDTH + LANES


def _rope(x, cos2, sin2):
    return x * cos2 + pltpu.roll(x, RET_DK // 2, 1) * sin2


def _proj_kernel(x1_ref, wrow_ref, wt_ref, cos_ref, sin_ref,
                 kb_ref, kib_ref, rq_ref, rk_ref, rv_ref, g_ref,
                 qt_ref, ktf_ref, vtf_ref, vt_ref, qit_ref, kiwt_ref):
    xb = x1_ref[...].astype(BF16)
    z = _dot(xb, wrow_ref[...])
    kb_ref[...] = z[:, 0:ATT_WIDTH].astype(BF16)
    kib_ref[...] = z[:, ATT_WIDTH:ATT_WIDTH + IDX_DIM].astype(BF16)
    o = ATT_WIDTH + LANES
    cos2 = cos_ref[...]
    sin2 = sin_ref[...]
    for h in range(RET_HEADS):
        sl = slice(h * RET_DK, (h + 1) * RET_DK)
        rq = z[:, o + h * RET_DK:o + (h + 1) * RET_DK]
        rk = z[:, o + RET_WIDTH + h * RET_DK:o + RET_WIDTH + (h + 1) * RET_DK]
        rq_ref[:, sl] = _rope(rq, cos2, sin2).astype(BF16)
        rk_ref[:, sl] = (_rope(rk, cos2, sin2) * (RET_DK ** -0.5)).astype(BF16)
    rv_ref[...] = z[:, o + 2 * RET_WIDTH:o + 3 * RET_WIDTH].astype(BF16)
    g_ref[...] = z[:, o + 3 * RET_WIDTH:o + 4 * RET_WIDTH]

    zt = _dot_nt(wt_ref[...], xb)
    tm = xb.shape[0]
    zero = jnp.zeros((ATT_DIM, tm), BF16)
    for h in range(ATT_HEADS):
        qh = (zt[h * ATT_DIM:(h + 1) * ATT_DIM, :] * (ATT_DIM ** -0.5 * LOG2E)).astype(BF16)
        qt_ref[0, h] = jnp.concatenate([qh, zero] if h % 2 == 0 else [zero, qh], axis=0)
    ktf_ref[0] = zt[ATT_WIDTH:2 * ATT_WIDTH, :]
    vt = zt[2 * ATT_WIDTH:3 * ATT_WIDTH, :]
    vtf_ref[0] = vt
    vt_ref[0] = vt.astype(BF16)
    for h in range(IDX_HEADS):
        qih = zt[3 * ATT_WIDTH + h * IDX_DIM:3 * ATT_WIDTH + (h + 1) * IDX_DIM, :]
        qit_ref[0, :, h * tm:(h + 1) * tm] = (qih * (IDX_DIM ** -0.5)).astype(BF16)
    kiwt_ref[0] = zt[4 * ATT_WIDTH:4 * ATT_WIDTH + LANES, :]


def _proj(x1, wrow, wt, cos2, sin2, batch, tiles_per_seq):
    rows, d = x1.shape
    tm = ROW_TILE
    nt = rows // tm
    seq = tiles_per_seq * tm
    row = lambda w: pl.BlockSpec((tm, w), lambda i: (i, 0))
    tab = pl.BlockSpec((tm, LANES), lambda i: (i % tiles_per_seq, 0))
    tcol = lambda r: pl.BlockSpec((1, r, tm), lambda i: (i // tiles_per_seq, 0, i % tiles_per_seq))
    out_shape = (
        jax.ShapeDtypeStruct((rows, ATT_WIDTH), BF16),
        jax.ShapeDtypeStruct((rows, IDX_DIM), BF16),
        jax.ShapeDtypeStruct((rows, RET_WIDTH), BF16),
        jax.ShapeDtypeStruct((rows, RET_WIDTH), BF16),
        jax.ShapeDtypeStruct((rows, RET_WIDTH), BF16),
        jax.ShapeDtypeStruct((rows, RET_WIDTH), F32),
        jax.ShapeDtypeStruct((nt, ATT_HEADS, 2 * ATT_DIM, tm), BF16),
        jax.ShapeDtypeStruct((batch, ATT_WIDTH, seq), F32),
        jax.ShapeDtypeStruct((batch, ATT_WIDTH, seq), F32),
        jax.ShapeDtypeStruct((nt, ATT_WIDTH, tm), BF16),
        jax.ShapeDtypeStruct((nt, IDX_DIM, IDX_HEADS * tm), BF16),
        jax.ShapeDtypeStruct((batch, LANES, seq), F32),
    )
    out_specs = (
        row(ATT_WIDTH), row(IDX_DIM), row(RET_WIDTH), row(RET_WIDTH), row(RET_WIDTH), row(RET_WIDTH),
        pl.BlockSpec((1, ATT_HEADS, 2 * ATT_DIM, tm), lambda i: (i, 0, 0, 0)),
        tcol(ATT_WIDTH), tcol(ATT_WIDTH),
        pl.BlockSpec((1, ATT_WIDTH, tm), lambda i: (i, 0, 0)),
        pl.BlockSpec((1, IDX_DIM, IDX_HEADS * tm), lambda i: (i, 0, 0)),
        tcol(LANES),
    )
    return pl.pallas_call(
        _proj_kernel,
        grid=(nt,),
        in_specs=[row(d), _const_spec((d, W_ROW_COLS)), _const_spec((W_T_ROWS, d)), tab, tab],
        out_specs=out_specs,
        out_shape=out_shape,
        compiler_params=pltpu.CompilerParams(
            dimension_semantics=("arbitrary",), vmem_limit_bytes=VMEM_LIMIT),
        name="proj",
    )(x1, wrow, wt, cos2, sin2)


def _pattern_to_float(u):
    key = u ^ INT_MIN
    bits = jnp.where(key >= 0, key, key ^ 0x7FFFFFFF)
    return lax.bitcast_convert_type(bits, F32)


def _kth_largest(count, topk, shape):
    def step(i, t_u):
        cand = t_u | (1 << (31 - i))
        cand_f = _pattern_to_float(cand)
        cnt = count(lambda s, base: jnp.where(s >= cand_f, 1, 0))
        return jnp.where(cnt >= topk, cand, t_u)

    return lax.fori_loop(0, 32, step, jnp.zeros(shape, I32))


def _tie_break(count, thr, need, nbits, slot_axis, shape):
    def search(_):
        def step(i, p):
            cand = p | (1 << (nbits - 1 - i))
            cnt = count(lambda s, base: jnp.where(
                s == thr, jnp.where((base + lax.broadcasted_iota(I32, s.shape, slot_axis)) < cand, 1, 0), 0))
            return jnp.where(cnt < need, cand, p)
        return lax.fori_loop(0, nbits, step, jnp.zeros(shape, I32))

    return search


SPARE_CHUNKS = 4


def _prompt_att_kernel(qt_ref, qit_ref, wit_ref, k_ref, vt_ref, ki_ref, o_ref,
                       sc_ref, ot_ref, m_ref, l_ref, s0_ref, s1_ref, c0_ref, c1_ref, *, topk, nq):
    j = pl.program_id(1)
    tq, tk = Q_TILE, K_TILE
    n_chunks = j + 2
    n_pairs = lax.shift_right_logical(n_chunks + 1, 1)
    col = lax.broadcasted_iota(I32, (1, tq), 1)
    rows_k = lax.broadcasted_iota(I32, (tk, tq), 0)

    def rows_of(c, n=1):
        return pl.ds(pl.multiple_of(c * tk, tk), n * tk)

    def p1(c, carry):
        acc = None
        for h in range(IDX_HEADS):
            s = _dot(ki_ref[0, c], qit_ref[0, :, h * tq:(h + 1) * tq])
            t = jnp.maximum(s, 0.0) * wit_ref[0, h:h + 1, :]
            acc = t if acc is None else acc + t
        last_row = jnp.where(c == 0, N_META - 1, j * tq + col - (c - 1) * tk)
        sc_ref[rows_of(c), :] = jnp.where(rows_k <= last_row, acc, -jnp.inf)
        return carry

    lax.fori_loop(0, n_chunks, p1, 0)
    sc_ref[rows_of(n_chunks, SPARE_CHUNKS), :] = jnp.full((SPARE_CHUNKS * tk, tq), -jnp.inf, F32)

    def count(ind_fn):
        def body(c4, part):
            base = pl.multiple_of(SPARE_CHUNKS * c4 * tk, SPARE_CHUNKS * tk)
            x = ind_fn(sc_ref[pl.ds(base, SPARE_CHUNKS * tk), :], base)
            return part + jnp.sum(x.reshape(SPARE_CHUNKS * tk // 8, 8, tq), axis=0)

        n_groups = lax.shift_right_logical(n_chunks + SPARE_CHUNKS - 1, 2)
        return jnp.sum(lax.fori_loop(0, n_groups, body, jnp.zeros((8, tq), I32)), axis=0, keepdims=True)

    t_u = _kth_largest(count, topk, (1, tq))
    short = t_u == 0
    thr = jnp.where(short, -jnp.inf, _pattern_to_float(t_u))
    cnt_gt = count(lambda s, base: jnp.where(s > thr, 1, 0))
    cnt_eq = count(lambda s, base: jnp.where(s == thr, 1, 0))
    need = topk - cnt_gt
    nbits = int(tk * (nq + 1 + SPARE_CHUNKS) - 1).bit_length()
    has_ties = jnp.max(jnp.where(short, 0, cnt_eq - need)) > 0
    last = lax.cond(has_ties, _tie_break(count, thr, need, nbits, 0, (1, tq)),
                    lambda _: jnp.full((1, tq), (1 << nbits) - 1, I32), 0)
    last = jnp.where(short, -1, last)

    def p3(c, carry):
        s = sc_ref[rows_of(c), :]
        keep_eq = jnp.where(c * tk + rows_k <= last, 0.0, NEG_BIAS)
        sc_ref[rows_of(c), :] = jnp.where(s > thr, 0.0, jnp.where(s == thr, keep_eq, NEG_BIAS))
        return carry

    lax.fori_loop(0, n_chunks + SPARE_CHUNKS, p3, 0)

    def head_slices(h):
        return slice((h // 2) * LANES, (h // 2 + 1) * LANES), slice(h * ATT_DIM, (h + 1) * ATT_DIM)

    rep = lambda x: jnp.broadcast_to(x, (8, tq))

    def logits_stage(c, s_ref, cmax_ref):
        for h in range(ATT_HEADS):
            pair, _ = head_slices(h)
            s = _dot(k_ref[0, c, :, pair], qt_ref[0, h]) + sc_ref[rows_of(c), :]
            s_ref[h] = s
            cmax_ref[h] = rep(jnp.max(s, axis=0, keepdims=True))

    def softmax_stage(c, s_ref, cmax_ref):
        for h in range(ATT_HEADS):
            _, hv = head_slices(h)
            m_old = m_ref[h]
            m_new = jnp.maximum(m_old, cmax_ref[h])
            a = jnp.exp2(m_old - m_new)
            p = jnp.exp2(s_ref[h] - m_new[0:1, :])
            m_ref[h] = m_new
            l_ref[h] = a * l_ref[h] + rep(jnp.sum(p, axis=0, keepdims=True))
            ot_ref[hv, :] = a[0:1, :] * ot_ref[hv, :] + _dot(vt_ref[0, c, hv, :], p.astype(BF16))

    m_ref[...] = jnp.full(m_ref.shape, NEG_BIAS, F32)
    l_ref[...] = jnp.zeros(l_ref.shape, F32)
    ot_ref[...] = jnp.zeros(ot_ref.shape, F32)
    logits_stage(0, s0_ref, c0_ref)

    def p4(i, carry):
        logits_stage(2 * i + 1, s1_ref, c1_ref)
        softmax_stage(2 * i, s0_ref, c0_ref)
        logits_stage(2 * i + 2, s0_ref, c0_ref)
        softmax_stage(2 * i + 1, s1_ref, c1_ref)
        return carry

    lax.fori_loop(0, n_pairs, p4, 0)
    for h in range(ATT_HEADS):
        _, hv = head_slices(h)
        ot_ref[hv, :] = ot_ref[hv, :] / l_ref[h][0:1, :]
    o_ref[0] = ot_ref[...].T.astype(BF16)


def _prompt_att(qt, qit, wit, k4, vt4, ki4, batch, nq, topk):
    tq = Q_TILE
    nc = nq + 1 + SPARE_CHUNKS
    per_batch = lambda shape: pl.BlockSpec((1,) + shape, lambda b, j: (b,) + (0,) * len(shape))
    hs = (ATT_HEADS, K_TILE, tq)
    return pl.pallas_call(
        functools.partial(_prompt_att_kernel, topk=topk, nq=nq),
        grid=(batch, nq),
        in_specs=[
            pl.BlockSpec((1, ATT_HEADS, 2 * ATT_DIM, tq), lambda b, j: (b * nq + j, 0, 0, 0)),
            pl.BlockSpec((1, IDX_DIM, IDX_HEADS * tq), lambda b, j: (b * nq + j, 0, 0)),
            pl.BlockSpec((1, IDX_HEADS, tq), lambda b, j: (b, 0, j)),
            per_batch((nc, K_TILE, ATT_WIDTH)),
            per_batch((nc, ATT_WIDTH, K_TILE)),
            per_batch((nc, K_TILE, IDX_DIM)),
        ],
        out_specs=pl.BlockSpec((1, tq, ATT_WIDTH), lambda b, j: (b * nq + j, 0, 0)),
        out_shape=jax.ShapeDtypeStruct((batch * nq, tq, ATT_WIDTH), BF16),
        scratch_shapes=[pltpu.VMEM((nc * K_TILE, tq), F32), pltpu.VMEM((ATT_WIDTH, tq), F32),
                        pltpu.VMEM((ATT_HEADS, 8, tq), F32), pltpu.VMEM((ATT_HEADS, 8, tq), F32),
                        pltpu.VMEM(hs, F32), pltpu.VMEM(hs, F32),
                        pltpu.VMEM((ATT_HEADS, 8, tq), F32), pltpu.VMEM((ATT_HEADS, 8, tq), F32)],
        compiler_params=pltpu.CompilerParams(
            dimension_semantics=("arbitrary", "arbitrary"), vmem_limit_bytes=VMEM_LIMIT),
        name="prompt_att",
    )(qt, qit, wit, k4, vt4, ki4)


def _ret_kernel(rq_ref, rk_ref, rv_ref, g_ref, s0_ref, decay_ref, xi_ref, zeta_ref, gl_ref,
                y_ref, sfin_ref, state_ref, *, shared_s0):
    t = pl.program_id(1)
    nseq = state_ref.shape[0]

    @pl.when(t == 0)
    def _():
        for r in range(nseq):
            state_ref[r] = s0_ref[0 if shared_s0 else r]

    for r in range(nseq):
        for h in range(RET_HEADS):
            sl = slice(h * RET_DK, (h + 1) * RET_DK)
            q = rq_ref[r, :, sl]
            k = rk_ref[r, :, sl]
            v = rv_ref[r, :, sl]
            sp = state_ref[r, h]
            inner = _dot_nt(q, k) * decay_ref[h]
            ret = _dot(inner.astype(BF16), v) + _dot(q, sp.astype(BF16)) * xi_ref[h]
            kz = (k.astype(F32) * zeta_ref[h]).T.astype(BF16)
            state_ref[r, h] = gl_ref[h] * sp + _dot(kz, v)
            mu = jnp.mean(ret, axis=-1, keepdims=True)
            d = ret - mu
            var = jnp.mean(d * d, axis=-1, keepdims=True)
            retn = d * lax.rsqrt(var + GN_EPS)
            g = g_ref[r, :, sl]
            y_ref[r, :, sl] = ((g * jax.nn.sigmoid(g)) * retn).astype(BF16)

    @pl.when(t == pl.num_programs(1) - 1)
    def _():
        sfin_ref[...] = state_ref[...]


def _retention(rq, rk, rv, g, s0, tables, shared_s0):
    batch, tokens, _ = rq.shape
    nchunk = tokens // RET_CHUNK
    nseq = RET_SEQS_PER_STEP if batch % RET_SEQS_PER_STEP == 0 else 1
    decay, xi, zeta, gl = tables
    tok = pl.BlockSpec((nseq, RET_CHUNK, RET_WIDTH), lambda b, t: (b, t, 0))
    hh = (RET_HEADS, RET_DK, RET_DV)
    s0_spec = (pl.BlockSpec((1,) + hh, lambda b, t: (0, 0, 0, 0)) if shared_s0 else
               pl.BlockSpec((nseq,) + hh, lambda b, t: (b, 0, 0, 0)))
    return pl.pallas_call(
        functools.partial(_ret_kernel, shared_s0=shared_s0),
        grid=(batch // nseq, nchunk),
        in_specs=[tok, tok, tok, tok, s0_spec,
                  _const_spec(hh), _const_spec(hh), _const_spec(hh), _const_spec((RET_HEADS, 1, RET_DV))],
        out_specs=(tok, pl.BlockSpec((nseq,) + hh, lambda b, t: (b, 0, 0, 0))),
        out_shape=(jax.ShapeDtypeStruct((batch, tokens, RET_WIDTH), BF16),
                   jax.ShapeDtypeStruct((batch,) + hh, F32)),
        scratch_shapes=[pltpu.VMEM((nseq,) + hh, F32)],
        compiler_params=pltpu.CompilerParams(dimension_semantics=("arbitrary", "arbitrary")),
        name="retention",
    )(rq, rk, rv, g, s0, decay, xi, zeta, gl)


def _ret_tables(length):
    lg = jnp.log(1.0 - 2.0 ** (-5.0 - jnp.arange(RET_HEADS, dtype=F32)))
    n = jnp.arange(RET_CHUNK, dtype=F32)
    live = n < length
    diff = n[:, None] - n[None, :]
    ok = (diff >= 0) & live[:, None] & live[None, :]
    decay = jnp.where(ok[None], jnp.exp(jnp.maximum(diff, 0.0)[None] * lg[:, None, None]), 0.0)
    xi = jnp.exp((n[None, :] + 1.0) * lg[:, None])
    zeta = jnp.where(live[None, :], jnp.exp((length - 1.0 - n)[None, :] * lg[:, None]), 0.0)
    bc = lambda a: jnp.broadcast_to(a[:, :, None], (RET_HEADS, RET_CHUNK, RET_DV)).astype(F32)
    gl = jnp.broadcast_to(jnp.exp(length * lg)[:, None, None], (RET_HEADS, 1, RET_DV)).astype(F32)
    return decay.astype(F32), bc(xi), bc(zeta), gl


def _sample_idx_kernel(pt_ref, qi_ref, wb_ref, kin_ref, *rest, n_steps, topk, dseq, pps):
    pages = rest[:pps]
    bias_ref = rest[pps]
    sc_ref = rest[pps + 1]
    pc = pl.program_id(1)
    n_chunks = n_steps * pps + 1
    qi = qi_ref[0]
    wb = wb_ref[0]

    def scores(ki_t):
        s = jnp.maximum(_dot(qi, ki_t), 0.0) * wb
        return jnp.sum(s.reshape(IDX_HEADS, 8, LANES), axis=0)

    for p in range(pps):
        sc_ref[pc * pps + p] = scores(pages[p][0, 0].astype(BF16))

    @pl.when(pc == n_steps - 1)
    def _():
        row = lax.broadcasted_iota(I32, (8, LANES), 0)
        lane = lax.broadcasted_iota(I32, (8, LANES), 1)
        vis = (lane <= row) & (lane < dseq)
        sc_ref[n_chunks - 1] = jnp.where(vis, scores(kin_ref[0]), -jnp.inf)
        slot0 = lax.broadcasted_iota(I32, (n_chunks, 8, LANES), 0) * LANES

        def count(ind_fn):
            return jnp.sum(jnp.sum(ind_fn(sc_ref[...], slot0), axis=0), axis=1, keepdims=True)

        real_query = lax.broadcasted_iota(I32, (8, 1), 0) < dseq
        t_u = _kth_largest(count, topk, (8, 1))
        short = t_u == 0
        thr = jnp.where(short, -jnp.inf, _pattern_to_float(t_u))
        cnt_gt = count(lambda s, base: jnp.where(s > thr, 1, 0))
        cnt_eq = count(lambda s, base: jnp.where(s == thr, 1, 0))
        need = topk - cnt_gt
        nbits = int(n_chunks * LANES - 1).bit_length()
        has_ties = jnp.max(jnp.where(real_query, jnp.where(short, 0, cnt_eq - need), 0)) > 0
        last = lax.cond(has_ties, _tie_break(count, thr, need, nbits, 2, (8, 1)),
                        lambda _: jnp.full((8, 1), (1 << nbits) - 1, I32), 0)
        last = jnp.where(short, -1, last)
        s = sc_ref[...]
        slot = slot0 + lax.broadcasted_iota(I32, s.shape, 2)
        keep_eq = jnp.where(slot <= last, 0.0, NEG_BIAS)
        bias_ref[0] = jnp.where(s > thr, 0.0, jnp.where(s == thr, keep_eq, NEG_BIAS))


def _page_specs(pps, rows, page_table_cols):
    def spec(p):
        return pl.BlockSpec((1, 1, rows, PAGE_SIZE),
                            lambda b, pc, pt: (0, pt[b * page_table_cols + pc * pps + p], 0, 0))
    return [spec(p) for p in range(pps)]


def _sample_idx(pt_flat, qi_blk, wb, ki_new_t, cache_idx_t, n_pages, topk, dseq):
    nb = qi_blk.shape[0]
    pps = IDX_PAGES_PER_STEP
    n_steps = n_pages // pps
    n_chunks = n_pages + 1
    grid_spec = pltpu.PrefetchScalarGridSpec(
        num_scalar_prefetch=1,
        grid=(nb, n_steps),
        in_specs=[pl.BlockSpec((1, 64, IDX_DIM), lambda b, pc, pt: (b, 0, 0)),
                  pl.BlockSpec((1, 64, LANES), lambda b, pc, pt: (b, 0, 0)),
                  pl.BlockSpec((1, IDX_DIM, LANES), lambda b, pc, pt: (b, 0, 0)),
                  *_page_specs(pps, IDX_DIM, n_pages)],
        out_specs=pl.BlockSpec((1, n_chunks, 8, LANES), lambda b, pc, pt: (b, 0, 0, 0)),
        scratch_shapes=[pltpu.VMEM((n_chunks, 8, LANES), F32)],
    )
    return pl.pallas_call(
        functools.partial(_sample_idx_kernel, n_steps=n_steps, topk=topk, dseq=dseq, pps=pps),
        grid_spec=grid_spec,
        out_shape=jax.ShapeDtypeStruct((nb, n_chunks, 8, LANES), F32),
        compiler_params=pltpu.CompilerParams(dimension_semantics=("arbitrary", "arbitrary")),
        name="sample_idx",
    )(pt_flat, qi_blk, wb, ki_new_t, *([cache_idx_t] * pps))


def _sample_att_kernel(pt_ref, qb_ref, bias_ref, kn_ref, vn_ref, *rest, n_steps, pps):
    kpages = rest[:pps]
    vpages = rest[pps:2 * pps]
    o_ref, m_ref, l_ref, acc_ref, kbuf, vbuf = rest[2 * pps:]
    pc = pl.program_id(1)
    qb = qb_ref[0]
    n_chunks = n_steps * pps + 1

    def rows64(b8):
        return jnp.concatenate([b8] * ATT_HEADS, axis=0)

    @pl.when(pc == 0)
    def _():
        s = _dot(qb, kn_ref[0]) + rows64(bias_ref[0, n_chunks - 1])
        m = jnp.max(s, axis=1, keepdims=True)
        p = jnp.exp2(s - m)
        m_ref[...] = m
        l_ref[...] = jnp.sum(p, axis=1, keepdims=True)
        acc_ref[...] = _dot_nt(p.astype(BF16), vn_ref[0])

    for p in range(pps):
        kbuf[:, p * PAGE_SIZE:(p + 1) * PAGE_SIZE] = kpages[p][0, 0].astype(BF16)
        vbuf[:, p * PAGE_SIZE:(p + 1) * PAGE_SIZE] = vpages[p][0, 0].astype(BF16)
    bias = jnp.concatenate([bias_ref[0, pc * pps + p] for p in range(pps)], axis=1)
    s = _dot(qb, kbuf[...]) + rows64(bias)
    m_old = m_ref[...]
    m_new = jnp.maximum(m_old, jnp.max(s, axis=1, keepdims=True))
    a = jnp.exp2(m_old - m_new)
    p = jnp.exp2(s - m_new)
    m_ref[...] = m_new
    l_ref[...] = a * l_ref[...] + jnp.sum(p, axis=1, keepdims=True)
    acc_ref[...] = a * acc_ref[...] + _dot_nt(p.astype(BF16), vbuf[...])

    @pl.when(pc == n_steps - 1)
    def _():
        o_ref[0] = acc_ref[...] / l_ref[...]


def _sample_att(pt_flat, qb, bias, k_new_t, v_new_t, cache_k_t, cache_v_t, n_pages):
    nb = qb.shape[0]
    pps = ATT_PAGES_PER_STEP
    n_steps = n_pages // pps
    n_chunks = n_pages + 1
    rows = ATT_HEADS * 8
    grid_spec = pltpu.PrefetchScalarGridSpec(
        num_scalar_prefetch=1,
        grid=(nb, n_steps),
        in_specs=[pl.BlockSpec((1, rows, ATT_WIDTH), lambda b, pc, pt: (b, 0, 0)),
                  pl.BlockSpec((1, n_chunks, 8, LANES), lambda b, pc, pt: (b, 0, 0, 0)),
                  pl.BlockSpec((1, ATT_WIDTH, LANES), lambda b, pc, pt: (b, 0, 0)),
                  pl.BlockSpec((1, ATT_WIDTH, LANES), lambda b, pc, pt: (b, 0, 0)),
                  *_page_specs(pps, ATT_WIDTH, n_pages),
                  *_page_specs(pps, ATT_WIDTH, n_pages)],
        out_specs=pl.BlockSpec((1, rows, ATT_WIDTH), lambda b, pc, pt: (b, 0, 0)),
        scratch_shapes=[pltpu.VMEM((rows, 1), F32), pltpu.VMEM((rows, 1), F32),
                        pltpu.VMEM((rows, ATT_WIDTH), F32),
                        pltpu.VMEM((ATT_WIDTH, pps * PAGE_SIZE), BF16),
                        pltpu.VMEM((ATT_WIDTH, pps * PAGE_SIZE), BF16)],
    )
    return pl.pallas_call(
        functools.partial(_sample_att_kernel, n_steps=n_steps, pps=pps),
        grid_spec=grid_spec,
        out_shape=jax.ShapeDtypeStruct((nb, rows, ATT_WIDTH), F32),
        compiler_params=pltpu.CompilerParams(dimension_semantics=("arbitrary", "arbitrary")),
        name="sample_att",
    )(pt_flat, qb, bias, k_new_t, v_new_t, *([cache_k_t] * pps), *([cache_v_t] * pps))


def _rope_tables(pos):
    half = RET_DK // 2
    inv = ROPE_BASE ** (-jnp.arange(half, dtype=F32) / half)
    ang = pos.astype(F32)[:, None] * inv[None, :]
    cos, sin = jnp.cos(ang), jnp.sin(ang)
    return jnp.concatenate([cos, cos], axis=1), jnp.concatenate([-sin, sin], axis=1)


def _pad_rows(a, rows):
    return jnp.pad(a, ((0, rows - a.shape[0]),) + ((0, 0),) * (a.ndim - 1))


def _pad_last(a, n):
    return jnp.pad(a, ((0, 0),) * (a.ndim - 1) + ((0, n - a.shape[-1]),))


def kernel(x_prompt, x_sample, cache_k, cache_v, cache_idx_k, state_ret, page_table, meta_tokens,
           ffn1_w_gate, ffn1_w_up, ffn1_w_down, ln1_g, ln1_b, w_in, w_out, ln2_g, ln2_b,
           ffn2_w_gate, ffn2_w_up, ffn2_w_down, ln3_g, ln3_b):
    batch, seq, d = x_prompt.shape
    nb, dseq, _ = x_sample.shape
    n_pages = page_table.shape[1]
    n_pool = cache_k.shape[1]
    past = n_pages * PAGE_SIZE
    nq = seq // Q_TILE
    assert d == D_MODEL and seq % Q_TILE == 0 and Q_TILE == ROW_TILE == K_TILE
    assert dseq <= 8 and n_pages % IDX_PAGES_PER_STEP == 0 and n_pages % ATT_PAGES_PER_STEP == 0
    ns = nb * dseq
    n_small = ns + N_META
    assert n_small <= ROW_TILE
    meta = slice(ns, n_small)

    bf = lambda a: a.astype(BF16)
    l = 0
    f1 = (bf(ffn1_w_gate[l]), bf(ffn1_w_up[l]), bf(ffn1_w_down[l]))
    f2 = (bf(ffn2_w_gate[l]), bf(ffn2_w_up[l]), bf(ffn2_w_down[l]))
    vec = lambda a: a[l][None, :].astype(F32)
    w = w_in[l]
    offs = np.cumsum([0, 512, 512, 512, 512, 64, 8, 512, 512, 512, 512])
    wq, wk, wv, wqi, wki, wwi, wrq, wrk, wrv, wg = [w[:, offs[i]:offs[i + 1]] for i in range(10)]
    kiw = jnp.concatenate([wki, wwi, jnp.zeros((d, LANES - IDX_DIM - IDX_HEADS), w.dtype)], axis=1)
    wrow = bf(jnp.concatenate([wk, kiw, wrq, wrk, wrv, wg], axis=1))
    wt = bf(jnp.concatenate([wq, wk, wv, wqi, kiw], axis=1).T)
    woa, wor = bf(w_out[l][:ATT_WIDTH]), bf(w_out[l][ATT_WIDTH:])

    xs = _pad_rows(jnp.concatenate([x_sample.reshape(ns, d), meta_tokens.astype(x_prompt.dtype)], axis=0), ROW_TILE)
    xp = x_prompt.reshape(batch * seq, d)
    pos_small = jnp.concatenate([jnp.tile(past + jnp.arange(dseq, dtype=I32), nb),
                                 jnp.arange(N_META, dtype=I32),
                                 jnp.zeros((ROW_TILE - n_small,), I32)])
    cos_s, sin_s = _rope_tables(pos_small)
    cos_p, sin_p = _rope_tables(N_META + jnp.arange(seq, dtype=I32))

    x1p = _ffn_ln(xp, *f1, vec(ln1_g), vec(ln1_b))
    x1s = _ffn_ln(xs, *f1, vec(ln1_g), vec(ln1_b))
    (kb_p, kib_p, rq_p, rk_p, rv_p, g_p, qt_p, ktf_p, vtf_p, vt_p, qit_p, kiwt_p) = _proj(
        x1p, wrow, wt, cos_p, sin_p, batch, nq)
    (kb_s, kib_s, rq_s, rk_s, rv_s, g_s, qt_s, ktf_s, vtf_s, vt_s, qit_s, kiwt_s) = _proj(
        x1s, wrow, wt, cos_s, sin_s, 1, 1)
    ktf_s, vtf_s, kiwt_s = ktf_s[0], vtf_s[0], kiwt_s[0]

    topk_p = min(TOPK_MAX, seq // 4)
    def key_chunks(meta_chunk, real):
        m = jnp.broadcast_to(meta_chunk[None, None], (batch, 1) + meta_chunk.shape)
        z = jnp.zeros((batch, SPARE_CHUNKS) + meta_chunk.shape, meta_chunk.dtype)
        return jnp.concatenate([m, real.reshape((batch, nq) + meta_chunk.shape), z], axis=1)

    k4 = key_chunks(_pad_rows(kb_s[meta], K_TILE), kb_p)
    ki4 = key_chunks(_pad_rows(kib_s[meta], K_TILE), kib_p)
    vt4 = key_chunks(_pad_last(vt_s[0][:, meta], K_TILE), vt_p)
    att_p = _prompt_att(qt_p, qit_p, kiwt_p[:, IDX_DIM:IDX_DIM + IDX_HEADS, :], k4, vt4, ki4, batch, nq, topk_p)
    att_p = att_p.reshape(batch * seq, ATT_WIDTH)

    pad_tok = lambda a, n: jnp.pad(a, ((0, 0), (0, RET_CHUNK - n), (0, 0)))
    m3 = lambda a: pad_tok(a[meta][None], N_META)
    zero_state = jnp.zeros((1, RET_HEADS, RET_DK, RET_DV), F32)
    _, s_meta = _retention(m3(rq_s), m3(rk_s), m3(rv_s), m3(g_s), zero_state, _ret_tables(N_META), True)
    b3 = lambda a: a.reshape(batch, seq, RET_WIDTH)
    yret_p, ret_prompt = _retention(b3(rq_p), b3(rk_p), b3(rv_p), b3(g_p), s_meta, _ret_tables(RET_CHUNK), True)
    s3 = lambda a: pad_tok(a[:ns].reshape(nb, dseq, RET_WIDTH), dseq)
    yret_s, ret_sample = _retention(s3(rq_s), s3(rk_s), s3(rv_s), s3(g_s), state_ret[l], _ret_tables(dseq), False)

    topk_s = min(TOPK_MAX, (past + dseq) // 4)
    pt_flat = page_table.reshape(-1).astype(I32)
    cache_k_t = jnp.transpose(cache_k[l], (0, 2, 3, 1)).reshape(1, n_pool, ATT_WIDTH, PAGE_SIZE)
    cache_v_t = jnp.transpose(cache_v[l], (0, 2, 3, 1)).reshape(1, n_pool, ATT_WIDTH, PAGE_SIZE)
    cache_idx_t = jnp.transpose(cache_idx_k[l], (0, 2, 1))[None]
    q_rows = jnp.concatenate([qt_s[0, h, (h % 2) * ATT_DIM:(h % 2 + 1) * ATT_DIM, :] for h in range(ATT_HEADS)],
                             axis=0).T[:ns]
    qi_rows = qit_s[0].reshape(IDX_DIM, IDX_HEADS, ROW_TILE)[:, :, :ns]
    qi_blk = jnp.pad(qi_rows.transpose(2, 1, 0).reshape(nb, dseq, IDX_HEADS, IDX_DIM).transpose(0, 2, 1, 3),
                     ((0, 0), (0, 0), (0, 8 - dseq), (0, 0))).reshape(nb, IDX_HEADS * 8, IDX_DIM)
    wi_s = kiwt_s[IDX_DIM:IDX_DIM + IDX_HEADS, :ns].reshape(IDX_HEADS, nb, dseq)
    wb = jnp.pad(wi_s.transpose(1, 0, 2), ((0, 0), (0, 0), (0, 8 - dseq))).reshape(nb, IDX_HEADS * 8, 1)
    wb = jnp.broadcast_to(wb, (nb, IDX_HEADS * 8, LANES)).astype(F32)
    new_t = lambda a_t: _pad_last(a_t[:, :ns].reshape(a_t.shape[0], nb, dseq).transpose(1, 0, 2), LANES)
    bias_s = _sample_idx(pt_flat, qi_blk, wb, bf(new_t(kiwt_s[:IDX_DIM])), cache_idx_t, n_pages, topk_s, dseq)
    head_mask = (jnp.arange(ATT_WIDTH)[None, :] // ATT_DIM == jnp.arange(ATT_HEADS)[:, None])
    q3 = jnp.pad(q_rows.reshape(nb, dseq, ATT_WIDTH), ((0, 0), (0, 8 - dseq), (0, 0)))
    qb = jnp.where(head_mask[None, :, None, :], q3[:, None, :, :], 0).astype(BF16).reshape(nb, ATT_HEADS * 8, ATT_WIDTH)
    att_s64 = _sample_att(pt_flat, qb, bias_s, bf(new_t(ktf_s)), bf(new_t(vtf_s)), cache_k_t, cache_v_t, n_pages)
    a5 = att_s64.reshape(nb, ATT_HEADS, 8, ATT_HEADS, ATT_DIM)
    att_s = jnp.stack([a5[:, h, :dseq, h, :] for h in range(ATT_HEADS)], axis=2).reshape(ns, ATT_WIDTH)

    lnw = (vec(ln2_g), vec(ln2_b), *f2, vec(ln3_g), vec(ln3_b))
    y_p = _out_ffn(x1p, att_p, yret_p.reshape(batch * seq, RET_WIDTH), woa, wor, *lnw)
    att_small = _pad_rows(bf(att_s), ROW_TILE)
    yret_small = _pad_rows(yret_s[:, :dseq].reshape(ns, RET_WIDTH), ROW_TILE)
    y_s = _out_ffn(x1s, att_small, yret_small, woa, wor, *lnw)

    def with_meta(real_t, small_t, feat_shape):
        m = jnp.broadcast_to(small_t[None, :, meta], (batch, small_t.shape[0], N_META))
        full = jnp.concatenate([m, real_t], axis=2)
        full = full.reshape((batch,) + feat_shape + (seq + N_META,))
        return jnp.moveaxis(full, -1, 1)[None]

    k_prompt = with_meta(ktf_p, ktf_s, (ATT_HEADS, ATT_DIM))
    v_prompt = with_meta(vtf_p, vtf_s, (ATT_HEADS, ATT_DIM))
    idxk_prompt = with_meta(kiwt_p[:, :IDX_DIM], kiwt_s[:IDX_DIM], (IDX_DIM,))
    y_prompt = y_p.reshape(batch, seq, d)
    y_sample = y_s[:ns].reshape(nb, dseq, d)
    k_sample = ktf_s[:, :ns].T.reshape(1, nb, dseq, ATT_HEADS, ATT_DIM)
    v_sample = vtf_s[:, :ns].T.reshape(1, nb, dseq, ATT_HEADS, ATT_DIM)
    idxk_sample = kiwt_s[:IDX_DIM, :ns].T.reshape(1, nb, dseq, IDX_DIM)
    return (y_prompt, y_sample, k_prompt, v_prompt, idxk_prompt, ret_prompt[None],
            k_sample, v_sample, idxk_sample, ret_sample[None])
```

```python
import functools

import jax
import jax.numpy as jnp
import numpy as np
from jax import lax
from jax.experimental import pallas as pl
from jax.experimental.pallas import tpu as pltpu

F32 = jnp.float32
BF16 = jnp.bfloat16
I32 = jnp.int32

D_MODEL = 1024
N_META = 16
ATT_DIM = 64
ATT_HEADS = 8
ATT_WIDTH = ATT_HEADS * ATT_DIM
IDX_HEADS = 8
IDX_DIM = 64
TOPK_MAX = 256
RET_HEADS = 4
RET_DK = 128
RET_DV = 128
RET_WIDTH = RET_HEADS * RET_DV
PAGE_SIZE = 128
ROPE_BASE = 10000.0
LN_EPS = 1e-5
GN_EPS = 1e-5
DEPTH = 1
ALPHA = (2.0 * DEPTH) ** 0.25

LANES = 128
ROW_TILE = 256
FFN_TILE = 512
Q_TILE = 256
K_TILE = 256
RET_CHUNK = 128
RET_SEQS_PER_STEP = 2
IDX_PAGES_PER_STEP = 32
ATT_PAGES_PER_STEP = 16
LOG2E = 1.4426950408889634
NEG_BIAS = -1e30
INT_MIN = -2147483648
VMEM_LIMIT = 56 * 1024 * 1024


def _dot(a, b):
    return jnp.dot(a, b, preferred_element_type=F32)


def _dot_nt(a, b):
    return lax.dot_general(a, b, (((1,), (1,)), ((), ())), preferred_element_type=F32)


def _layernorm(y, g, b):
    mu = jnp.mean(y, axis=-1, keepdims=True)
    d = y - mu
    var = jnp.mean(d * d, axis=-1, keepdims=True)
    return d * lax.rsqrt(var + LN_EPS) * g + b


def _swiglu(xb, wg_ref, wu_ref, wd_ref):
    hg = _dot(xb, wg_ref[...])
    hu = _dot(xb, wu_ref[...])
    act = (hg * jax.nn.sigmoid(hg)) * hu
    return _dot(act.astype(BF16), wd_ref[...])


def _ffn_ln_kernel(x_ref, wg_ref, wu_ref, wd_ref, g_ref, b_ref, o_ref):
    x = x_ref[...]
    y = ALPHA * x + 0.5 * _swiglu(x.astype(BF16), wg_ref, wu_ref, wd_ref)
    o_ref[...] = _layernorm(y, g_ref[...], b_ref[...])


def _const_spec(shape):
    return pl.BlockSpec(shape, lambda *_: (0,) * len(shape))


def _weight_spec(shape):
    return pl.BlockSpec(shape, lambda *_: (0,) * len(shape), pipeline_mode=pl.Buffered(1))


def _ffn_ln(x, wg, wu, wd, g, b):
    rows, d = x.shape
    dff = wg.shape[1]
    tm = min(FFN_TILE, rows)
    return pl.pallas_call(
        _ffn_ln_kernel,
        grid=(rows // tm,),
        in_specs=[pl.BlockSpec((tm, d), lambda i: (i, 0)),
                  _weight_spec((d, dff)), _weight_spec((d, dff)), _weight_spec((dff, d)),
                  _const_spec((1, d)), _const_spec((1, d))],
        out_specs=pl.BlockSpec((tm, d), lambda i: (i, 0)),
        out_shape=jax.ShapeDtypeStruct((rows, d), F32),
        compiler_params=pltpu.CompilerParams(
            dimension_semantics=("arbitrary",), vmem_limit_bytes=VMEM_LIMIT),
        name="ffn_ln",
    )(x, wg, wu, wd, g, b)


def _out_ffn_kernel(x1_ref, att_ref, yret_ref, woa_ref, wor_ref, g2_ref, b2_ref,
                    wg_ref, wu_ref, wd_ref, g3_ref, b3_ref, o_ref):
    m = _dot(att_ref[...], woa_ref[...]) + _dot(yret_ref[...], wor_ref[...])
    x2 = _layernorm(ALPHA * x1_ref[...] + m, g2_ref[...], b2_ref[...])
    y = ALPHA * x2 + 0.5 * _swiglu(x2.astype(BF16), wg_ref, wu_ref, wd_ref)
    o_ref[...] = _layernorm(y, g3_ref[...], b3_ref[...])


def _out_ffn(x1, att, yret, woa, wor, g2, b2, wg, wu, wd, g3, b3):
    rows, d = x1.shape
    dff = wg.shape[1]
    tm = min(FFN_TILE, rows)
    row = lambda w: pl.BlockSpec((tm, w), lambda i: (i, 0))
    return pl.pallas_call(
        _out_ffn_kernel,
        grid=(rows // tm,),
        in_specs=[row(d), row(ATT_WIDTH), row(RET_WIDTH),
                  _weight_spec((ATT_WIDTH, d)), _weight_spec((RET_WIDTH, d)),
                  _const_spec((1, d)), _const_spec((1, d)),
                  _weight_spec((d, dff)), _weight_spec((d, dff)), _weight_spec((dff, d)),
                  _const_spec((1, d)), _const_spec((1, d))],
        out_specs=row(d),
        out_shape=jax.ShapeDtypeStruct((rows, d), F32),
        compiler_params=pltpu.CompilerParams(
            dimension_semantics=("arbitrary",), vmem_limit_bytes=VMEM_LIMIT),
        name="out_ffn",
    )(x1, att, yret, woa, wor, g2, b2, wg, wu, wd, g3, b3)


W_ROW_COLS = ATT_WIDTH + LANES + 4 * RET_WIDTH
W_T_ROWS = 4 * ATT_WIDTH + LANES
V_ROWS = ATT_DIM + 16


def _rope(x, cos2, sin2):
    return x * cos2 + pltpu.roll(x, RET_DK // 2, 1) * sin2


def _proj_kernel(x1_ref, wrow_ref, wt_ref, cos_ref, sin_ref,
                 kb_ref, kib_ref, rq_ref, rk_ref, rv_ref, g_ref,
                 qt_ref, ktf_ref, vtf_ref, vt_ref, qit_ref, kiwt_ref):
    xb = x1_ref[...].astype(BF16)
    z = _dot(xb, wrow_ref[...])
    kb_ref[...] = z[:, 0:ATT_WIDTH].astype(BF16)
    kib_ref[...] = z[:, ATT_WIDTH:ATT_WIDTH + IDX_DIM].astype(BF16)
    o = ATT_WIDTH + LANES
    cos2 = cos_ref[...]
    sin2 = sin_ref[...]
    for h in range(RET_HEADS):
        sl = slice(h * RET_DK, (h + 1) * RET_DK)
        rq = z[:, o + h * RET_DK:o + (h + 1) * RET_DK]
        rk = z[:, o + RET_WIDTH + h * RET_DK:o + RET_WIDTH + (h + 1) * RET_DK]
        rq_ref[:, sl] = _rope(rq, cos2, sin2).astype(BF16)
        rk_ref[:, sl] = (_rope(rk, cos2, sin2) * (RET_DK ** -0.5)).astype(BF16)
    rv_ref[...] = z[:, o + 2 * RET_WIDTH:o + 3 * RET_WIDTH].astype(BF16)
    g_ref[...] = z[:, o + 3 * RET_WIDTH:o + 4 * RET_WIDTH]

    zt = _dot_nt(wt_ref[...], xb)
    tm = xb.shape[0]
    zero = jnp.zeros((ATT_DIM, tm), BF16)
    for h in range(ATT_HEADS):
        qh = (zt[h * ATT_DIM:(h + 1) * ATT_DIM, :] * (ATT_DIM ** -0.5 * LOG2E)).astype(BF16)
        qt_ref[0, h] = jnp.concatenate([qh, zero] if h % 2 == 0 else [zero, qh], axis=0)
    ktf_ref[0] = zt[ATT_WIDTH:2 * ATT_WIDTH, :]
    vt = zt[2 * ATT_WIDTH:3 * ATT_WIDTH, :]
    vtf_ref[0] = vt
    ones_blk = jnp.where(lax.broadcasted_iota(I32, (V_ROWS - ATT_DIM, tm), 0) == 0, 1.0, 0.0).astype(BF16)
    for h in range(ATT_HEADS):
        vt_ref[0, h] = jnp.concatenate([vt[h * ATT_DIM:(h + 1) * ATT_DIM, :].astype(BF16), ones_blk], axis=0)
    for h in range(IDX_HEADS):
        qih = zt[3 * ATT_WIDTH + h * IDX_DIM:3 * ATT_WIDTH + (h + 1) * IDX_DIM, :]
        qit_ref[0, :, h * tm:(h + 1) * tm] = (qih * (IDX_DIM ** -0.5)).astype(BF16)
    kiwt_ref[0] = zt[4 * ATT_WIDTH:4 * ATT_WIDTH + LANES, :]


def _proj(x1, wrow, wt, cos2, sin2, batch, tiles_per_seq):
    rows, d = x1.shape
    tm = ROW_TILE
    nt = rows // tm
    seq = tiles_per_seq * tm
    row = lambda w: pl.BlockSpec((tm, w), lambda i: (i, 0))
    tab = pl.BlockSpec((tm, LANES), lambda i: (i % tiles_per_seq, 0))
    tcol = lambda r: pl.BlockSpec((1, r, tm), lambda i: (i // tiles_per_seq, 0, i % tiles_per_seq))
    out_shape = (
        jax.ShapeDtypeStruct((rows, ATT_WIDTH), BF16),
        jax.ShapeDtypeStruct((rows, IDX_DIM), BF16),
        jax.ShapeDtypeStruct((rows, RET_WIDTH), BF16),
        jax.ShapeDtypeStruct((rows, RET_WIDTH), BF16),
        jax.ShapeDtypeStruct((rows, RET_WIDTH), BF16),
        jax.ShapeDtypeStruct((rows, RET_WIDTH), F32),
        jax.ShapeDtypeStruct((nt, ATT_HEADS, 2 * ATT_DIM, tm), BF16),
        jax.ShapeDtypeStruct((batch, ATT_WIDTH, seq), F32),
        jax.ShapeDtypeStruct((batch, ATT_WIDTH, seq), F32),
        jax.ShapeDtypeStruct((nt, ATT_HEADS, V_ROWS, tm), BF16),
        jax.ShapeDtypeStruct((nt, IDX_DIM, IDX_HEADS * tm), BF16),
        jax.ShapeDtypeStruct((batch, LANES, seq), F32),
    )
    out_specs = (
        row(ATT_WIDTH), row(IDX_DIM), row(RET_WIDTH), row(RET_WIDTH), row(RET_WIDTH), row(RET_WIDTH),
        pl.BlockSpec((1, ATT_HEADS, 2 * ATT_DIM, tm), lambda i: (i, 0, 0, 0)),
        tcol(ATT_WIDTH), tcol(ATT_WIDTH),
        pl.BlockSpec((1, ATT_HEADS, V_ROWS, tm), lambda i: (i, 0, 0, 0)),
        pl.BlockSpec((1, IDX_DIM, IDX_HEADS * tm), lambda i: (i, 0, 0)),
        tcol(LANES),
    )
    return pl.pallas_call(
        _proj_kernel,
        grid=(nt,),
        in_specs=[row(d), _const_spec((d, W_ROW_COLS)), _const_spec((W_T_ROWS, d)), tab, tab],
        out_specs=out_specs,
        out_shape=out_shape,
        compiler_params=pltpu.CompilerParams(
            dimension_semantics=("arbitrary",), vmem_limit_bytes=VMEM_LIMIT),
        name="proj",
    )(x1, wrow, wt, cos2, sin2)


def _pattern_to_float(u):
    key = u ^ INT_MIN
    bits = jnp.where(key >= 0, key, key ^ 0x7FFFFFFF)
    return lax.bitcast_convert_type(bits, F32)


def _kth_largest(count, topk, shape, tile_shape):
    def step(i, t_u):
        cand = t_u | (1 << (31 - i))
        cand_f = jnp.broadcast_to(_pattern_to_float(cand), tile_shape)
        cnt = count(lambda s, base: jnp.where(s >= cand_f, 1, 0))
        return jnp.where(cnt >= topk, cand, t_u)

    return lax.fori_loop(0, 32, step, jnp.zeros(shape, I32))


def _tie_break(count, thr, need, nbits, slot_axis, shape):
    def search(_):
        def step(i, p):
            cand = p | (1 << (nbits - 1 - i))
            cnt = count(lambda s, base: jnp.where(
                s == thr, jnp.where((base + lax.broadcasted_iota(I32, s.shape, slot_axis)) < cand, 1, 0), 0))
            return jnp.where(cnt < need, cand, p)
        return lax.fori_loop(0, nbits, step, jnp.zeros(shape, I32))

    return search


SPARE_CHUNKS = 4
COUNT_LANES = 4


def _prompt_att_kernel(qt_ref, qit_ref, wit_ref, k_ref, vt_ref, ki_ref, o_ref,
                       sc_ref, ot_ref, m_ref, s0_ref, s1_ref, c0_ref, c1_ref, *, topk, nq):
    j = pl.program_id(1)
    tq, tk = Q_TILE, K_TILE
    n_chunks = j + 2
    n_pairs = lax.shift_right_logical(n_chunks + 1, 1)
    col = lax.broadcasted_iota(I32, (1, tq), 1)
    rows_k = lax.broadcasted_iota(I32, (tk, tq), 0)

    def rows_of(c, n=1):
        return pl.ds(pl.multiple_of(c * tk, tk), n * tk)

    def p1(c, carry):
        acc = None
        for h in range(IDX_HEADS):
            s = _dot(ki_ref[0, c], qit_ref[0, :, h * tq:(h + 1) * tq])
            t = jnp.maximum(s, 0.0) * wit_ref[0, h:h + 1, :]
            acc = t if acc is None else acc + t
        last_row = jnp.where(c == 0, N_META - 1, j * tq + col - (c - 1) * tk)
        sc_ref[rows_of(c), :] = jnp.where(rows_k <= last_row, acc, -jnp.inf)
        return carry

    lax.fori_loop(0, n_chunks, p1, 0)
    sc_ref[rows_of(n_chunks, SPARE_CHUNKS), :] = jnp.full((SPARE_CHUNKS * tk, tq), -jnp.inf, F32)

    def count(ind_fn):
        def body(c4, part):
            base = pl.multiple_of(SPARE_CHUNKS * c4 * tk, SPARE_CHUNKS * tk)
            x = ind_fn(sc_ref[pl.ds(base, SPARE_CHUNKS * tk), :], base)
            return part + jnp.sum(x.reshape(-1, COUNT_LANES, 8, tq), axis=0)

        n_groups = lax.shift_right_logical(n_chunks + SPARE_CHUNKS - 1, 2)
        part = lax.fori_loop(0, n_groups, body, jnp.zeros((COUNT_LANES, 8, tq), I32))
        return jnp.sum(part.reshape(COUNT_LANES * 8, tq), axis=0, keepdims=True)

    t_u = _kth_largest(count, topk, (1, tq), (1, tq))
    short = t_u == 0
    thr = jnp.where(short, -jnp.inf, _pattern_to_float(t_u))
    cnt_gt = count(lambda s, base: jnp.where(s > thr, 1, 0))
    cnt_eq = count(lambda s, base: jnp.where(s == thr, 1, 0))
    need = topk - cnt_gt
    nbits = int(tk * (nq + 1 + SPARE_CHUNKS) - 1).bit_length()
    has_ties = jnp.max(jnp.where(short, 0, cnt_eq - need)) > 0
    last = lax.cond(has_ties, _tie_break(count, thr, need, nbits, 0, (1, tq)),
                    lambda _: jnp.full((1, tq), (1 << nbits) - 1, I32), 0)
    last = jnp.where(short, -1, last)

    def p3(c, carry):
        s = sc_ref[rows_of(c), :]
        keep_eq = jnp.where(c * tk + rows_k <= last, 0.0, NEG_BIAS)
        sc_ref[rows_of(c), :] = jnp.where(s > thr, 0.0, jnp.where(s == thr, keep_eq, NEG_BIAS))
        return carry

    lax.fori_loop(0, n_chunks + SPARE_CHUNKS, p3, 0)

    def k_lanes(h):
        return slice((h // 2) * LANES, (h // 2 + 1) * LANES)

    rep = lambda x: jnp.broadcast_to(x, (8, tq))

    def logits_stage(c, s_ref, cmax_ref):
        for h in range(ATT_HEADS):
            s = _dot(k_ref[0, c, :, k_lanes(h)], qt_ref[0, h]) + sc_ref[rows_of(c), :]
            s_ref[h] = s
            cmax_ref[h] = rep(jnp.max(s, axis=0, keepdims=True))

    def softmax_stage(c, s_ref, cmax_ref):
        for h in range(ATT_HEADS):
            m_old = m_ref[h]
            m_new = jnp.maximum(m_old, cmax_ref[h])
            a = jnp.exp2(m_old - m_new)
            p = jnp.exp2(s_ref[h] - m_new[0:1, :])
            m_ref[h] = m_new
            ot_ref[h] = a[0:1, :] * ot_ref[h] + _dot(vt_ref[0, c, h], p.astype(BF16))

    m_ref[...] = jnp.full(m_ref.shape, NEG_BIAS, F32)
    ot_ref[...] = jnp.zeros(ot_ref.shape, F32)
    logits_stage(0, s0_ref, c0_ref)

    def p4(i, carry):
        logits_stage(2 * i + 1, s1_ref, c1_ref)
        softmax_stage(2 * i, s0_ref, c0_ref)
        logits_stage(2 * i + 2, s0_ref, c0_ref)
        softmax_stage(2 * i + 1, s1_ref, c1_ref)
        return carry

    lax.fori_loop(0, n_pairs, p4, 0)
    out_t = jnp.concatenate([ot_ref[h, 0:ATT_DIM, :] / ot_ref[h, ATT_DIM:ATT_DIM + 1, :]
                             for h in range(ATT_HEADS)], axis=0)
    o_ref[0] = out_t.T.astype(BF16)


def _prompt_att(qt, qit, wit, k4, vt4, ki4, batch, nq, topk):
    tq = Q_TILE
    nc = nq + 1 + SPARE_CHUNKS
    per_batch = lambda shape: pl.BlockSpec((1,) + shape, lambda b, j: (b,) + (0,) * len(shape))
    hs = (ATT_HEADS, K_TILE, tq)
    return pl.pallas_call(
        functools.partial(_prompt_att_kernel, topk=topk, nq=nq),
        grid=(batch, nq),
        in_specs=[
            pl.BlockSpec((1, ATT_HEADS, 2 * ATT_DIM, tq), lambda b, j: (b * nq + j, 0, 0, 0)),
            pl.BlockSpec((1, IDX_DIM, IDX_HEADS * tq), lambda b, j: (b * nq + j, 0, 0)),
            pl.BlockSpec((1, IDX_HEADS, tq), lambda b, j: (b, 0, j)),
            per_batch((nc, K_TILE, ATT_WIDTH)),
            per_batch((nc, ATT_HEADS, V_ROWS, K_TILE)),
            per_batch((nc, K_TILE, IDX_DIM)),
        ],
        out_specs=pl.BlockSpec((1, tq, ATT_WIDTH), lambda b, j: (b * nq + j, 0, 0)),
        out_shape=jax.ShapeDtypeStruct((batch * nq, tq, ATT_WIDTH), BF16),
        scratch_shapes=[pltpu.VMEM((nc * K_TILE, tq), F32), pltpu.VMEM((ATT_HEADS, V_ROWS, tq), F32),
                        pltpu.VMEM((ATT_HEADS, 8, tq), F32),
                        pltpu.VMEM(hs, F32), pltpu.VMEM(hs, F32),
                        pltpu.VMEM((ATT_HEADS, 8, tq), F32), pltpu.VMEM((ATT_HEADS, 8, tq), F32)],
        compiler_params=pltpu.CompilerParams(
            dimension_semantics=("arbitrary", "arbitrary"), vmem_limit_bytes=VMEM_LIMIT),
        name="prompt_att",
    )(qt, qit, wit, k4, vt4, ki4)


def _ret_kernel(rq_ref, rk_ref, rv_ref, g_ref, s0_ref, decay_ref, xi_ref, zeta_ref, gl_ref,
                y_ref, sfin_ref, state_ref, *, shared_s0):
    t = pl.program_id(1)
    nseq = state_ref.shape[0]

    @pl.when(t == 0)
    def _():
        for r in range(nseq):
            state_ref[r] = s0_ref[0 if shared_s0 else r]

    for r in range(nseq):
        for h in range(RET_HEADS):
            sl = slice(h * RET_DK, (h + 1) * RET_DK)
            q = rq_ref[r, :, sl]
            k = rk_ref[r, :, sl]
            v = rv_ref[r, :, sl]
            sp = state_ref[r, h]
            inner = _dot_nt(q, k) * decay_ref[h]
            ret = _dot(inner.astype(BF16), v) + _dot(q, sp.astype(BF16)) * xi_ref[h]
            kz = (k.astype(F32) * zeta_ref[h]).T.astype(BF16)
            state_ref[r, h] = gl_ref[h] * sp + _dot(kz, v)
            mu = jnp.mean(ret, axis=-1, keepdims=True)
            d = ret - mu
            var = jnp.mean(d * d, axis=-1, keepdims=True)
            retn = d * lax.rsqrt(var + GN_EPS)
            g = g_ref[r, :, sl]
            y_ref[r, :, sl] = ((g * jax.nn.sigmoid(g)) * retn).astype(BF16)

    @pl.when(t == pl.num_programs(1) - 1)
    def _():
        sfin_ref[...] = state_ref[...]


def _retention(rq, rk, rv, g, s0, tables, shared_s0):
    batch, tokens, _ = rq.shape
    nchunk = tokens // RET_CHUNK
    nseq = RET_SEQS_PER_STEP if batch % RET_SEQS_PER_STEP == 0 else 1
    decay, xi, zeta, gl = tables
    tok = pl.BlockSpec((nseq, RET_CHUNK, RET_WIDTH), lambda b, t: (b, t, 0))
    hh = (RET_HEADS, RET_DK, RET_DV)
    s0_spec = (pl.BlockSpec((1,) + hh, lambda b, t: (0, 0, 0, 0)) if shared_s0 else
               pl.BlockSpec((nseq,) + hh, lambda b, t: (b, 0, 0, 0)))
    return pl.pallas_call(
        functools.partial(_ret_kernel, shared_s0=shared_s0),
        grid=(batch // nseq, nchunk),
        in_specs=[tok, tok, tok, tok, s0_spec,
                  _const_spec(hh), _const_spec(hh), _const_spec(hh), _const_spec((RET_HEADS, 1, RET_DV))],
        out_specs=(tok, pl.BlockSpec((nseq,) + hh, lambda b, t: (b, 0, 0, 0))),
        out_shape=(jax.ShapeDtypeStruct((batch, tokens, RET_WIDTH), BF16),
                   jax.ShapeDtypeStruct((batch,) + hh, F32)),
        scratch_shapes=[pltpu.VMEM((nseq,) + hh, F32)],
        compiler_params=pltpu.CompilerParams(dimension_semantics=("arbitrary", "arbitrary")),
        name="retention",
    )(rq, rk, rv, g, s0, decay, xi, zeta, gl)


def _ret_tables(length):
    lg = jnp.log(1.0 - 2.0 ** (-5.0 - jnp.arange(RET_HEADS, dtype=F32)))
    n = jnp.arange(RET_CHUNK, dtype=F32)
    live = n < length
    diff = n[:, None] - n[None, :]
    ok = (diff >= 0) & live[:, None] & live[None, :]
    decay = jnp.where(ok[None], jnp.exp(jnp.maximum(diff, 0.0)[None] * lg[:, None, None]), 0.0)
    xi = jnp.exp((n[None, :] + 1.0) * lg[:, None])
    zeta = jnp.where(live[None, :], jnp.exp((length - 1.0 - n)[None, :] * lg[:, None]), 0.0)
    bc = lambda a: jnp.broadcast_to(a[:, :, None], (RET_HEADS, RET_CHUNK, RET_DV)).astype(F32)
    gl = jnp.broadcast_to(jnp.exp(length * lg)[:, None, None], (RET_HEADS, 1, RET_DV)).astype(F32)
    return decay.astype(F32), bc(xi), bc(zeta), gl


def _sample_idx_kernel(pt_ref, qi_ref, wb_ref, kin_ref, *rest, n_steps, topk, dseq, pps):
    pages = rest[:pps]
    bias_ref = rest[pps]
    sc_ref = rest[pps + 1]
    pc = pl.program_id(1)
    n_chunks = n_steps * pps + 1
    qi = qi_ref[0]
    wb = wb_ref[0]

    def scores(ki_t):
        s = jnp.maximum(_dot(qi, ki_t), 0.0) * wb
        return jnp.sum(s.reshape(IDX_HEADS, 8, LANES), axis=0)

    for p in range(pps):
        sc_ref[pc * pps + p] = scores(pages[p][0, 0].astype(BF16))

    @pl.when(pc == n_steps - 1)
    def _():
        row = lax.broadcasted_iota(I32, (8, LANES), 0)
        lane = lax.broadcasted_iota(I32, (8, LANES), 1)
        vis = (lane <= row) & (lane < dseq)
        sc_ref[n_chunks - 1] = jnp.where(vis, scores(kin_ref[0]), -jnp.inf)
        slot0 = lax.broadcasted_iota(I32, (n_chunks, 8, LANES), 0) * LANES

        def count(ind_fn):
            x = ind_fn(sc_ref[...], slot0).astype(F32)
            acc = [x[i] for i in range(COUNT_LANES)]
            for c in range(COUNT_LANES, n_chunks):
                acc[c % COUNT_LANES] = acc[c % COUNT_LANES] + x[c]
            per_lane = (acc[0] + acc[1]) + (acc[2] + acc[3])
            return jnp.sum(per_lane, axis=1, keepdims=True).astype(I32)

        real_query = lax.broadcasted_iota(I32, (8, 1), 0) < dseq
        t_u = _kth_largest(count, topk, (8, 1), (8, LANES))
        short = t_u == 0
        thr = jnp.where(short, -jnp.inf, _pattern_to_float(t_u))
        cnt_gt = count(lambda s, base: jnp.where(s > thr, 1, 0))
        cnt_eq = count(lambda s, base: jnp.where(s == thr, 1, 0))
        need = topk - cnt_gt
        nbits = int(n_chunks * LANES - 1).bit_length()
        has_ties = jnp.max(jnp.where(real_query, jnp.where(short, 0, cnt_eq - need), 0)) > 0
        last = lax.cond(has_ties, _tie_break(count, thr, need, nbits, 2, (8, 1)),
                        lambda _: jnp.full((8, 1), (1 << nbits) - 1, I32), 0)
        last = jnp.where(short, -1, last)
        s = sc_ref[...]
        slot = slot0 + lax.broadcasted_iota(I32, s.shape, 2)
        keep_eq = jnp.where(slot <= last, 0.0, NEG_BIAS)
        bias_ref[0] = jnp.where(s > thr, 0.0, jnp.where(s == thr, keep_eq, NEG_BIAS))


def _page_specs(pps, rows, page_table_cols):
    def spec(p):
        return pl.BlockSpec((1, 1, rows, PAGE_SIZE),
                            lambda b, pc, pt: (0, pt[b * page_table_cols + pc * pps + p], 0, 0))
    return [spec(p) for p in range(pps)]


def _sample_idx(pt_flat, qi_blk, wb, ki_new_t, cache_idx_t, n_pages, topk, dseq):
    nb = qi_blk.shape[0]
    pps = IDX_PAGES_PER_STEP
    n_steps = n_pages // pps
    n_chunks = n_pages + 1
    grid_spec = pltpu.PrefetchScalarGridSpec(
        num_scalar_prefetch=1,
        grid=(nb, n_steps),
        in_specs=[pl.BlockSpec((1, 64, IDX_DIM), lambda b, pc, pt: (b, 0, 0)),
                  pl.BlockSpec((1, 64, LANES), lambda b, pc, pt: (b, 0, 0)),
                  pl.BlockSpec((1, IDX_DIM, LANES), lambda b, pc, pt: (b, 0, 0)),
                  *_page_specs(pps, IDX_DIM, n_pages)],
        out_specs=pl.BlockSpec((1, n_chunks, 8, LANES), lambda b, pc, pt: (b, 0, 0, 0)),
        scratch_shapes=[pltpu.VMEM((n_chunks, 8, LANES), F32)],
    )
    return pl.pallas_call(
        functools.partial(_sample_idx_kernel, n_steps=n_steps, topk=topk, dseq=dseq, pps=pps),
        grid_spec=grid_spec,
        out_shape=jax.ShapeDtypeStruct((nb, n_chunks, 8, LANES), F32),
        compiler_params=pltpu.CompilerParams(dimension_semantics=("arbitrary", "arbitrary")),
        name="sample_idx",
    )(pt_flat, qi_blk, wb, ki_new_t, *([cache_idx_t] * pps))


def _sample_att_kernel(pt_ref, qb_ref, bias_ref, kn_ref, vn_ref, *rest, n_steps, pps):
    kpages = rest[:pps]
    vpages = rest[pps:2 * pps]
    o_ref, m_ref, l_ref, acc_ref, kbuf, vbuf = rest[2 * pps:]
    pc = pl.program_id(1)
    qb = qb_ref[0]
    n_chunks = n_steps * pps + 1

    def rows64(b8):
        return jnp.concatenate([b8] * ATT_HEADS, axis=0)

    @pl.when(pc == 0)
    def _():
        s = _dot(qb, kn_ref[0]) + rows64(bias_ref[0, n_chunks - 1])
        m = jnp.max(s, axis=1, keepdims=True)
        p = jnp.exp2(s - m)
        m_ref[...] = m
        l_ref[...] = jnp.sum(p, axis=1, keepdims=True)
        acc_ref[...] = _dot_nt(p.astype(BF16), vn_ref[0])

    for p in range(pps):
        kbuf[:, p * PAGE_SIZE:(p + 1) * PAGE_SIZE] = kpages[p][0, 0].astype(BF16)
        vbuf[:, p * PAGE_SIZE:(p + 1) * PAGE_SIZE] = vpages[p][0, 0].astype(BF16)
    bias = jnp.concatenate([bias_ref[0, pc * pps + p] for p in range(pps)], axis=1)
    s = _dot(qb, kbuf[...]) + rows64(bias)
    m_old = m_ref[...]
    m_new = jnp.maximum(m_old, jnp.max(s, axis=1, keepdims=True))
    a = jnp.exp2(m_old - m_new)
    p = jnp.exp2(s - m_new)
    m_ref[...] = m_new
    l_ref[...] = a * l_ref[...] + jnp.sum(p, axis=1, keepdims=True)
    acc_ref[...] = a * acc_ref[...] + _dot_nt(p.astype(BF16), vbuf[...])

    @pl.when(pc == n_steps - 1)
    def _():
        o_ref[0] = acc_ref[...] / l_ref[...]


def _sample_att(pt_flat, qb, bias, k_new_t, v_new_t, cache_k_t, cache_v_t, n_pages):
    nb = qb.shape[0]
    pps = ATT_PAGES_PER_STEP
    n_steps = n_pages // pps
    n_chunks = n_pages + 1
    rows = ATT_HEADS * 8
    grid_spec = pltpu.PrefetchScalarGridSpec(
        num_scalar_prefetch=1,
        grid=(nb, n_steps),
        in_specs=[pl.BlockSpec((1, rows, ATT_WIDTH), lambda b, pc, pt: (b, 0, 0)),
                  pl.BlockSpec((1, n_chunks, 8, LANES), lambda b, pc, pt: (b, 0, 0, 0)),
                  pl.BlockSpec((1, ATT_WIDTH, LANES), lambda b, pc, pt: (b, 0, 0)),
                  pl.BlockSpec((1, ATT_WIDTH, LANES), lambda b, pc, pt: (b, 0, 0)),
                  *_page_specs(pps, ATT_WIDTH, n_pages),
                  *_page_specs(pps, ATT_WIDTH, n_pages)],
        out_specs=pl.BlockSpec((1, rows, ATT_WIDTH), lambda b, pc, pt: (b, 0, 0)),
        scratch_shapes=[pltpu.VMEM((rows, 1), F32), pltpu.VMEM((rows, 1), F32),
                        pltpu.VMEM((rows, ATT_WIDTH), F32),
                        pltpu.VMEM((ATT_WIDTH, pps * PAGE_SIZE), BF16),
                        pltpu.VMEM((ATT_WIDTH, pps * PAGE_SIZE), BF16)],
    )
    return pl.pallas_call(
        functools.partial(_sample_att_kernel, n_steps=n_steps, pps=pps),
        grid_spec=grid_spec,
        out_shape=jax.ShapeDtypeStruct((nb, rows, ATT_WIDTH), F32),
        compiler_params=pltpu.CompilerParams(dimension_semantics=("arbitrary", "arbitrary")),
        name="sample_att",
    )(pt_flat, qb, bias, k_new_t, v_new_t, *([cache_k_t] * pps), *([cache_v_t] * pps))


def _rope_tables(pos):
    half = RET_DK // 2
    inv = ROPE_BASE ** (-jnp.arange(half, dtype=F32) / half)
    ang = pos.astype(F32)[:, None] * inv[None, :]
    cos, sin = jnp.cos(ang), jnp.sin(ang)
    return jnp.concatenate([cos, cos], axis=1), jnp.concatenate([-sin, sin], axis=1)


def _pad_rows(a, rows):
    return jnp.pad(a, ((0, rows - a.shape[0]),) + ((0, 0),) * (a.ndim - 1))


def _pad_last(a, n):
    return jnp.pad(a, ((0, 0),) * (a.ndim - 1) + ((0, n - a.shape[-1]),))


def kernel(x_prompt, x_sample, cache_k, cache_v, cache_idx_k, state_ret, page_table, meta_tokens,
           ffn1_w_gate, ffn1_w_up, ffn1_w_down, ln1_g, ln1_b, w_in, w_out, ln2_g, ln2_b,
           ffn2_w_gate, ffn2_w_up, ffn2_w_down, ln3_g, ln3_b):
    batch, seq, d = x_prompt.shape
    nb, dseq, _ = x_sample.shape
    n_pages = page_table.shape[1]
    n_pool = cache_k.shape[1]
    past = n_pages * PAGE_SIZE
    nq = seq // Q_TILE
    assert d == D_MODEL and seq % Q_TILE == 0 and Q_TILE == ROW_TILE == K_TILE
    assert dseq <= 8 and n_pages % IDX_PAGES_PER_STEP == 0 and n_pages % ATT_PAGES_PER_STEP == 0
    ns = nb * dseq
    n_small = ns + N_META
    assert n_small <= ROW_TILE
    meta = slice(ns, n_small)

    bf = lambda a: a.astype(BF16)
    l = 0
    f1 = (bf(ffn1_w_gate[l]), bf(ffn1_w_up[l]), bf(ffn1_w_down[l]))
    f2 = (bf(ffn2_w_gate[l]), bf(ffn2_w_up[l]), bf(ffn2_w_down[l]))
    vec = lambda a: a[l][None, :].astype(F32)
    w = w_in[l]
    offs = np.cumsum([0, 512, 512, 512, 512, 64, 8, 512, 512, 512, 512])
    wq, wk, wv, wqi, wki, wwi, wrq, wrk, wrv, wg = [w[:, offs[i]:offs[i + 1]] for i in range(10)]
    kiw = jnp.concatenate([wki, wwi, jnp.zeros((d, LANES - IDX_DIM - IDX_HEADS), w.dtype)], axis=1)
    wrow = bf(jnp.concatenate([wk, kiw, wrq, wrk, wrv, wg], axis=1))
    wt = bf(jnp.concatenate([wq, wk, wv, wqi, kiw], axis=1).T)
    woa, wor = bf(w_out[l][:ATT_WIDTH]), bf(w_out[l][ATT_WIDTH:])

    xs = _pad_rows(jnp.concatenate([x_sample.reshape(ns, d), meta_tokens.astype(x_prompt.dtype)], axis=0), ROW_TILE)
    xp = x_prompt.reshape(batch * seq, d)
    pos_small = jnp.concatenate([jnp.tile(past + jnp.arange(dseq, dtype=I32), nb),
                                 jnp.arange(N_META, dtype=I32),
                                 jnp.zeros((ROW_TILE - n_small,), I32)])
    cos_s, sin_s = _rope_tables(pos_small)
    cos_p, sin_p = _rope_tables(N_META + jnp.arange(seq, dtype=I32))

    x1p = _ffn_ln(xp, *f1, vec(ln1_g), vec(ln1_b))
    x1s = _ffn_ln(xs, *f1, vec(ln1_g), vec(ln1_b))
    (kb_p, kib_p, rq_p, rk_p, rv_p, g_p, qt_p, ktf_p, vtf_p, vt_p, qit_p, kiwt_p) = _proj(
        x1p, wrow, wt, cos_p, sin_p, batch, nq)
    (kb_s, kib_s, rq_s, rk_s, rv_s, g_s, qt_s, ktf_s, vtf_s, vt_s, qit_s, kiwt_s) = _proj(
        x1s, wrow, wt, cos_s, sin_s, 1, 1)
    ktf_s, vtf_s, kiwt_s = ktf_s[0], vtf_s[0], kiwt_s[0]

    topk_p = min(TOPK_MAX, seq // 4)
    def key_chunks(meta_chunk, real):
        m = jnp.broadcast_to(meta_chunk[None, None], (batch, 1) + meta_chunk.shape)
        z = jnp.zeros((batch, SPARE_CHUNKS) + meta_chunk.shape, meta_chunk.dtype)
        return jnp.concatenate([m, real.reshape((batch, nq) + meta_chunk.shape), z], axis=1)

    k4 = key_chunks(_pad_rows(kb_s[meta], K_TILE), kb_p)
    ki4 = key_chunks(_pad_rows(kib_s[meta], K_TILE), kib_p)
    vt4 = key_chunks(_pad_last(vt_s[0][:, :, meta], K_TILE), vt_p)
    att_p = _prompt_att(qt_p, qit_p, kiwt_p[:, IDX_DIM:IDX_DIM + IDX_HEADS, :], k4, vt4, ki4, batch, nq, topk_p)
    att_p = att_p.reshape(batch * seq, ATT_WIDTH)

    pad_tok = lambda a, n: jnp.pad(a, ((0, 0), (0, RET_CHUNK - n), (0, 0)))
    m3 = lambda a: pad_tok(a[meta][None], N_META)
    zero_state = jnp.zeros((1, RET_HEADS, RET_DK, RET_DV), F32)
    _, s_meta = _retention(m3(rq_s), m3(rk_s), m3(rv_s), m3(g_s), zero_state, _ret_tables(N_META), True)
    b3 = lambda a: a.reshape(batch, seq, RET_WIDTH)
    yret_p, ret_prompt = _retention(b3(rq_p), b3(rk_p), b3(rv_p), b3(g_p), s_meta, _ret_tables(RET_CHUNK), True)
    s3 = lambda a: pad_tok(a[:ns].reshape(nb, dseq, RET_WIDTH), dseq)
    yret_s, ret_sample = _retention(s3(rq_s), s3(rk_s), s3(rv_s), s3(g_s), state_ret[l], _ret_tables(dseq), False)

    topk_s = min(TOPK_MAX, (past + dseq) // 4)
    pt_flat = page_table.reshape(-1).astype(I32)
    cache_k_t = jnp.transpose(cache_k[l], (0, 2, 3, 1)).reshape(1, n_pool, ATT_WIDTH, PAGE_SIZE)
    cache_v_t = jnp.transpose(cache_v[l], (0, 2, 3, 1)).reshape(1, n_pool, ATT_WIDTH, PAGE_SIZE)
    cache_idx_t = jnp.transpose(cache_idx_k[l], (0, 2, 1))[None]
    q_rows = jnp.concatenate([qt_s[0, h, (h % 2) * ATT_DIM:(h % 2 + 1) * ATT_DIM, :] for h in range(ATT_HEADS)],
                             axis=0).T[:ns]
    qi_rows = qit_s[0].reshape(IDX_DIM, IDX_HEADS, ROW_TILE)[:, :, :ns]
    qi_blk = jnp.pad(qi_rows.transpose(2, 1, 0).reshape(nb, dseq, IDX_HEADS, IDX_DIM).transpose(0, 2, 1, 3),
                     ((0, 0), (0, 0), (0, 8 - dseq), (0, 0))).reshape(nb, IDX_HEADS * 8, IDX_DIM)
    wi_s = kiwt_s[IDX_DIM:IDX_DIM + IDX_HEADS, :ns].reshape(IDX_HEADS, nb, dseq)
    wb = jnp.pad(wi_s.transpose(1, 0, 2), ((0, 0), (0, 0), (0, 8 - dseq))).reshape(nb, IDX_HEADS * 8, 1)
    wb = jnp.broadcast_to(wb, (nb, IDX_HEADS * 8, LANES)).astype(F32)
    new_t = lambda a_t: _pad_last(a_t[:, :ns].reshape(a_t.shape[0], nb, dseq).transpose(1, 0, 2), LANES)
    bias_s = _sample_idx(pt_flat, qi_blk, wb, bf(new_t(kiwt_s[:IDX_DIM])), cache_idx_t, n_pages, topk_s, dseq)
    head_mask = (jnp.arange(ATT_WIDTH)[None, :] // ATT_DIM == jnp.arange(ATT_HEADS)[:, None])
    q3 = jnp.pad(q_rows.reshape(nb, dseq, ATT_WIDTH), ((0, 0), (0, 8 - dseq), (0, 0)))
    qb = jnp.where(head_mask[None, :, None, :], q3[:, None, :, :], 0).astype(BF16).reshape(nb, ATT_HEADS * 8, ATT_WIDTH)
    att_s64 = _sample_att(pt_flat, qb, bias_s, bf(new_t(ktf_s)), bf(new_t(vtf_s)), cache_k_t, cache_v_t, n_pages)
    a5 = att_s64.reshape(nb, ATT_HEADS, 8, ATT_HEADS, ATT_DIM)
    att_s = jnp.stack([a5[:, h, :dseq, h, :] for h in range(ATT_HEADS)], axis=2).reshape(ns, ATT_WIDTH)

    lnw = (vec(ln2_g), vec(ln2_b), *f2, vec(ln3_g), vec(ln3_b))
    y_p = _out_ffn(x1p, att_p, yret_p.reshape(batch * seq, RET_WIDTH), woa, wor, *lnw)
    att_small = _pad_rows(bf(att_s), ROW_TILE)
    yret_small = _pad_rows(yret_s[:, :dseq].reshape(ns, RET_WIDTH), ROW_TILE)
    y_s = _out_ffn(x1s, att_small, yret_small, woa, wor, *lnw)

    def with_meta(real_t, small_t, feat_shape):
        m = jnp.broadcast_to(small_t[None, :, meta], (batch, small_t.shape[0], N_META))
        full = jnp.concatenate([m, real_t], axis=2)
        full = full.reshape((batch,) + feat_shape + (seq + N_META,))
        return jnp.moveaxis(full, -1, 1)[None]

    k_prompt = with_meta(ktf_p, ktf_s, (ATT_HEADS, ATT_DIM))
    v_prompt = with_meta(vtf_p, vtf_s, (ATT_HEADS, ATT_DIM))
    idxk_prompt = with_meta(kiwt_p[:, :IDX_DIM], kiwt_s[:IDX_DIM], (IDX_DIM,))
    y_prompt = y_p.reshape(batch, seq, d)
    y_sample = y_s[:ns].reshape(nb, dseq, d)
    k_sample = ktf_s[:, :ns].T.reshape(1, nb, dseq, ATT_HEADS, ATT_DIM)
    v_sample = vtf_s[:, :ns].T.reshape(1, nb, dseq, ATT_HEADS, ATT_DIM)
    idxk_sample = kiwt_s[:IDX_DIM, :ns].T.reshape(1, nb, dseq, IDX_DIM)
    return (y_prompt, y_sample, k_prompt, v_prompt, idxk_prompt, ret_prompt[None],
            k_sample, v_sample, idxk_sample, ret_sample[None])
```

```python
import functools

import jax
import jax.numpy as jnp
import numpy as np
from jax import lax
from jax.experimental import pallas as pl
from jax.experimental.pallas import tpu as pltpu

F32 = jnp.float32
BF16 = jnp.bfloat16
I32 = jnp.int32

D_MODEL = 1024
N_META = 16
ATT_DIM = 64
ATT_HEADS = 8
ATT_WIDTH = ATT_HEADS * ATT_DIM
IDX_HEADS = 8
IDX_DIM = 64
TOPK_MAX = 256
RET_HEADS = 4
RET_DK = 128
RET_DV = 128
RET_WIDTH = RET_HEADS * RET_DV
PAGE_SIZE = 128
ROPE_BASE = 10000.0
LN_EPS = 1e-5
GN_EPS = 1e-5
DEPTH = 1
ALPHA = (2.0 * DEPTH) ** 0.25

LANES = 128
ROW_TILE = 256
FFN_TILE = 512
Q_TILE = 256
K_TILE = 256
RET_CHUNK = 128
RET_SEQS_PER_STEP = 2
IDX_PAGES_PER_STEP = 32
ATT_PAGES_PER_STEP = 16
LOG2E = 1.4426950408889634
NEG_BIAS = -1e30
INT_MIN = -2147483648
VMEM_LIMIT = 56 * 1024 * 1024


def _dot(a, b):
    return jnp.dot(a, b, preferred_element_type=F32)


def _dot_nt(a, b):
    return lax.dot_general(a, b, (((1,), (1,)), ((), ())), preferred_element_type=F32)


def _layernorm(y, g, b):
    mu = jnp.mean(y, axis=-1, keepdims=True)
    d = y - mu
    var = jnp.mean(d * d, axis=-1, keepdims=True)
    return d * lax.rsqrt(var + LN_EPS) * g + b


def _swiglu(xb, wg_ref, wu_ref, wd_ref):
    hg = _dot(xb, wg_ref[...])
    hu = _dot(xb, wu_ref[...])
    act = (hg * jax.nn.sigmoid(hg)) * hu
    return _dot(act.astype(BF16), wd_ref[...])


def _ffn_ln_kernel(x_ref, wg_ref, wu_ref, wd_ref, g_ref, b_ref, o_ref):
    x = x_ref[...]
    y = ALPHA * x + 0.5 * _swiglu(x.astype(BF16), wg_ref, wu_ref, wd_ref)
    o_ref[...] = _layernorm(y, g_ref[...], b_ref[...])


def _const_spec(shape):
    return pl.BlockSpec(shape, lambda *_: (0,) * len(shape))


def _weight_spec(shape):
    return pl.BlockSpec(shape, lambda *_: (0,) * len(shape), pipeline_mode=pl.Buffered(1))


def _ffn_ln(x, wg, wu, wd, g, b):
    rows, d = x.shape
    dff = wg.shape[1]
    tm = min(FFN_TILE, rows)
    return pl.pallas_call(
        _ffn_ln_kernel,
        grid=(rows // tm,),
        in_specs=[pl.BlockSpec((tm, d), lambda i: (i, 0)),
                  _weight_spec((d, dff)), _weight_spec((d, dff)), _weight_spec((dff, d)),
                  _const_spec((1, d)), _const_spec((1, d))],
        out_specs=pl.BlockSpec((tm, d), lambda i: (i, 0)),
        out_shape=jax.ShapeDtypeStruct((rows, d), F32),
        compiler_params=pltpu.CompilerParams(
            dimension_semantics=("arbitrary",), vmem_limit_bytes=VMEM_LIMIT),
        name="ffn_ln",
    )(x, wg, wu, wd, g, b)


def _out_ffn_kernel(x1_ref, att_ref, yret_ref, woa_ref, wor_ref, g2_ref, b2_ref,
                    wg_ref, wu_ref, wd_ref, g3_ref, b3_ref, o_ref):
    m = _dot(att_ref[...], woa_ref[...]) + _dot(yret_ref[...], wor_ref[...])
    x2 = _layernorm(ALPHA * x1_ref[...] + m, g2_ref[...], b2_ref[...])
    y = ALPHA * x2 + 0.5 * _swiglu(x2.astype(BF16), wg_ref, wu_ref, wd_ref)
    o_ref[...] = _layernorm(y, g3_ref[...], b3_ref[...])


def _out_ffn(x1, att, yret, woa, wor, g2, b2, wg, wu, wd, g3, b3):
    rows, d = x1.shape
    dff = wg.shape[1]
    tm = min(FFN_TILE, rows)
    row = lambda w: pl.BlockSpec((tm, w), lambda i: (i, 0))
    return pl.pallas_call(
        _out_ffn_kernel,
        grid=(rows // tm,),
        in_specs=[row(d), row(ATT_WIDTH), row(RET_WIDTH),
                  _weight_spec((ATT_WIDTH, d)), _weight_spec((RET_WIDTH, d)),
                  _const_spec((1, d)), _const_spec((1, d)),
                  _weight_spec((d, dff)), _weight_spec((d, dff)), _weight_spec((dff, d)),
                  _const_spec((1, d)), _const_spec((1, d))],
        out_specs=row(d),
        out_shape=jax.ShapeDtypeStruct((rows, d), F32),
        compiler_params=pltpu.CompilerParams(
            dimension_semantics=("arbitrary",), vmem_limit_bytes=VMEM_LIMIT),
        name="out_ffn",
    )(x1, att, yret, woa, wor, g2, b2, wg, wu, wd, g3, b3)


W_ROW_COLS = ATT_WIDTH + LANES + 4 * RET_WIDTH
W_T_ROWS = 4 * ATT_WIDTH + LANES
V_ROWS = ATT_DIM + 16


def _rope(x, cos2, sin2):
    return x * cos2 + pltpu.roll(x, RET_DK // 2, 1) * sin2


def _proj_kernel(x1_ref, wrow_ref, wt_ref, cos_ref, sin_ref,
                 kb_ref, kib_ref, rq_ref, rk_ref, rv_ref, g_ref,
                 qt_ref, ktf_ref, vtf_ref, vt_ref, qit_ref, kiwt_ref):
    xb = x1_ref[...].astype(BF16)
    z = _dot(xb, wrow_ref[...])
    kb_ref[...] = z[:, 0:ATT_WIDTH].astype(BF16)
    kib_ref[...] = z[:, ATT_WIDTH:ATT_WIDTH + IDX_DIM].astype(BF16)
    o = ATT_WIDTH + LANES
    cos2 = cos_ref[...]
    sin2 = sin_ref[...]
    for h in range(RET_HEADS):
        sl = slice(h * RET_DK, (h + 1) * RET_DK)
        rq = z[:, o + h * RET_DK:o + (h + 1) * RET_DK]
        rk = z[:, o + RET_WIDTH + h * RET_DK:o + RET_WIDTH + (h + 1) * RET_DK]
        rq_ref[:, sl] = _rope(rq, cos2, sin2).astype(BF16)
        rk_ref[:, sl] = (_rope(rk, cos2, sin2) * (RET_DK ** -0.5)).astype(BF16)
    rv_ref[...] = z[:, o + 2 * RET_WIDTH:o + 3 * RET_WIDTH].astype(BF16)
    g_ref[...] = z[:, o + 3 * RET_WIDTH:o + 4 * RET_WIDTH]

    zt = _dot_nt(wt_ref[...], xb)
    tm = xb.shape[0]
    zero = jnp.zeros((ATT_DIM, tm), BF16)
    for h in range(ATT_HEADS):
        qh = (zt[h * ATT_DIM:(h + 1) * ATT_DIM, :] * (ATT_DIM ** -0.5 * LOG2E)).astype(BF16)
        qt_ref[0, h] = jnp.concatenate([qh, zero] if h % 2 == 0 else [zero, qh], axis=0)
    ktf_ref[0] = zt[ATT_WIDTH:2 * ATT_WIDTH, :]
    vt = zt[2 * ATT_WIDTH:3 * ATT_WIDTH, :]
    vtf_ref[0] = vt
    ones_blk = jnp.where(lax.broadcasted_iota(I32, (V_ROWS - ATT_DIM, tm), 0) == 0, 1.0, 0.0).astype(BF16)
    for h in range(ATT_HEADS):
        vt_ref[0, h] = jnp.concatenate([vt[h * ATT_DIM:(h + 1) * ATT_DIM, :].astype(BF16), ones_blk], axis=0)
    for h in range(IDX_HEADS):
        qih = zt[3 * ATT_WIDTH + h * IDX_DIM:3 * ATT_WIDTH + (h + 1) * IDX_DIM, :]
        qit_ref[0, :, h * tm:(h + 1) * tm] = (qih * (IDX_DIM ** -0.5)).astype(BF16)
    kiwt_ref[0] = zt[4 * ATT_WIDTH:4 * ATT_WIDTH + LANES, :]


def _proj(x1, wrow, wt, cos2, sin2, batch, tiles_per_seq):
    rows, d = x1.shape
    tm = ROW_TILE
    nt = rows // tm
    seq = tiles_per_seq * tm
    row = lambda w: pl.BlockSpec((tm, w), lambda i: (i, 0))
    tab = pl.BlockSpec((tm, LANES), lambda i: (i % tiles_per_seq, 0))
    tcol = lambda r: pl.BlockSpec((1, r, tm), lambda i: (i // tiles_per_seq, 0, i % tiles_per_seq))
    out_shape = (
        jax.ShapeDtypeStruct((rows, ATT_WIDTH), BF16),
        jax.ShapeDtypeStruct((rows, IDX_DIM), BF16),
        jax.ShapeDtypeStruct((rows, RET_WIDTH), BF16),
        jax.ShapeDtypeStruct((rows, RET_WIDTH), BF16),
        jax.ShapeDtypeStruct((rows, RET_WIDTH), BF16),
        jax.ShapeDtypeStruct((rows, RET_WIDTH), F32),
        jax.ShapeDtypeStruct((nt, ATT_HEADS, 2 * ATT_DIM, tm), BF16),
        jax.ShapeDtypeStruct((batch, ATT_WIDTH, seq), F32),
        jax.ShapeDtypeStruct((batch, ATT_WIDTH, seq), F32),
        jax.ShapeDtypeStruct((nt, ATT_HEADS, V_ROWS, tm), BF16),
        jax.ShapeDtypeStruct((nt, IDX_DIM, IDX_HEADS * tm), BF16),
        jax.ShapeDtypeStruct((batch, LANES, seq), F32),
    )
    out_specs = (
        row(ATT_WIDTH), row(IDX_DIM), row(RET_WIDTH), row(RET_WIDTH), row(RET_WIDTH), row(RET_WIDTH),
        pl.BlockSpec((1, ATT_HEADS, 2 * ATT_DIM, tm), lambda i: (i, 0, 0, 0)),
        tcol(ATT_WIDTH), tcol(ATT_WIDTH),
        pl.BlockSpec((1, ATT_HEADS, V_ROWS, tm), lambda i: (i, 0, 0, 0)),
        pl.BlockSpec((1, IDX_DIM, IDX_HEADS * tm), lambda i: (i, 0, 0)),
        tcol(LANES),
    )
    return pl.pallas_call(
        _proj_kernel,
        grid=(nt,),
        in_specs=[row(d), _const_spec((d, W_ROW_COLS)), _const_spec((W_T_ROWS, d)), tab, tab],
        out_specs=out_specs,
        out_shape=out_shape,
        compiler_params=pltpu.CompilerParams(
            dimension_semantics=("arbitrary",), vmem_limit_bytes=VMEM_LIMIT),
        name="proj",
    )(x1, wrow, wt, cos2, sin2)


def _pattern_to_float(u):
    key = u ^ INT_MIN
    bits = jnp.where(key >= 0, key, key ^ 0x7FFFFFFF)
    return lax.bitcast_convert_type(bits, F32)


def _kth_largest(count, topk, shape, tile_shape):
    def step(i, t_u):
        cand = t_u | (1 << (31 - i))
        cand_f = jnp.broadcast_to(_pattern_to_float(cand), tile_shape)
        cnt = count(lambda s, base: jnp.where(s >= cand_f, 1, 0))
        return jnp.where(cnt >= topk, cand, t_u)

    return lax.fori_loop(0, 32, step, jnp.zeros(shape, I32))


def _tie_break(count, thr, need, nbits, slot_axis, shape):
    def search(_):
        def step(i, p):
            cand = p | (1 << (nbits - 1 - i))
            cnt = count(lambda s, base: jnp.where(
                s == thr, jnp.where((base + lax.broadcasted_iota(I32, s.shape, slot_axis)) < cand, 1, 0), 0))
            return jnp.where(cnt < need, cand, p)
        return lax.fori_loop(0, nbits, step, jnp.zeros(shape, I32))

    return search


SPARE_CHUNKS = 4
COUNT_LANES = 4


def _prompt_att_kernel(qt_ref, qit_ref, wit_ref, k_ref, vt_ref, ki_ref, o_ref,
                       sc_ref, ot_ref, m_ref, s0_ref, s1_ref, c0_ref, c1_ref, *, topk, nq):
    j = pl.program_id(1)
    tq, tk = Q_TILE, K_TILE
    n_chunks = j + 2
    n_pairs = lax.shift_right_logical(n_chunks + 1, 1)
    col = lax.broadcasted_iota(I32, (1, tq), 1)
    rows_k = lax.broadcasted_iota(I32, (tk, tq), 0)

    def rows_of(c, n=1):
        return pl.ds(pl.multiple_of(c * tk, tk), n * tk)

    def p1(i, carry):
        for c in (2 * i, 2 * i + 1):
            acc = None
            for h in range(IDX_HEADS):
                s = _dot(ki_ref[0, c], qit_ref[0, :, h * tq:(h + 1) * tq])
                t = jnp.maximum(s, 0.0) * wit_ref[0, h:h + 1, :]
                acc = t if acc is None else acc + t
            last_row = jnp.where(c == 0, N_META - 1, j * tq + col - (c - 1) * tk)
            sc_ref[rows_of(c), :] = jnp.where(rows_k <= last_row, acc, -jnp.inf)
        return carry

    lax.fori_loop(0, n_pairs, p1, 0)
    sc_ref[rows_of(n_chunks, SPARE_CHUNKS), :] = jnp.full((SPARE_CHUNKS * tk, tq), -jnp.inf, F32)

    def count(ind_fn):
        def body(c4, part):
            base = pl.multiple_of(SPARE_CHUNKS * c4 * tk, SPARE_CHUNKS * tk)
            x = ind_fn(sc_ref[pl.ds(base, SPARE_CHUNKS * tk), :], base)
            return part + jnp.sum(x.reshape(-1, COUNT_LANES, 8, tq), axis=0)

        n_groups = lax.shift_right_logical(n_chunks + SPARE_CHUNKS - 1, 2)
        part = lax.fori_loop(0, n_groups, body, jnp.zeros((COUNT_LANES, 8, tq), I32))
        return jnp.sum(part.reshape(COUNT_LANES * 8, tq), axis=0, keepdims=True)

    t_u = _kth_largest(count, topk, (1, tq), (1, tq))
    short = t_u == 0
    thr = jnp.where(short, -jnp.inf, _pattern_to_float(t_u))
    cnt_gt = count(lambda s, base: jnp.where(s > thr, 1, 0))
    cnt_eq = count(lambda s, base: jnp.where(s == thr, 1, 0))
    need = topk - cnt_gt
    nbits = int(tk * (nq + 1 + SPARE_CHUNKS) - 1).bit_length()
    has_ties = jnp.max(jnp.where(short, 0, cnt_eq - need)) > 0
    last = lax.cond(has_ties, _tie_break(count, thr, need, nbits, 0, (1, tq)),
                    lambda _: jnp.full((1, tq), (1 << nbits) - 1, I32), 0)
    last = jnp.where(short, -1, last)

    def p3(c, carry):
        s = sc_ref[rows_of(c), :]
        keep_eq = jnp.where(c * tk + rows_k <= last, 0.0, NEG_BIAS)
        sc_ref[rows_of(c), :] = jnp.where(s > thr, 0.0, jnp.where(s == thr, keep_eq, NEG_BIAS))
        return carry

    lax.fori_loop(0, n_chunks + SPARE_CHUNKS, p3, 0)

    def k_lanes(h):
        return slice((h // 2) * LANES, (h // 2 + 1) * LANES)

    rep = lambda x: jnp.broadcast_to(x, (8, tq))

    def logits_stage(c, h, s_ref, cmax_ref):
        s = _dot(k_ref[0, c, :, k_lanes(h)], qt_ref[0, h]) + sc_ref[rows_of(c), :]
        s_ref[h] = s
        cmax_ref[h] = rep(jnp.max(s, axis=0, keepdims=True))

    def softmax_stage(c, h, s_ref, cmax_ref):
        m_old = m_ref[h]
        m_new = jnp.maximum(m_old, cmax_ref[h])
        a = jnp.exp2(m_old - m_new)
        p = jnp.exp2(s_ref[h] - m_new[0:1, :])
        m_ref[h] = m_new
        ot_ref[h] = a[0:1, :] * ot_ref[h] + _dot(vt_ref[0, c, h], p.astype(BF16))

    m_ref[...] = jnp.full(m_ref.shape, NEG_BIAS, F32)
    ot_ref[...] = jnp.zeros(ot_ref.shape, F32)
    for h in range(ATT_HEADS):
        logits_stage(0, h, s0_ref, c0_ref)

    def p4(i, carry):
        for h in range(ATT_HEADS):
            logits_stage(2 * i + 1, h, s1_ref, c1_ref)
            softmax_stage(2 * i, h, s0_ref, c0_ref)
        for h in range(ATT_HEADS):
            logits_stage(2 * i + 2, h, s0_ref, c0_ref)
            softmax_stage(2 * i + 1, h, s1_ref, c1_ref)
        return carry

    lax.fori_loop(0, n_pairs, p4, 0)
    out_t = jnp.concatenate([ot_ref[h, 0:ATT_DIM, :] / ot_ref[h, ATT_DIM:ATT_DIM + 1, :]
                             for h in range(ATT_HEADS)], axis=0)
    o_ref[0] = out_t.T.astype(BF16)


def _prompt_att(qt, qit, wit, k4, vt4, ki4, batch, nq, topk):
    tq = Q_TILE
    nc = nq + 1 + SPARE_CHUNKS
    per_batch = lambda shape: pl.BlockSpec((1,) + shape, lambda b, j: (b,) + (0,) * len(shape))
    hs = (ATT_HEADS, K_TILE, tq)
    return pl.pallas_call(
        functools.partial(_prompt_att_kernel, topk=topk, nq=nq),
        grid=(batch, nq),
        in_specs=[
            pl.BlockSpec((1, ATT_HEADS, 2 * ATT_DIM, tq), lambda b, j: (b * nq + j, 0, 0, 0)),
            pl.BlockSpec((1, IDX_DIM, IDX_HEADS * tq), lambda b, j: (b * nq + j, 0, 0)),
            pl.BlockSpec((1, IDX_HEADS, tq), lambda b, j: (b, 0, j)),
            per_batch((nc, K_TILE, ATT_WIDTH)),
            per_batch((nc, ATT_HEADS, V_ROWS, K_TILE)),
            per_batch((nc, K_TILE, IDX_DIM)),
        ],
        out_specs=pl.BlockSpec((1, tq, ATT_WIDTH), lambda b, j: (b * nq + j, 0, 0)),
        out_shape=jax.ShapeDtypeStruct((batch * nq, tq, ATT_WIDTH), BF16),
        scratch_shapes=[pltpu.VMEM((nc * K_TILE, tq), F32), pltpu.VMEM((ATT_HEADS, V_ROWS, tq), F32),
                        pltpu.VMEM((ATT_HEADS, 8, tq), F32),
                        pltpu.VMEM(hs, F32), pltpu.VMEM(hs, F32),
                        pltpu.VMEM((ATT_HEADS, 8, tq), F32), pltpu.VMEM((ATT_HEADS, 8, tq), F32)],
        compiler_params=pltpu.CompilerParams(
            dimension_semantics=("arbitrary", "arbitrary"), vmem_limit_bytes=VMEM_LIMIT),
        name="prompt_att",
    )(qt, qit, wit, k4, vt4, ki4)


def _ret_kernel(rq_ref, rk_ref, rv_ref, g_ref, s0_ref, decay_ref, xi_ref, zeta_ref, gl_ref,
                y_ref, sfin_ref, state_ref, *, shared_s0):
    t = pl.program_id(1)
    nseq = state_ref.shape[0]

    @pl.when(t == 0)
    def _():
        for r in range(nseq):
            state_ref[r] = s0_ref[0 if shared_s0 else r]

    for r in range(nseq):
        for h in range(RET_HEADS):
            sl = slice(h * RET_DK, (h + 1) * RET_DK)
            q = rq_ref[r, :, sl]
            k = rk_ref[r, :, sl]
            v = rv_ref[r, :, sl]
            sp = state_ref[r, h]
            inner = _dot_nt(q, k) * decay_ref[h]
            ret = _dot(inner.astype(BF16), v) + _dot(q, sp.astype(BF16)) * xi_ref[h]
            kz = (k.astype(F32) * zeta_ref[h]).T.astype(BF16)
            state_ref[r, h] = gl_ref[h] * sp + _dot(kz, v)
            mu = jnp.mean(ret, axis=-1, keepdims=True)
            d = ret - mu
            var = jnp.mean(d * d, axis=-1, keepdims=True)
            retn = d * lax.rsqrt(var + GN_EPS)
            g = g_ref[r, :, sl]
            y_ref[r, :, sl] = ((g * jax.nn.sigmoid(g)) * retn).astype(BF16)

    @pl.when(t == pl.num_programs(1) - 1)
    def _():
        sfin_ref[...] = state_ref[...]


def _retention(rq, rk, rv, g, s0, tables, shared_s0):
    batch, tokens, _ = rq.shape
    nchunk = tokens // RET_CHUNK
    nseq = RET_SEQS_PER_STEP if batch % RET_SEQS_PER_STEP == 0 else 1
    decay, xi, zeta, gl = tables
    tok = pl.BlockSpec((nseq, RET_CHUNK, RET_WIDTH), lambda b, t: (b, t, 0))
    hh = (RET_HEADS, RET_DK, RET_DV)
    s0_spec = (pl.BlockSpec((1,) + hh, lambda b, t: (0, 0, 0, 0)) if shared_s0 else
               pl.BlockSpec((nseq,) + hh, lambda b, t: (b, 0, 0, 0)))
    return pl.pallas_call(
        functools.partial(_ret_kernel, shared_s0=shared_s0),
        grid=(batch // nseq, nchunk),
        in_specs=[tok, tok, tok, tok, s0_spec,
                  _const_spec(hh), _const_spec(hh), _const_spec(hh), _const_spec((RET_HEADS, 1, RET_DV))],
        out_specs=(tok, pl.BlockSpec((nseq,) + hh, lambda b, t: (b, 0, 0, 0))),
        out_shape=(jax.ShapeDtypeStruct((batch, tokens, RET_WIDTH), BF16),
                   jax.ShapeDtypeStruct((batch,) + hh, F32)),
        scratch_shapes=[pltpu.VMEM((nseq,) + hh, F32)],
        compiler_params=pltpu.CompilerParams(dimension_semantics=("arbitrary", "arbitrary")),
        name="retention",
    )(rq, rk, rv, g, s0, decay, xi, zeta, gl)


def _ret_tables(length):
    lg = jnp.log(1.0 - 2.0 ** (-5.0 - jnp.arange(RET_HEADS, dtype=F32)))
    n = jnp.arange(RET_CHUNK, dtype=F32)
    live = n < length
    diff = n[:, None] - n[None, :]
    ok = (diff >= 0) & live[:, None] & live[None, :]
    decay = jnp.where(ok[None], jnp.exp(jnp.maximum(diff, 0.0)[None] * lg[:, None, None]), 0.0)
    xi = jnp.exp((n[None, :] + 1.0) * lg[:, None])
    zeta = jnp.where(live[None, :], jnp.exp((length - 1.0 - n)[None, :] * lg[:, None]), 0.0)
    bc = lambda a: jnp.broadcast_to(a[:, :, None], (RET_HEADS, RET_CHUNK, RET_DV)).astype(F32)
    gl = jnp.broadcast_to(jnp.exp(length * lg)[:, None, None], (RET_HEADS, 1, RET_DV)).astype(F32)
    return decay.astype(F32), bc(xi), bc(zeta), gl


def _sample_idx_kernel(pt_ref, qi_ref, wb_ref, kin_ref, *rest, n_steps, topk, dseq, pps):
    pages = rest[:pps]
    bias_ref = rest[pps]
    sc_ref = rest[pps + 1]
    pc = pl.program_id(1)
    n_chunks = n_steps * pps + 1
    qi = qi_ref[0]
    wb = wb_ref[0]

    def scores(ki_t):
        s = jnp.maximum(_dot(qi, ki_t), 0.0) * wb
        return jnp.sum(s.reshape(IDX_HEADS, 8, LANES), axis=0)

    for p in range(pps):
        sc_ref[pc * pps + p] = scores(pages[p][0, 0].astype(BF16))

    @pl.when(pc == n_steps - 1)
    def _():
        row = lax.broadcasted_iota(I32, (8, LANES), 0)
        lane = lax.broadcasted_iota(I32, (8, LANES), 1)
        vis = (lane <= row) & (lane < dseq)
        sc_ref[n_chunks - 1] = jnp.where(vis, scores(kin_ref[0]), -jnp.inf)
        slot0 = lax.broadcasted_iota(I32, (n_chunks, 8, LANES), 0) * LANES

        def count(ind_fn):
            x = ind_fn(sc_ref[...], slot0).astype(F32)
            acc = [x[i] for i in range(COUNT_LANES)]
            for c in range(COUNT_LANES, n_chunks):
                acc[c % COUNT_LANES] = acc[c % COUNT_LANES] + x[c]
            per_lane = (acc[0] + acc[1]) + (acc[2] + acc[3])
            return jnp.sum(per_lane, axis=1, keepdims=True).astype(I32)

        real_query = lax.broadcasted_iota(I32, (8, 1), 0) < dseq
        t_u = _kth_largest(count, topk, (8, 1), (8, LANES))
        short = t_u == 0
        thr = jnp.where(short, -jnp.inf, _pattern_to_float(t_u))
        cnt_gt = count(lambda s, base: jnp.where(s > thr, 1, 0))
        cnt_eq = count(lambda s, base: jnp.where(s == thr, 1, 0))
        need = topk - cnt_gt
        nbits = int(n_chunks * LANES - 1).bit_length()
        has_ties = jnp.max(jnp.where(real_query, jnp.where(short, 0, cnt_eq - need), 0)) > 0
        last = lax.cond(has_ties, _tie_break(count, thr, need, nbits, 2, (8, 1)),
                        lambda _: jnp.full((8, 1), (1 << nbits) - 1, I32), 0)
        last = jnp.where(short, -1, last)
        s = sc_ref[...]
        slot = slot0 + lax.broadcasted_iota(I32, s.shape, 2)
        keep_eq = jnp.where(slot <= last, 0.0, NEG_BIAS)
        bias_ref[0] = jnp.where(s > thr, 0.0, jnp.where(s == thr, keep_eq, NEG_BIAS))


def _page_specs(pps, rows, page_table_cols):
    def spec(p):
        return pl.BlockSpec((1, 1, rows, PAGE_SIZE),
                            lambda b, pc, pt: (0, pt[b * page_table_cols + pc * pps + p], 0, 0))
    return [spec(p) for p in range(pps)]


def _sample_idx(pt_flat, qi_blk, wb, ki_new_t, cache_idx_t, n_pages, topk, dseq):
    nb = qi_blk.shape[0]
    pps = IDX_PAGES_PER_STEP
    n_steps = n_pages // pps
    n_chunks = n_pages + 1
    grid_spec = pltpu.PrefetchScalarGridSpec(
        num_scalar_prefetch=1,
        grid=(nb, n_steps),
        in_specs=[pl.BlockSpec((1, 64, IDX_DIM), lambda b, pc, pt: (b, 0, 0)),
                  pl.BlockSpec((1, 64, LANES), lambda b, pc, pt: (b, 0, 0)),
                  pl.BlockSpec((1, IDX_DIM, LANES), lambda b, pc, pt: (b, 0, 0)),
                  *_page_specs(pps, IDX_DIM, n_pages)],
        out_specs=pl.BlockSpec((1, n_chunks, 8, LANES), lambda b, pc, pt: (b, 0, 0, 0)),
        scratch_shapes=[pltpu.VMEM((n_chunks, 8, LANES), F32)],
    )
    return pl.pallas_call(
        functools.partial(_sample_idx_kernel, n_steps=n_steps, topk=topk, dseq=dseq, pps=pps),
        grid_spec=grid_spec,
        out_shape=jax.ShapeDtypeStruct((nb, n_chunks, 8, LANES), F32),
        compiler_params=pltpu.CompilerParams(dimension_semantics=("arbitrary", "arbitrary")),
        name="sample_idx",
    )(pt_flat, qi_blk, wb, ki_new_t, *([cache_idx_t] * pps))


def _sample_att_kernel(pt_ref, qb_ref, bias_ref, kn_ref, vn_ref, *rest, n_steps, pps):
    kpages = rest[:pps]
    vpages = rest[pps:2 * pps]
    o_ref, m_ref, l_ref, acc_ref, kbuf, vbuf = rest[2 * pps:]
    pc = pl.program_id(1)
    qb = qb_ref[0]
    n_chunks = n_steps * pps + 1

    def rows64(b8):
        return jnp.concatenate([b8] * ATT_HEADS, axis=0)

    @pl.when(pc == 0)
    def _():
        s = _dot(qb, kn_ref[0]) + rows64(bias_ref[0, n_chunks - 1])
        m = jnp.max(s, axis=1, keepdims=True)
        p = jnp.exp2(s - m)
        m_ref[...] = m
        l_ref[...] = jnp.sum(p, axis=1, keepdims=True)
        acc_ref[...] = _dot_nt(p.astype(BF16), vn_ref[0])

    for p in range(pps):
        kbuf[:, p * PAGE_SIZE:(p + 1) * PAGE_SIZE] = kpages[p][0, 0].astype(BF16)
        vbuf[:, p * PAGE_SIZE:(p + 1) * PAGE_SIZE] = vpages[p][0, 0].astype(BF16)
    bias = jnp.concatenate([bias_ref[0, pc * pps + p] for p in range(pps)], axis=1)
    s = _dot(qb, kbuf[...]) + rows64(bias)
    m_old = m_ref[...]
    m_new = jnp.maximum(m_old, jnp.max(s, axis=1, keepdims=True))
    a = jnp.exp2(m_old - m_new)
    p = jnp.exp2(s - m_new)
    m_ref[...] = m_new
    l_ref[...] = a * l_ref[...] + jnp.sum(p, axis=1, keepdims=True)
    acc_ref[...] = a * acc_ref[...] + _dot_nt(p.astype(BF16), vbuf[...])

    @pl.when(pc == n_steps - 1)
    def _():
        o_ref[0] = acc_ref[...] / l_ref[...]


def _sample_att(pt_flat, qb, bias, k_new_t, v_new_t, cache_k_t, cache_v_t, n_pages):
    nb = qb.shape[0]
    pps = ATT_PAGES_PER_STEP
    n_steps = n_pages // pps
    n_chunks = n_pages + 1
    rows = ATT_HEADS * 8
    grid_spec = pltpu.PrefetchScalarGridSpec(
        num_scalar_prefetch=1,
        grid=(nb, n_steps),
        in_specs=[pl.BlockSpec((1, rows, ATT_WIDTH), lambda b, pc, pt: (b, 0, 0)),
                  pl.BlockSpec((1, n_chunks, 8, LANES), lambda b, pc, pt: (b, 0, 0, 0)),
                  pl.BlockSpec((1, ATT_WIDTH, LANES), lambda b, pc, pt: (b, 0, 0)),
                  pl.BlockSpec((1, ATT_WIDTH, LANES), lambda b, pc, pt: (b, 0, 0)),
                  *_page_specs(pps, ATT_WIDTH, n_pages),
                  *_page_specs(pps, ATT_WIDTH, n_pages)],
        out_specs=pl.BlockSpec((1, rows, ATT_WIDTH), lambda b, pc, pt: (b, 0, 0)),
        scratch_shapes=[pltpu.VMEM((rows, 1), F32), pltpu.VMEM((rows, 1), F32),
                        pltpu.VMEM((rows, ATT_WIDTH), F32),
                        pltpu.VMEM((ATT_WIDTH, pps * PAGE_SIZE), BF16),
                        pltpu.VMEM((ATT_WIDTH, pps * PAGE_SIZE), BF16)],
    )
    return pl.pallas_call(
        functools.partial(_sample_att_kernel, n_steps=n_steps, pps=pps),
        grid_spec=grid_spec,
        out_shape=jax.ShapeDtypeStruct((nb, rows, ATT_WIDTH), F32),
        compiler_params=pltpu.CompilerParams(dimension_semantics=("arbitrary", "arbitrary")),
        name="sample_att",
    )(pt_flat, qb, bias, k_new_t, v_new_t, *([cache_k_t] * pps), *([cache_v_t] * pps))


def _rope_tables(pos):
    half = RET_DK // 2
    inv = ROPE_BASE ** (-jnp.arange(half, dtype=F32) / half)
    ang = pos.astype(F32)[:, None] * inv[None, :]
    cos, sin = jnp.cos(ang), jnp.sin(ang)
    return jnp.concatenate([cos, cos], axis=1), jnp.concatenate([-sin, sin], axis=1)


def _pad_rows(a, rows):
    return jnp.pad(a, ((0, rows - a.shape[0]),) + ((0, 0),) * (a.ndim - 1))


def _pad_last(a, n):
    return jnp.pad(a, ((0, 0),) * (a.ndim - 1) + ((0, n - a.shape[-1]),))


def kernel(x_prompt, x_sample, cache_k, cache_v, cache_idx_k, state_ret, page_table, meta_tokens,
           ffn1_w_gate, ffn1_w_up, ffn1_w_down, ln1_g, ln1_b, w_in, w_out, ln2_g, ln2_b,
           ffn2_w_gate, ffn2_w_up, ffn2_w_down, ln3_g, ln3_b):
    batch, seq, d = x_prompt.shape
    nb, dseq, _ = x_sample.shape
    n_pages = page_table.shape[1]
    n_pool = cache_k.shape[1]
    past = n_pages * PAGE_SIZE
    nq = seq // Q_TILE
    assert d == D_MODEL and seq % Q_TILE == 0 and Q_TILE == ROW_TILE == K_TILE
    assert dseq <= 8 and n_pages % IDX_PAGES_PER_STEP == 0 and n_pages % ATT_PAGES_PER_STEP == 0
    ns = nb * dseq
    n_small = ns + N_META
    assert n_small <= ROW_TILE
    meta = slice(ns, n_small)

    bf = lambda a: a.astype(BF16)
    l = 0
    f1 = (bf(ffn1_w_gate[l]), bf(ffn1_w_up[l]), bf(ffn1_w_down[l]))
    f2 = (bf(ffn2_w_gate[l]), bf(ffn2_w_up[l]), bf(ffn2_w_down[l]))
    vec = lambda a: a[l][None, :].astype(F32)
    w = w_in[l]
    offs = np.cumsum([0, 512, 512, 512, 512, 64, 8, 512, 512, 512, 512])
    wq, wk, wv, wqi, wki, wwi, wrq, wrk, wrv, wg = [w[:, offs[i]:offs[i + 1]] for i in range(10)]
    kiw = jnp.concatenate([wki, wwi, jnp.zeros((d, LANES - IDX_DIM - IDX_HEADS), w.dtype)], axis=1)
    wrow = bf(jnp.concatenate([wk, kiw, wrq, wrk, wrv, wg], axis=1))
    wt = bf(jnp.concatenate([wq, wk, wv, wqi, kiw], axis=1).T)
    woa, wor = bf(w_out[l][:ATT_WIDTH]), bf(w_out[l][ATT_WIDTH:])

    xs = _pad_rows(jnp.concatenate([x_sample.reshape(ns, d), meta_tokens.astype(x_prompt.dtype)], axis=0), ROW_TILE)
    xp = x_prompt.reshape(batch * seq, d)
    pos_small = jnp.concatenate([jnp.tile(past + jnp.arange(dseq, dtype=I32), nb),
                                 jnp.arange(N_META, dtype=I32),
                                 jnp.zeros((ROW_TILE - n_small,), I32)])
    cos_s, sin_s = _rope_tables(pos_small)
    cos_p, sin_p = _rope_tables(N_META + jnp.arange(seq, dtype=I32))

    x1p = _ffn_ln(xp, *f1, vec(ln1_g), vec(ln1_b))
    x1s = _ffn_ln(xs, *f1, vec(ln1_g), vec(ln1_b))
    (kb_p, kib_p, rq_p, rk_p, rv_p, g_p, qt_p, ktf_p, vtf_p, vt_p, qit_p, kiwt_p) = _proj(
        x1p, wrow, wt, cos_p, sin_p, batch, nq)
    (kb_s, kib_s, rq_s, rk_s, rv_s, g_s, qt_s, ktf_s, vtf_s, vt_s, qit_s, kiwt_s) = _proj(
        x1s, wrow, wt, cos_s, sin_s, 1, 1)
    ktf_s, vtf_s, kiwt_s = ktf_s[0], vtf_s[0], kiwt_s[0]

    topk_p = min(TOPK_MAX, seq // 4)
    def key_chunks(meta_chunk, real):
        m = jnp.broadcast_to(meta_chunk[None, None], (batch, 1) + meta_chunk.shape)
        z = jnp.zeros((batch, SPARE_CHUNKS) + meta_chunk.shape, meta_chunk.dtype)
        return jnp.concatenate([m, real.reshape((batch, nq) + meta_chunk.shape), z], axis=1)

    k4 = key_chunks(_pad_rows(kb_s[meta], K_TILE), kb_p)
    ki4 = key_chunks(_pad_rows(kib_s[meta], K_TILE), kib_p)
    vt4 = key_chunks(_pad_last(vt_s[0][:, :, meta], K_TILE), vt_p)
    att_p = _prompt_att(qt_p, qit_p, kiwt_p[:, IDX_DIM:IDX_DIM + IDX_HEADS, :], k4, vt4, ki4, batch, nq, topk_p)
    att_p = att_p.reshape(batch * seq, ATT_WIDTH)

    pad_tok = lambda a, n: jnp.pad(a, ((0, 0), (0, RET_CHUNK - n), (0, 0)))
    m3 = lambda a: pad_tok(a[meta][None], N_META)
    zero_state = jnp.zeros((1, RET_HEADS, RET_DK, RET_DV), F32)
    _, s_meta = _retention(m3(rq_s), m3(rk_s), m3(rv_s), m3(g_s), zero_state, _ret_tables(N_META), True)
    b3 = lambda a: a.reshape(batch, seq, RET_WIDTH)
    yret_p, ret_prompt = _retention(b3(rq_p), b3(rk_p), b3(rv_p), b3(g_p), s_meta, _ret_tables(RET_CHUNK), True)
    s3 = lambda a: pad_tok(a[:ns].reshape(nb, dseq, RET_WIDTH), dseq)
    yret_s, ret_sample = _retention(s3(rq_s), s3(rk_s), s3(rv_s), s3(g_s), state_ret[l], _ret_tables(dseq), False)

    topk_s = min(TOPK_MAX, (past + dseq) // 4)
    pt_flat = page_table.reshape(-1).astype(I32)
    cache_k_t = jnp.transpose(cache_k[l], (0, 2, 3, 1)).reshape(1, n_pool, ATT_WIDTH, PAGE_SIZE)
    cache_v_t = jnp.transpose(cache_v[l], (0, 2, 3, 1)).reshape(1, n_pool, ATT_WIDTH, PAGE_SIZE)
    cache_idx_t = jnp.transpose(cache_idx_k[l], (0, 2, 1))[None]
    q_rows = jnp.concatenate([qt_s[0, h, (h % 2) * ATT_DIM:(h % 2 + 1) * ATT_DIM, :] for h in range(ATT_HEADS)],
                             axis=0).T[:ns]
    qi_rows = qit_s[0].reshape(IDX_DIM, IDX_HEADS, ROW_TILE)[:, :, :ns]
    qi_blk = jnp.pad(qi_rows.transpose(2, 1, 0).reshape(nb, dseq, IDX_HEADS, IDX_DIM).transpose(0, 2, 1, 3),
                     ((0, 0), (0, 0), (0, 8 - dseq), (0, 0))).reshape(nb, IDX_HEADS * 8, IDX_DIM)
    wi_s = kiwt_s[IDX_DIM:IDX_DIM + IDX_HEADS, :ns].reshape(IDX_HEADS, nb, dseq)
    wb = jnp.pad(wi_s.transpose(1, 0, 2), ((0, 0), (0, 0), (0, 8 - dseq))).reshape(nb, IDX_HEADS * 8, 1)
    wb = jnp.broadcast_to(wb, (nb, IDX_HEADS * 8, LANES)).astype(F32)
    new_t = lambda a_t: _pad_last(a_t[:, :ns].reshape(a_t.shape[0], nb, dseq).transpose(1, 0, 2), LANES)
    bias_s = _sample_idx(pt_flat, qi_blk, wb, bf(new_t(kiwt_s[:IDX_DIM])), cache_idx_t, n_pages, topk_s, dseq)
    head_mask = (jnp.arange(ATT_WIDTH)[None, :] // ATT_DIM == jnp.arange(ATT_HEADS)[:, None])
    q3 = jnp.pad(q_rows.reshape(nb, dseq, ATT_WIDTH), ((0, 0), (0, 8 - dseq), (0, 0)))
    qb = jnp.where(head_mask[None, :, None, :], q3[:, None, :, :], 0).astype(BF16).reshape(nb, ATT_HEADS * 8, ATT_WIDTH)
    att_s64 = _sample_att(pt_flat, qb, bias_s, bf(new_t(ktf_s)), bf(new_t(vtf_s)), cache_k_t, cache_v_t, n_pages)
    a5 = att_s64.reshape(nb, ATT_HEADS, 8, ATT_HEADS, ATT_DIM)
    att_s = jnp.stack([a5[:, h, :dseq, h, :] for h in range(ATT_HEADS)], axis=2).reshape(ns, ATT_WIDTH)

    lnw = (vec(ln2_g), vec(ln2_b), *f2, vec(ln3_g), vec(ln3_b))
    y_p = _out_ffn(x1p, att_p, yret_p.reshape(batch * seq, RET_WIDTH), woa, wor, *lnw)
    att_small = _pad_rows(bf(att_s), ROW_TILE)
    yret_small = _pad_rows(yret_s[:, :dseq].reshape(ns, RET_WIDTH), ROW_TILE)
    y_s = _out_ffn(x1s, att_small, yret_small, woa, wor, *lnw)

    def with_meta(real_t, small_t, feat_shape):
        m = jnp.broadcast_to(small_t[None, :, meta], (batch, small_t.shape[0], N_META))
        full = jnp.concatenate([m, real_t], axis=2)
        full = full.reshape((batch,) + feat_shape + (seq + N_META,))
        return jnp.moveaxis(full, -1, 1)[None]

    k_prompt = with_meta(ktf_p, ktf_s, (ATT_HEADS, ATT_DIM))
    v_prompt = with_meta(vtf_p, vtf_s, (ATT_HEADS, ATT_DIM))
    idxk_prompt = with_meta(kiwt_p[:, :IDX_DIM], kiwt_s[:IDX_DIM], (IDX_DIM,))
    y_prompt = y_p.reshape(batch, seq, d)
    y_sample = y_s[:ns].reshape(nb, dseq, d)
    k_sample = ktf_s[:, :ns].T.reshape(1, nb, dseq, ATT_HEADS, ATT_DIM)
    v_sample = vtf_s[:, :ns].T.reshape(1, nb, dseq, ATT_HEADS, ATT_DIM)
    idxk_sample = kiwt_s[:IDX_DIM, :ns].T.reshape(1, nb, dseq, IDX_DIM)
    return (y_prompt, y_sample, k_prompt, v_prompt, idxk_prompt, ret_prompt[None],
            k_sample, v_sample, idxk_sample, ret_sample[None])
```

```python
import functools

import jax
import jax.numpy as jnp
import numpy as np
from jax import lax
from jax.experimental import pallas as pl
from jax.experimental.pallas import tpu as pltpu

F32 = jnp.float32
BF16 = jnp.bfloat16
I32 = jnp.int32

D_MODEL = 1024
N_META = 16
ATT_DIM = 64
ATT_HEADS = 8
ATT_WIDTH = ATT_HEADS * ATT_DIM
IDX_HEADS = 8
IDX_DIM = 64
TOPK_MAX = 256
RET_HEADS = 4
RET_DK = 128
RET_DV = 128
RET_WIDTH = RET_HEADS * RET_DV
PAGE_SIZE = 128
ROPE_BASE = 10000.0
LN_EPS = 1e-5
GN_EPS = 1e-5
DEPTH = 1
ALPHA = (2.0 * DEPTH) ** 0.25

LANES = 128
ROW_TILE = 256
FFN_TILE = 512
Q_TILE = 256
K_TILE = 256
RET_CHUNK = 128
RET_SEQS_PER_STEP = 4
IDX_PAGES_PER_STEP = 32
ATT_PAGES_PER_STEP = 32
LOG2E = 1.4426950408889634
NEG_BIAS = -1e30
INT_MIN = -2147483648
VMEM_LIMIT = 56 * 1024 * 1024


def _dot(a, b):
    return jnp.dot(a, b, preferred_element_type=F32)


def _dot_nt(a, b):
    return lax.dot_general(a, b, (((1,), (1,)), ((), ())), preferred_element_type=F32)


def _layernorm(y, g, b):
    mu = jnp.mean(y, axis=-1, keepdims=True)
    d = y - mu
    var = jnp.mean(d * d, axis=-1, keepdims=True)
    return d * lax.rsqrt(var + LN_EPS) * g + b


def _swiglu(xb, wg_ref, wu_ref, wd_ref):
    hg = _dot(xb, wg_ref[...])
    hu = _dot(xb, wu_ref[...])
    act = (hg * jax.nn.sigmoid(hg)) * hu
    return _dot(act.astype(BF16), wd_ref[...])


def _ffn_ln_kernel(x_ref, wg_ref, wu_ref, wd_ref, g_ref, b_ref, o_ref):
    x = x_ref[...]
    y = ALPHA * x + 0.5 * _swiglu(x.astype(BF16), wg_ref, wu_ref, wd_ref)
    o_ref[...] = _layernorm(y, g_ref[...], b_ref[...])


def _const_spec(shape):
    return pl.BlockSpec(shape, lambda *_: (0,) * len(shape))


def _weight_spec(shape):
    return pl.BlockSpec(shape, lambda *_: (0,) * len(shape), pipeline_mode=pl.Buffered(1))


def _ffn_ln(x, wg, wu, wd, g, b):
    rows, d = x.shape
    dff = wg.shape[1]
    tm = min(FFN_TILE, rows)
    return pl.pallas_call(
        _ffn_ln_kernel,
        grid=(rows // tm,),
        in_specs=[pl.BlockSpec((tm, d), lambda i: (i, 0)),
                  _weight_spec((d, dff)), _weight_spec((d, dff)), _weight_spec((dff, d)),
                  _const_spec((1, d)), _const_spec((1, d))],
        out_specs=pl.BlockSpec((tm, d), lambda i: (i, 0)),
        out_shape=jax.ShapeDtypeStruct((rows, d), F32),
        compiler_params=pltpu.CompilerParams(
            dimension_semantics=("arbitrary",), vmem_limit_bytes=VMEM_LIMIT),
        name="ffn_ln",
    )(x, wg, wu, wd, g, b)


def _out_ffn_kernel(x1_ref, att_ref, yret_ref, woa_ref, wor_ref, g2_ref, b2_ref,
                    wg_ref, wu_ref, wd_ref, g3_ref, b3_ref, o_ref):
    m = _dot(att_ref[...], woa_ref[...]) + _dot(yret_ref[...], wor_ref[...])
    x2 = _layernorm(ALPHA * x1_ref[...] + m, g2_ref[...], b2_ref[...])
    y = ALPHA * x2 + 0.5 * _swiglu(x2.astype(BF16), wg_ref, wu_ref, wd_ref)
    o_ref[...] = _layernorm(y, g3_ref[...], b3_ref[...])


def _out_ffn(x1, att, yret, woa, wor, g2, b2, wg, wu, wd, g3, b3):
    rows, d = x1.shape
    dff = wg.shape[1]
    tm = min(FFN_TILE, rows)
    row = lambda w: pl.BlockSpec((tm, w), lambda i: (i, 0))
    return pl.pallas_call(
        _out_ffn_kernel,
        grid=(rows // tm,),
        in_specs=[row(d), row(ATT_WIDTH), row(RET_WIDTH),
                  _weight_spec((ATT_WIDTH, d)), _weight_spec((RET_WIDTH, d)),
                  _const_spec((1, d)), _const_spec((1, d)),
                  _weight_spec((d, dff)), _weight_spec((d, dff)), _weight_spec((dff, d)),
                  _const_spec((1, d)), _const_spec((1, d))],
        out_specs=row(d),
        out_shape=jax.ShapeDtypeStruct((rows, d), F32),
        compiler_params=pltpu.CompilerParams(
            dimension_semantics=("arbitrary",), vmem_limit_bytes=VMEM_LIMIT),
        name="out_ffn",
    )(x1, att, yret, woa, wor, g2, b2, wg, wu, wd, g3, b3)


W_ROW_COLS = ATT_WIDTH + LANES + 4 * RET_WIDTH
W_T_ROWS = 4 * ATT_WIDTH + LANES
V_ROWS = ATT_DIM + 16


def _rope(x, cos2, sin2):
    return x * cos2 + pltpu.roll(x, RET_DK // 2, 1) * sin2


def _proj_kernel(x1_ref, wrow_ref, wt_ref, cos_ref, sin_ref,
                 kb_ref, kib_ref, rq_ref, rk_ref, rv_ref, g_ref,
                 qt_ref, ktf_ref, vtf_ref, vt_ref, qit_ref, kiwt_ref):
    xb = x1_ref[...].astype(BF16)
    z = _dot(xb, wrow_ref[...])
    kb_ref[...] = z[:, 0:ATT_WIDTH].astype(BF16)
    kib_ref[...] = z[:, ATT_WIDTH:ATT_WIDTH + IDX_DIM].astype(BF16)
    o = ATT_WIDTH + LANES
    cos2 = cos_ref[...]
    sin2 = sin_ref[...]
    for h in range(RET_HEADS):
        sl = slice(h * RET_DK, (h + 1) * RET_DK)
        rq = z[:, o + h * RET_DK:o + (h + 1) * RET_DK]
        rk = z[:, o + RET_WIDTH + h * RET_DK:o + RET_WIDTH + (h + 1) * RET_DK]
        rq_ref[:, sl] = _rope(rq, cos2, sin2).astype(BF16)
        rk_ref[:, sl] = (_rope(rk, cos2, sin2) * (RET_DK ** -0.5)).astype(BF16)
    rv_ref[...] = z[:, o + 2 * RET_WIDTH:o + 3 * RET_WIDTH].astype(BF16)
    g_ref[...] = z[:, o + 3 * RET_WIDTH:o + 4 * RET_WIDTH]

    zt = _dot_nt(wt_ref[...], xb)
    tm = xb.shape[0]
    zero = jnp.zeros((ATT_DIM, tm), BF16)
    for h in range(ATT_HEADS):
        qh = (zt[h * ATT_DIM:(h + 1) * ATT_DIM, :] * (ATT_DIM ** -0.5 * LOG2E)).astype(BF16)
        qt_ref[0, h] = jnp.concatenate([qh, zero] if h % 2 == 0 else [zero, qh], axis=0)
    ktf_ref[0] = zt[ATT_WIDTH:2 * ATT_WIDTH, :]
    vt = zt[2 * ATT_WIDTH:3 * ATT_WIDTH, :]
    vtf_ref[0] = vt
    ones_blk = jnp.where(lax.broadcasted_iota(I32, (V_ROWS - ATT_DIM, tm), 0) == 0, 1.0, 0.0).astype(BF16)
    for h in range(ATT_HEADS):
        vt_ref[0, h] = jnp.concatenate([vt[h * ATT_DIM:(h + 1) * ATT_DIM, :].astype(BF16), ones_blk], axis=0)
    for h in range(IDX_HEADS):
        qih = zt[3 * ATT_WIDTH + h * IDX_DIM:3 * ATT_WIDTH + (h + 1) * IDX_DIM, :]
        qit_ref[0, :, h * tm:(h + 1) * tm] = (qih * (IDX_DIM ** -0.5)).astype(BF16)
    kiwt_ref[0] = zt[4 * ATT_WIDTH:4 * ATT_WIDTH + LANES, :]


def _proj(x1, wrow, wt, cos2, sin2, batch, tiles_per_seq):
    rows, d = x1.shape
    tm = ROW_TILE
    nt = rows // tm
    seq = tiles_per_seq * tm
    row = lambda w: pl.BlockSpec((tm, w), lambda i: (i, 0))
    tab = pl.BlockSpec((tm, LANES), lambda i: (i % tiles_per_seq, 0))
    tcol = lambda r: pl.BlockSpec((1, r, tm), lambda i: (i // tiles_per_seq, 0, i % tiles_per_seq))
    out_shape = (
        jax.ShapeDtypeStruct((rows, ATT_WIDTH), BF16),
        jax.ShapeDtypeStruct((rows, IDX_DIM), BF16),
        jax.ShapeDtypeStruct((rows, RET_WIDTH), BF16),
        jax.ShapeDtypeStruct((rows, RET_WIDTH), BF16),
        jax.ShapeDtypeStruct((rows, RET_WIDTH), BF16),
        jax.ShapeDtypeStruct((rows, RET_WIDTH), F32),
        jax.ShapeDtypeStruct((nt, ATT_HEADS, 2 * ATT_DIM, tm), BF16),
        jax.ShapeDtypeStruct((batch, ATT_WIDTH, seq), F32),
        jax.ShapeDtypeStruct((batch, ATT_WIDTH, seq), F32),
        jax.ShapeDtypeStruct((nt, ATT_HEADS, V_ROWS, tm), BF16),
        jax.ShapeDtypeStruct((nt, IDX_DIM, IDX_HEADS * tm), BF16),
        jax.ShapeDtypeStruct((batch, LANES, seq), F32),
    )
    out_specs = (
        row(ATT_WIDTH), row(IDX_DIM), row(RET_WIDTH), row(RET_WIDTH), row(RET_WIDTH), row(RET_WIDTH),
        pl.BlockSpec((1, ATT_HEADS, 2 * ATT_DIM, tm), lambda i: (i, 0, 0, 0)),
        tcol(ATT_WIDTH), tcol(ATT_WIDTH),
        pl.BlockSpec((1, ATT_HEADS, V_ROWS, tm), lambda i: (i, 0, 0, 0)),
        pl.BlockSpec((1, IDX_DIM, IDX_HEADS * tm), lambda i: (i, 0, 0)),
        tcol(LANES),
    )
    return pl.pallas_call(
        _proj_kernel,
        grid=(nt,),
        in_specs=[row(d), _const_spec((d, W_ROW_COLS)), _const_spec((W_T_ROWS, d)), tab, tab],
        out_specs=out_specs,
        out_shape=out_shape,
        compiler_params=pltpu.CompilerParams(
            dimension_semantics=("arbitrary",), vmem_limit_bytes=VMEM_LIMIT),
        name="proj",
    )(x1, wrow, wt, cos2, sin2)


def _pattern_to_float(u):
    key = u ^ INT_MIN
    bits = jnp.where(key >= 0, key, key ^ 0x7FFFFFFF)
    return lax.bitcast_convert_type(bits, F32)


def _kth_largest(count, topk, shape, tile_shape):
    def step(i, t_u):
        cand = t_u | (1 << (31 - i))
        cand_f = jnp.broadcast_to(_pattern_to_float(cand), tile_shape)
        cnt = count(lambda s, base: jnp.where(s >= cand_f, 1, 0))
        return jnp.where(cnt >= topk, cand, t_u)

    return lax.fori_loop(0, 32, step, jnp.zeros(shape, I32))


def _tie_break(count, thr, need, nbits, slot_axis, shape):
    def search(_):
        def step(i, p):
            cand = p | (1 << (nbits - 1 - i))
            cnt = count(lambda s, base: jnp.where(
                s == thr, jnp.where((base + lax.broadcasted_iota(I32, s.shape, slot_axis)) < cand, 1, 0), 0))
            return jnp.where(cnt < need, cand, p)
        return lax.fori_loop(0, nbits, step, jnp.zeros(shape, I32))

    return search


SPARE_CHUNKS = 4
COUNT_LANES = 4


def _prompt_att_kernel(qt_ref, qit_ref, wit_ref, k_ref, vt_ref, ki_ref, o_ref,
                       sc_ref, ot_ref, m_ref, s0_ref, s1_ref, c0_ref, c1_ref, *, topk, nq):
    j = pl.program_id(1)
    tq, tk = Q_TILE, K_TILE
    n_chunks = j + 2
    n_pairs = lax.shift_right_logical(n_chunks + 1, 1)
    col = lax.broadcasted_iota(I32, (1, tq), 1)
    rows_k = lax.broadcasted_iota(I32, (tk, tq), 0)

    def rows_of(c, n=1):
        return pl.ds(pl.multiple_of(c * tk, tk), n * tk)

    def p1(i, carry):
        for c in (2 * i, 2 * i + 1):
            acc = None
            for h in range(IDX_HEADS):
                s = _dot(ki_ref[0, c], qit_ref[0, :, h * tq:(h + 1) * tq])
                t = jnp.maximum(s, 0.0) * wit_ref[0, h:h + 1, :]
                acc = t if acc is None else acc + t
            last_row = jnp.where(c == 0, N_META - 1, j * tq + col - (c - 1) * tk)
            sc_ref[rows_of(c), :] = jnp.where(rows_k <= last_row, acc, -jnp.inf)
        return carry

    lax.fori_loop(0, n_pairs, p1, 0)
    sc_ref[rows_of(n_chunks, SPARE_CHUNKS), :] = jnp.full((SPARE_CHUNKS * tk, tq), -jnp.inf, F32)

    def count(ind_fn):
        def body(c4, part):
            base = pl.multiple_of(SPARE_CHUNKS * c4 * tk, SPARE_CHUNKS * tk)
            x = ind_fn(sc_ref[pl.ds(base, SPARE_CHUNKS * tk), :], base)
            return part + jnp.sum(x.reshape(-1, COUNT_LANES, 8, tq), axis=0)

        n_groups = lax.shift_right_logical(n_chunks + SPARE_CHUNKS - 1, 2)
        part = lax.fori_loop(0, n_groups, body, jnp.zeros((COUNT_LANES, 8, tq), I32))
        return jnp.sum(part.reshape(COUNT_LANES * 8, tq), axis=0, keepdims=True)

    t_u = _kth_largest(count, topk, (1, tq), (1, tq))
    short = t_u == 0
    thr = jnp.where(short, -jnp.inf, _pattern_to_float(t_u))
    cnt_gt = count(lambda s, base: jnp.where(s > thr, 1, 0))
    cnt_eq = count(lambda s, base: jnp.where(s == thr, 1, 0))
    need = topk - cnt_gt
    nbits = int(tk * (nq + 1 + SPARE_CHUNKS) - 1).bit_length()
    has_ties = jnp.max(jnp.where(short, 0, cnt_eq - need)) > 0
    last = lax.cond(has_ties, _tie_break(count, thr, need, nbits, 0, (1, tq)),
                    lambda _: jnp.full((1, tq), (1 << nbits) - 1, I32), 0)
    last = jnp.where(short, -1, last)

    def p3(c, carry):
        s = sc_ref[rows_of(c), :]
        keep_eq = jnp.where(c * tk + rows_k <= last, 0.0, NEG_BIAS)
        sc_ref[rows_of(c), :] = jnp.where(s > thr, 0.0, jnp.where(s == thr, keep_eq, NEG_BIAS))
        return carry

    lax.fori_loop(0, n_chunks + SPARE_CHUNKS, p3, 0)

    def k_lanes(h):
        return slice((h // 2) * LANES, (h // 2 + 1) * LANES)

    rep = lambda x: jnp.broadcast_to(x, (8, tq))

    def logits_stage(c, h, s_ref, cmax_ref):
        s = _dot(k_ref[0, c, :, k_lanes(h)], qt_ref[0, h]) + sc_ref[rows_of(c), :]
        s_ref[h] = s
        cmax_ref[h] = rep(jnp.max(s, axis=0, keepdims=True))

    def softmax_stage(c, h, s_ref, cmax_ref):
        m_old = m_ref[h]
        m_new = jnp.maximum(m_old, cmax_ref[h])
        a = jnp.exp2(m_old - m_new)
        p = jnp.exp2(s_ref[h] - m_new[0:1, :])
        m_ref[h] = m_new
        ot_ref[h] = a[0:1, :] * ot_ref[h] + _dot(vt_ref[0, c, h], p.astype(BF16))

    m_ref[...] = jnp.full(m_ref.shape, NEG_BIAS, F32)
    ot_ref[...] = jnp.zeros(ot_ref.shape, F32)
    for h in range(ATT_HEADS):
        logits_stage(0, h, s0_ref, c0_ref)

    def p4(i, carry):
        for h in range(ATT_HEADS):
            logits_stage(2 * i + 1, h, s1_ref, c1_ref)
            softmax_stage(2 * i, h, s0_ref, c0_ref)
        for h in range(ATT_HEADS):
            logits_stage(2 * i + 2, h, s0_ref, c0_ref)
            softmax_stage(2 * i + 1, h, s1_ref, c1_ref)
        return carry

    lax.fori_loop(0, n_pairs, p4, 0)
    out_t = jnp.concatenate([ot_ref[h, 0:ATT_DIM, :] / ot_ref[h, ATT_DIM:ATT_DIM + 1, :]
                             for h in range(ATT_HEADS)], axis=0)
    o_ref[0] = out_t.T.astype(BF16)


def _prompt_att(qt, qit, wit, k4, vt4, ki4, batch, nq, topk):
    tq = Q_TILE
    nc = nq + 1 + SPARE_CHUNKS
    per_batch = lambda shape: pl.BlockSpec((1,) + shape, lambda b, j: (b,) + (0,) * len(shape))
    hs = (ATT_HEADS, K_TILE, tq)
    return pl.pallas_call(
        functools.partial(_prompt_att_kernel, topk=topk, nq=nq),
        grid=(batch, nq),
        in_specs=[
            pl.BlockSpec((1, ATT_HEADS, 2 * ATT_DIM, tq), lambda b, j: (b * nq + j, 0, 0, 0)),
            pl.BlockSpec((1, IDX_DIM, IDX_HEADS * tq), lambda b, j: (b * nq + j, 0, 0)),
            pl.BlockSpec((1, IDX_HEADS, tq), lambda b, j: (b, 0, j)),
            per_batch((nc, K_TILE, ATT_WIDTH)),
            per_batch((nc, ATT_HEADS, V_ROWS, K_TILE)),
            per_batch((nc, K_TILE, IDX_DIM)),
        ],
        out_specs=pl.BlockSpec((1, tq, ATT_WIDTH), lambda b, j: (b * nq + j, 0, 0)),
        out_shape=jax.ShapeDtypeStruct((batch * nq, tq, ATT_WIDTH), BF16),
        scratch_shapes=[pltpu.VMEM((nc * K_TILE, tq), F32), pltpu.VMEM((ATT_HEADS, V_ROWS, tq), F32),
                        pltpu.VMEM((ATT_HEADS, 8, tq), F32),
                        pltpu.VMEM(hs, F32), pltpu.VMEM(hs, F32),
                        pltpu.VMEM((ATT_HEADS, 8, tq), F32), pltpu.VMEM((ATT_HEADS, 8, tq), F32)],
        compiler_params=pltpu.CompilerParams(
            dimension_semantics=("arbitrary", "arbitrary"), vmem_limit_bytes=VMEM_LIMIT),
        name="prompt_att",
    )(qt, qit, wit, k4, vt4, ki4)


def _ret_kernel(rq_ref, rk_ref, rv_ref, g_ref, s0_ref, decay_ref, xi_ref, zeta_ref, gl_ref,
                y_ref, sfin_ref, state_ref, *, shared_s0):
    t = pl.program_id(1)
    nseq = state_ref.shape[0]

    @pl.when(t == 0)
    def _():
        for r in range(nseq):
            state_ref[r] = s0_ref[0 if shared_s0 else r]

    for r in range(nseq):
        for h in range(RET_HEADS):
            sl = slice(h * RET_DK, (h + 1) * RET_DK)
            q = rq_ref[r, :, sl]
            k = rk_ref[r, :, sl]
            v = rv_ref[r, :, sl]
            sp = state_ref[r, h]
            inner = _dot_nt(q, k) * decay_ref[h]
            ret = _dot(inner.astype(BF16), v) + _dot(q, sp.astype(BF16)) * xi_ref[h]
            kz = (k.astype(F32) * zeta_ref[h]).T.astype(BF16)
            state_ref[r, h] = gl_ref[h] * sp + _dot(kz, v)
            mu = jnp.mean(ret, axis=-1, keepdims=True)
            d = ret - mu
            var = jnp.mean(d * d, axis=-1, keepdims=True)
            retn = d * lax.rsqrt(var + GN_EPS)
            g = g_ref[r, :, sl]
            y_ref[r, :, sl] = ((g * jax.nn.sigmoid(g)) * retn).astype(BF16)

    @pl.when(t == pl.num_programs(1) - 1)
    def _():
        sfin_ref[...] = state_ref[...]


def _retention(rq, rk, rv, g, s0, tables, shared_s0):
    batch, tokens, _ = rq.shape
    nchunk = tokens // RET_CHUNK
    nseq = RET_SEQS_PER_STEP if batch % RET_SEQS_PER_STEP == 0 else 1
    decay, xi, zeta, gl = tables
    tok = pl.BlockSpec((nseq, RET_CHUNK, RET_WIDTH), lambda b, t: (b, t, 0))
    hh = (RET_HEADS, RET_DK, RET_DV)
    s0_spec = (pl.BlockSpec((1,) + hh, lambda b, t: (0, 0, 0, 0)) if shared_s0 else
               pl.BlockSpec((nseq,) + hh, lambda b, t: (b, 0, 0, 0)))
    return pl.pallas_call(
        functools.partial(_ret_kernel, shared_s0=shared_s0),
        grid=(batch // nseq, nchunk),
        in_specs=[tok, tok, tok, tok, s0_spec,
                  _const_spec(hh), _const_spec(hh), _const_spec(hh), _const_spec((RET_HEADS, 1, RET_DV))],
        out_specs=(tok, pl.BlockSpec((nseq,) + hh, lambda b, t: (b, 0, 0, 0))),
        out_shape=(jax.ShapeDtypeStruct((batch, tokens, RET_WIDTH), BF16),
                   jax.ShapeDtypeStruct((batch,) + hh, F32)),
        scratch_shapes=[pltpu.VMEM((nseq,) + hh, F32)],
        compiler_params=pltpu.CompilerParams(dimension_semantics=("arbitrary", "arbitrary")),
        name="retention",
    )(rq, rk, rv, g, s0, decay, xi, zeta, gl)


def _ret_tables(length):
    lg = jnp.log(1.0 - 2.0 ** (-5.0 - jnp.arange(RET_HEADS, dtype=F32)))
    n = jnp.arange(RET_CHUNK, dtype=F32)
    live = n < length
    diff = n[:, None] - n[None, :]
    ok = (diff >= 0) & live[:, None] & live[None, :]
    decay = jnp.where(ok[None], jnp.exp(jnp.maximum(diff, 0.0)[None] * lg[:, None, None]), 0.0)
    xi = jnp.exp((n[None, :] + 1.0) * lg[:, None])
    zeta = jnp.where(live[None, :], jnp.exp((length - 1.0 - n)[None, :] * lg[:, None]), 0.0)
    bc = lambda a: jnp.broadcast_to(a[:, :, None], (RET_HEADS, RET_CHUNK, RET_DV)).astype(F32)
    gl = jnp.broadcast_to(jnp.exp(length * lg)[:, None, None], (RET_HEADS, 1, RET_DV)).astype(F32)
    return decay.astype(F32), bc(xi), bc(zeta), gl


def _sample_idx_kernel(pt_ref, qi_ref, wb_ref, kin_ref, *rest, n_steps, topk, dseq, pps):
    pages = rest[:pps]
    bias_ref = rest[pps]
    sc_ref = rest[pps + 1]
    pc = pl.program_id(1)
    n_chunks = n_steps * pps + 1
    qi = qi_ref[0]
    wb = wb_ref[0]

    def scores(ki_t):
        n = ki_t.shape[1]
        s = jnp.maximum(_dot(qi, ki_t), 0.0) * jnp.concatenate([wb] * (n // LANES), axis=1)
        return jnp.sum(s.reshape(IDX_HEADS, 8, n), axis=0)

    s_step = scores(jnp.concatenate([pages[p][0, 0].astype(BF16) for p in range(pps)], axis=1))
    for p in range(pps):
        sc_ref[pc * pps + p] = s_step[:, p * LANES:(p + 1) * LANES]

    @pl.when(pc == n_steps - 1)
    def _():
        row = lax.broadcasted_iota(I32, (8, LANES), 0)
        lane = lax.broadcasted_iota(I32, (8, LANES), 1)
        vis = (lane <= row) & (lane < dseq)
        sc_ref[n_chunks - 1] = jnp.where(vis, scores(kin_ref[0]), -jnp.inf)
        slot0 = lax.broadcasted_iota(I32, (n_chunks, 8, LANES), 0) * LANES

        def count(ind_fn):
            x = ind_fn(sc_ref[...], slot0).astype(F32)
            acc = [x[i] for i in range(COUNT_LANES)]
            for c in range(COUNT_LANES, n_chunks):
                acc[c % COUNT_LANES] = acc[c % COUNT_LANES] + x[c]
            per_lane = (acc[0] + acc[1]) + (acc[2] + acc[3])
            return jnp.sum(per_lane, axis=1, keepdims=True).astype(I32)

        real_query = lax.broadcasted_iota(I32, (8, 1), 0) < dseq
        t_u = _kth_largest(count, topk, (8, 1), (8, LANES))
        short = t_u == 0
        thr = jnp.where(short, -jnp.inf, _pattern_to_float(t_u))
        cnt_gt = count(lambda s, base: jnp.where(s > thr, 1, 0))
        cnt_eq = count(lambda s, base: jnp.where(s == thr, 1, 0))
        need = topk - cnt_gt
        nbits = int(n_chunks * LANES - 1).bit_length()
        has_ties = jnp.max(jnp.where(real_query, jnp.where(short, 0, cnt_eq - need), 0)) > 0
        last = lax.cond(has_ties, _tie_break(count, thr, need, nbits, 2, (8, 1)),
                        lambda _: jnp.full((8, 1), (1 << nbits) - 1, I32), 0)
        last = jnp.where(short, -1, last)
        s = sc_ref[...]
        slot = slot0 + lax.broadcasted_iota(I32, s.shape, 2)
        keep_eq = jnp.where(slot <= last, 0.0, NEG_BIAS)
        bias_ref[0] = jnp.where(s > thr, 0.0, jnp.where(s == thr, keep_eq, NEG_BIAS))


def _page_specs(pps, rows, page_table_cols):
    def spec(p):
        return pl.BlockSpec((1, 1, rows, PAGE_SIZE),
                            lambda b, pc, pt: (0, pt[b * page_table_cols + pc * pps + p], 0, 0))
    return [spec(p) for p in range(pps)]


def _sample_idx(pt_flat, qi_blk, wb, ki_new_t, cache_idx_t, n_pages, topk, dseq):
    nb = qi_blk.shape[0]
    pps = IDX_PAGES_PER_STEP
    n_steps = n_pages // pps
    n_chunks = n_pages + 1
    grid_spec = pltpu.PrefetchScalarGridSpec(
        num_scalar_prefetch=1,
        grid=(nb, n_steps),
        in_specs=[pl.BlockSpec((1, 64, IDX_DIM), lambda b, pc, pt: (b, 0, 0)),
                  pl.BlockSpec((1, 64, LANES), lambda b, pc, pt: (b, 0, 0)),
                  pl.BlockSpec((1, IDX_DIM, LANES), lambda b, pc, pt: (b, 0, 0)),
                  *_page_specs(pps, IDX_DIM, n_pages)],
        out_specs=pl.BlockSpec((1, n_chunks, 8, LANES), lambda b, pc, pt: (b, 0, 0, 0)),
        scratch_shapes=[pltpu.VMEM((n_chunks, 8, LANES), F32)],
    )
    return pl.pallas_call(
        functools.partial(_sample_idx_kernel, n_steps=n_steps, topk=topk, dseq=dseq, pps=pps),
        grid_spec=grid_spec,
        out_shape=jax.ShapeDtypeStruct((nb, n_chunks, 8, LANES), F32),
        compiler_params=pltpu.CompilerParams(dimension_semantics=("arbitrary", "arbitrary")),
        name="sample_idx",
    )(pt_flat, qi_blk, wb, ki_new_t, *([cache_idx_t] * pps))


def _sample_att_kernel(pt_ref, qb_ref, bias_ref, kn_ref, vn_ref, *rest, n_steps, pps):
    kpages = rest[:pps]
    vpages = rest[pps:2 * pps]
    o_ref, m_ref, l_ref, acc_ref, kbuf, vbuf = rest[2 * pps:]
    pc = pl.program_id(1)
    qb = qb_ref[0]
    n_chunks = n_steps * pps + 1

    def rows64(b8):
        return jnp.concatenate([b8] * ATT_HEADS, axis=0)

    @pl.when(pc == 0)
    def _():
        s = _dot(qb, kn_ref[0]) + rows64(bias_ref[0, n_chunks - 1])
        m = jnp.max(s, axis=1, keepdims=True)
        p = jnp.exp2(s - m)
        m_ref[...] = m
        l_ref[...] = jnp.sum(p, axis=1, keepdims=True)
        acc_ref[...] = _dot_nt(p.astype(BF16), vn_ref[0])

    for p in range(pps):
        kbuf[:, p * PAGE_SIZE:(p + 1) * PAGE_SIZE] = kpages[p][0, 0].astype(BF16)
        vbuf[:, p * PAGE_SIZE:(p + 1) * PAGE_SIZE] = vpages[p][0, 0].astype(BF16)
    bias = jnp.concatenate([bias_ref[0, pc * pps + p] for p in range(pps)], axis=1)
    s = _dot(qb, kbuf[...]) + rows64(bias)
    m_old = m_ref[...]
    m_new = jnp.maximum(m_old, jnp.max(s, axis=1, keepdims=True))
    a = jnp.exp2(m_old - m_new)
    p = jnp.exp2(s - m_new)
    m_ref[...] = m_new
    l_ref[...] = a * l_ref[...] + jnp.sum(p, axis=1, keepdims=True)
    acc_ref[...] = a * acc_ref[...] + _dot_nt(p.astype(BF16), vbuf[...])

    @pl.when(pc == n_steps - 1)
    def _():
        o_ref[0] = acc_ref[...] / l_ref[...]


def _sample_att(pt_flat, qb, bias, k_new_t, v_new_t, cache_k_t, cache_v_t, n_pages):
    nb = qb.shape[0]
    pps = ATT_PAGES_PER_STEP
    n_steps = n_pages // pps
    n_chunks = n_pages + 1
    rows = ATT_HEADS * 8
    grid_spec = pltpu.PrefetchScalarGridSpec(
        num_scalar_prefetch=1,
        grid=(nb, n_steps),
        in_specs=[pl.BlockSpec((1, rows, ATT_WIDTH), lambda b, pc, pt: (b, 0, 0)),
                  pl.BlockSpec((1, n_chunks, 8, LANES), lambda b, pc, pt: (b, 0, 0, 0)),
                  pl.BlockSpec((1, ATT_WIDTH, LANES), lambda b, pc, pt: (b, 0, 0)),
                  pl.BlockSpec((1, ATT_WIDTH, LANES), lambda b, pc, pt: (b, 0, 0)),
                  *_page_specs(pps, ATT_WIDTH, n_pages),
                  *_page_specs(pps, ATT_WIDTH, n_pages)],
        out_specs=pl.BlockSpec((1, rows, ATT_WIDTH), lambda b, pc, pt: (b, 0, 0)),
        scratch_shapes=[pltpu.VMEM((rows, 1), F32), pltpu.VMEM((rows, 1), F32),
                        pltpu.VMEM((rows, ATT_WIDTH), F32),
                        pltpu.VMEM((ATT_WIDTH, pps * PAGE_SIZE), BF16),
                        pltpu.VMEM((ATT_WIDTH, pps * PAGE_SIZE), BF16)],
    )
    return pl.pallas_call(
        functools.partial(_sample_att_kernel, n_steps=n_steps, pps=pps),
        grid_spec=grid_spec,
        out_shape=jax.ShapeDtypeStruct((nb, rows, ATT_WIDTH), F32),
        compiler_params=pltpu.CompilerParams(
            dimension_semantics=("arbitrary", "arbitrary"), vmem_limit_bytes=VMEM_LIMIT),
        name="sample_att",
    )(pt_flat, qb, bias, k_new_t, v_new_t, *([cache_k_t] * pps), *([cache_v_t] * pps))


def _rope_tables(pos):
    half = RET_DK // 2
    inv = ROPE_BASE ** (-jnp.arange(half, dtype=F32) / half)
    ang = pos.astype(F32)[:, None] * inv[None, :]
    cos, sin = jnp.cos(ang), jnp.sin(ang)
    return jnp.concatenate([cos, cos], axis=1), jnp.concatenate([-sin, sin], axis=1)


def _pad_rows(a, rows):
    return jnp.pad(a, ((0, rows - a.shape[0]),) + ((0, 0),) * (a.ndim - 1))


def _pad_last(a, n):
    return jnp.pad(a, ((0, 0),) * (a.ndim - 1) + ((0, n - a.shape[-1]),))


def kernel(x_prompt, x_sample, cache_k, cache_v, cache_idx_k, state_ret, page_table, meta_tokens,
           ffn1_w_gate, ffn1_w_up, ffn1_w_down, ln1_g, ln1_b, w_in, w_out, ln2_g, ln2_b,
           ffn2_w_gate, ffn2_w_up, ffn2_w_down, ln3_g, ln3_b):
    batch, seq, d = x_prompt.shape
    nb, dseq, _ = x_sample.shape
    n_pages = page_table.shape[1]
    n_pool = cache_k.shape[1]
    past = n_pages * PAGE_SIZE
    nq = seq // Q_TILE
    assert d == D_MODEL and seq % Q_TILE == 0 and Q_TILE == ROW_TILE == K_TILE
    assert dseq <= 8 and n_pages % IDX_PAGES_PER_STEP == 0 and n_pages % ATT_PAGES_PER_STEP == 0
    ns = nb * dseq
    n_small = ns + N_META
    assert n_small <= ROW_TILE
    meta = slice(ns, n_small)

    bf = lambda a: a.astype(BF16)
    l = 0
    f1 = (bf(ffn1_w_gate[l]), bf(ffn1_w_up[l]), bf(ffn1_w_down[l]))
    f2 = (bf(ffn2_w_gate[l]), bf(ffn2_w_up[l]), bf(ffn2_w_down[l]))
    vec = lambda a: a[l][None, :].astype(F32)
    w = w_in[l]
    offs = np.cumsum([0, 512, 512, 512, 512, 64, 8, 512, 512, 512, 512])
    wq, wk, wv, wqi, wki, wwi, wrq, wrk, wrv, wg = [w[:, offs[i]:offs[i + 1]] for i in range(10)]
    kiw = jnp.concatenate([wki, wwi, jnp.zeros((d, LANES - IDX_DIM - IDX_HEADS), w.dtype)], axis=1)
    wrow = bf(jnp.concatenate([wk, kiw, wrq, wrk, wrv, wg], axis=1))
    wt = bf(jnp.concatenate([wq, wk, wv, wqi, kiw], axis=1).T)
    woa, wor = bf(w_out[l][:ATT_WIDTH]), bf(w_out[l][ATT_WIDTH:])

    xs = _pad_rows(jnp.concatenate([x_sample.reshape(ns, d), meta_tokens.astype(x_prompt.dtype)], axis=0), ROW_TILE)
    xp = x_prompt.reshape(batch * seq, d)
    pos_small = jnp.concatenate([jnp.tile(past + jnp.arange(dseq, dtype=I32), nb),
                                 jnp.arange(N_META, dtype=I32),
                                 jnp.zeros((ROW_TILE - n_small,), I32)])
    cos_s, sin_s = _rope_tables(pos_small)
    cos_p, sin_p = _rope_tables(N_META + jnp.arange(seq, dtype=I32))

    x1p = _ffn_ln(xp, *f1, vec(ln1_g), vec(ln1_b))
    x1s = _ffn_ln(xs, *f1, vec(ln1_g), vec(ln1_b))
    (kb_p, kib_p, rq_p, rk_p, rv_p, g_p, qt_p, ktf_p, vtf_p, vt_p, qit_p, kiwt_p) = _proj(
        x1p, wrow, wt, cos_p, sin_p, batch, nq)
    (kb_s, kib_s, rq_s, rk_s, rv_s, g_s, qt_s, ktf_s, vtf_s, vt_s, qit_s, kiwt_s) = _proj(
        x1s, wrow, wt, cos_s, sin_s, 1, 1)
    ktf_s, vtf_s, kiwt_s = ktf_s[0], vtf_s[0], kiwt_s[0]

    topk_p = min(TOPK_MAX, seq // 4)
    def key_chunks(meta_chunk, real):
        m = jnp.broadcast_to(meta_chunk[None, None], (batch, 1) + meta_chunk.shape)
        z = jnp.zeros((batch, SPARE_CHUNKS) + meta_chunk.shape, meta_chunk.dtype)
        return jnp.concatenate([m, real.reshape((batch, nq) + meta_chunk.shape), z], axis=1)

    k4 = key_chunks(_pad_rows(kb_s[meta], K_TILE), kb_p)
    ki4 = key_chunks(_pad_rows(kib_s[meta], K_TILE), kib_p)
    vt4 = key_chunks(_pad_last(vt_s[0][:, :, meta], K_TILE), vt_p)
    att_p = _prompt_att(qt_p, qit_p, kiwt_p[:, IDX_DIM:IDX_DIM + IDX_HEADS, :], k4, vt4, ki4, batch, nq, topk_p)
    att_p = att_p.reshape(batch * seq, ATT_WIDTH)

    pad_tok = lambda a, n: jnp.pad(a, ((0, 0), (0, RET_CHUNK - n), (0, 0)))
    m3 = lambda a: pad_tok(a[meta][None], N_META)
    zero_state = jnp.zeros((1, RET_HEADS, RET_DK, RET_DV), F32)
    _, s_meta = _retention(m3(rq_s), m3(rk_s), m3(rv_s), m3(g_s), zero_state, _ret_tables(N_META), True)
    b3 = lambda a: a.reshape(batch, seq, RET_WIDTH)
    yret_p, ret_prompt = _retention(b3(rq_p), b3(rk_p), b3(rv_p), b3(g_p), s_meta, _ret_tables(RET_CHUNK), True)
    s3 = lambda a: pad_tok(a[:ns].reshape(nb, dseq, RET_WIDTH), dseq)
    yret_s, ret_sample = _retention(s3(rq_s), s3(rk_s), s3(rv_s), s3(g_s), state_ret[l], _ret_tables(dseq), False)

    topk_s = min(TOPK_MAX, (past + dseq) // 4)
    pt_flat = page_table.reshape(-1).astype(I32)
    cache_k_t = jnp.transpose(cache_k[l], (0, 2, 3, 1)).reshape(1, n_pool, ATT_WIDTH, PAGE_SIZE)
    cache_v_t = jnp.transpose(cache_v[l], (0, 2, 3, 1)).reshape(1, n_pool, ATT_WIDTH, PAGE_SIZE)
    cache_idx_t = jnp.transpose(cache_idx_k[l], (0, 2, 1))[None]
    q_rows = jnp.concatenate([qt_s[0, h, (h % 2) * ATT_DIM:(h % 2 + 1) * ATT_DIM, :] for h in range(ATT_HEADS)],
                             axis=0).T[:ns]
    qi_rows = qit_s[0].reshape(IDX_DIM, IDX_HEADS, ROW_TILE)[:, :, :ns]
    qi_blk = jnp.pad(qi_rows.transpose(2, 1, 0).reshape(nb, dseq, IDX_HEADS, IDX_DIM).transpose(0, 2, 1, 3),
                     ((0, 0), (0, 0), (0, 8 - dseq), (0, 0))).reshape(nb, IDX_HEADS * 8, IDX_DIM)
    wi_s = kiwt_s[IDX_DIM:IDX_DIM + IDX_HEADS, :ns].reshape(IDX_HEADS, nb, dseq)
    wb = jnp.pad(wi_s.transpose(1, 0, 2), ((0, 0), (0, 0), (0, 8 - dseq))).reshape(nb, IDX_HEADS * 8, 1)
    wb = jnp.broadcast_to(wb, (nb, IDX_HEADS * 8, LANES)).astype(F32)
    new_t = lambda a_t: _pad_last(a_t[:, :ns].reshape(a_t.shape[0], nb, dseq).transpose(1, 0, 2), LANES)
    bias_s = _sample_idx(pt_flat, qi_blk, wb, bf(new_t(kiwt_s[:IDX_DIM])), cache_idx_t, n_pages, topk_s, dseq)
    head_mask = (jnp.arange(ATT_WIDTH)[None, :] // ATT_DIM == jnp.arange(ATT_HEADS)[:, None])
    q3 = jnp.pad(q_rows.reshape(nb, dseq, ATT_WIDTH), ((0, 0), (0, 8 - dseq), (0, 0)))
    qb = jnp.where(head_mask[None, :, None, :], q3[:, None, :, :], 0).astype(BF16).reshape(nb, ATT_HEADS * 8, ATT_WIDTH)
    att_s64 = _sample_att(pt_flat, qb, bias_s, bf(new_t(ktf_s)), bf(new_t(vtf_s)), cache_k_t, cache_v_t, n_pages)
    a5 = att_s64.reshape(nb, ATT_HEADS, 8, ATT_HEADS, ATT_DIM)
    att_s = jnp.stack([a5[:, h, :dseq, h, :] for h in range(ATT_HEADS)], axis=2).reshape(ns, ATT_WIDTH)

    lnw = (vec(ln2_g), vec(ln2_b), *f2, vec(ln3_g), vec(ln3_b))
    y_p = _out_ffn(x1p, att_p, yret_p.reshape(batch * seq, RET_WIDTH), woa, wor, *lnw)
    att_small = _pad_rows(bf(att_s), ROW_TILE)
    yret_small = _pad_rows(yret_s[:, :dseq].reshape(ns, RET_WIDTH), ROW_TILE)
    y_s = _out_ffn(x1s, att_small, yret_small, woa, wor, *lnw)

    def with_meta(real_t, small_t, feat_shape):
        m = jnp.broadcast_to(small_t[None, :, meta], (batch, small_t.shape[0], N_META))
        full = jnp.concatenate([m, real_t], axis=2)
        full = full.reshape((batch,) + feat_shape + (seq + N_META,))
        return jnp.moveaxis(full, -1, 1)[None]

    k_prompt = with_meta(ktf_p, ktf_s, (ATT_HEADS, ATT_DIM))
    v_prompt = with_meta(vtf_p, vtf_s, (ATT_HEADS, ATT_DIM))
    idxk_prompt = with_meta(kiwt_p[:, :IDX_DIM], kiwt_s[:IDX_DIM], (IDX_DIM,))
    y_prompt = y_p.reshape(batch, seq, d)
    y_sample = y_s[:ns].reshape(nb, dseq, d)
    k_sample = ktf_s[:, :ns].T.reshape(1, nb, dseq, ATT_HEADS, ATT_DIM)
    v_sample = vtf_s[:, :ns].T.reshape(1, nb, dseq, ATT_HEADS, ATT_DIM)
    idxk_sample = kiwt_s[:IDX_DIM, :ns].T.reshape(1, nb, dseq, IDX_DIM)
    return (y_prompt, y_sample, k_prompt, v_prompt, idxk_prompt, ret_prompt[None],
            k_sample, v_sample, idxk_sample, ret_sample[None])
```

```python
import functools

import jax
import jax.numpy as jnp
import numpy as np
from jax import lax
from jax.experimental import pallas as pl
from jax.experimental.pallas import tpu as pltpu

F32 = jnp.float32
BF16 = jnp.bfloat16
I32 = jnp.int32

D_MODEL = 1024
N_META = 16
ATT_DIM = 64
ATT_HEADS = 8
ATT_WIDTH = ATT_HEADS * ATT_DIM
IDX_HEADS = 8
IDX_DIM = 64
TOPK_MAX = 256
RET_HEADS = 4
RET_DK = 128
RET_DV = 128
RET_WIDTH = RET_HEADS * RET_DV
PAGE_SIZE = 128
ROPE_BASE = 10000.0
LN_EPS = 1e-5
GN_EPS = 1e-5
DEPTH = 1
ALPHA = (2.0 * DEPTH) ** 0.25

LANES = 128
ROW_TILE = 256
FFN_TILE = 512
Q_TILE = 256
K_TILE = 256
RET_CHUNK = 128
RET_SEQS_PER_STEP = 4
IDX_PAGES_PER_STEP = 32
ATT_PAGES_PER_STEP = 32
LOG2E = 1.4426950408889634
NEG_BIAS = -1e30
INT_MIN = -2147483648
VMEM_LIMIT = 56 * 1024 * 1024


def _dot(a, b):
    return jnp.dot(a, b, preferred_element_type=F32)


def _dot_nt(a, b):
    return lax.dot_general(a, b, (((1,), (1,)), ((), ())), preferred_element_type=F32)


def _layernorm(y, g, b):
    mu = jnp.mean(y, axis=-1, keepdims=True)
    d = y - mu
    var = jnp.mean(d * d, axis=-1, keepdims=True)
    return d * lax.rsqrt(var + LN_EPS) * g + b


def _swiglu(xb, wg_ref, wu_ref, wd_ref):
    hg = _dot(xb, wg_ref[...])
    hu = _dot(xb, wu_ref[...])
    act = (hg * jax.nn.sigmoid(hg)) * hu
    return _dot(act.astype(BF16), wd_ref[...])


def _ffn_ln_kernel(x_ref, wg_ref, wu_ref, wd_ref, g_ref, b_ref, o_ref):
    x = x_ref[...]
    y = ALPHA * x + 0.5 * _swiglu(x.astype(BF16), wg_ref, wu_ref, wd_ref)
    o_ref[...] = _layernorm(y, g_ref[...], b_ref[...])


def _const_spec(shape):
    return pl.BlockSpec(shape, lambda *_: (0,) * len(shape))


def _weight_spec(shape):
    return pl.BlockSpec(shape, lambda *_: (0,) * len(shape), pipeline_mode=pl.Buffered(1))


def _ffn_ln(x, wg, wu, wd, g, b):
    rows, d = x.shape
    dff = wg.shape[1]
    tm = min(FFN_TILE, rows)
    return pl.pallas_call(
        _ffn_ln_kernel,
        grid=(rows // tm,),
        in_specs=[pl.BlockSpec((tm, d), lambda i: (i, 0)),
                  _weight_spec((d, dff)), _weight_spec((d, dff)), _weight_spec((dff, d)),
                  _const_spec((1, d)), _const_spec((1, d))],
        out_specs=pl.BlockSpec((tm, d), lambda i: (i, 0)),
        out_shape=jax.ShapeDtypeStruct((rows, d), F32),
        compiler_params=pltpu.CompilerParams(
            dimension_semantics=("arbitrary",), vmem_limit_bytes=VMEM_LIMIT),
        name="ffn_ln",
    )(x, wg, wu, wd, g, b)


def _out_ffn_kernel(x1_ref, att_ref, yret_ref, woa_ref, wor_ref, g2_ref, b2_ref,
                    wg_ref, wu_ref, wd_ref, g3_ref, b3_ref, o_ref):
    m = _dot(att_ref[...], woa_ref[...]) + _dot(yret_ref[...], wor_ref[...])
    x2 = _layernorm(ALPHA * x1_ref[...] + m, g2_ref[...], b2_ref[...])
    y = ALPHA * x2 + 0.5 * _swiglu(x2.astype(BF16), wg_ref, wu_ref, wd_ref)
    o_ref[...] = _layernorm(y, g3_ref[...], b3_ref[...])


def _out_ffn(x1, att, yret, woa, wor, g2, b2, wg, wu, wd, g3, b3):
    rows, d = x1.shape
    dff = wg.shape[1]
    tm = min(FFN_TILE, rows)
    row = lambda w: pl.BlockSpec((tm, w), lambda i: (i, 0))
    return pl.pallas_call(
        _out_ffn_kernel,
        grid=(rows // tm,),
        in_specs=[row(d), row(ATT_WIDTH), row(RET_WIDTH),
                  _weight_spec((ATT_WIDTH, d)), _weight_spec((RET_WIDTH, d)),
                  _const_spec((1, d)), _const_spec((1, d)),
                  _weight_spec((d, dff)), _weight_spec((d, dff)), _weight_spec((dff, d)),
                  _const_spec((1, d)), _const_spec((1, d))],
        out_specs=row(d),
        out_shape=jax.ShapeDtypeStruct((rows, d), F32),
        compiler_params=pltpu.CompilerParams(
            dimension_semantics=("arbitrary",), vmem_limit_bytes=VMEM_LIMIT),
        name="out_ffn",
    )(x1, att, yret, woa, wor, g2, b2, wg, wu, wd, g3, b3)


W_ROW_COLS = ATT_WIDTH + LANES + 4 * RET_WIDTH
W_T_ROWS = 4 * ATT_WIDTH + LANES
V_ROWS = ATT_DIM + 16


def _rope(x, cos2, sin2):
    return x * cos2 + pltpu.roll(x, RET_DK // 2, 1) * sin2


def _proj_kernel(x1_ref, wrow_ref, wt_ref, cos_ref, sin_ref,
                 kb_ref, kib_ref, rq_ref, rk_ref, rv_ref, g_ref,
                 qt_ref, ktf_ref, vtf_ref, vt_ref, qit_ref, kiwt_ref):
    xb = x1_ref[...].astype(BF16)
    z = _dot(xb, wrow_ref[...])
    kb_ref[...] = z[:, 0:ATT_WIDTH].astype(BF16)
    kib_ref[...] = z[:, ATT_WIDTH:ATT_WIDTH + IDX_DIM].astype(BF16)
    o = ATT_WIDTH + LANES
    cos2 = cos_ref[...]
    sin2 = sin_ref[...]
    for h in range(RET_HEADS):
        sl = slice(h * RET_DK, (h + 1) * RET_DK)
        rq = z[:, o + h * RET_DK:o + (h + 1) * RET_DK]
        rk = z[:, o + RET_WIDTH + h * RET_DK:o + RET_WIDTH + (h + 1) * RET_DK]
        rq_ref[:, sl] = _rope(rq, cos2, sin2).astype(BF16)
        rk_ref[:, sl] = (_rope(rk, cos2, sin2) * (RET_DK ** -0.5)).astype(BF16)
    rv_ref[...] = z[:, o + 2 * RET_WIDTH:o + 3 * RET_WIDTH].astype(BF16)
    g_ref[...] = z[:, o + 3 * RET_WIDTH:o + 4 * RET_WIDTH]

    zt = _dot_nt(wt_ref[...], xb)
    tm = xb.shape[0]
    zero = jnp.zeros((ATT_DIM, tm), BF16)
    for h in range(ATT_HEADS):
        qh = (zt[h * ATT_DIM:(h + 1) * ATT_DIM, :] * (ATT_DIM ** -0.5 * LOG2E)).astype(BF16)
        qt_ref[0, h] = jnp.concatenate([qh, zero] if h % 2 == 0 else [zero, qh], axis=0)
    ktf_ref[0] = zt[ATT_WIDTH:2 * ATT_WIDTH, :]
    vt = zt[2 * ATT_WIDTH:3 * ATT_WIDTH, :]
    vtf_ref[0] = vt
    ones_blk = jnp.where(lax.broadcasted_iota(I32, (V_ROWS - ATT_DIM, tm), 0) == 0, 1.0, 0.0).astype(BF16)
    for h in range(ATT_HEADS):
        vt_ref[0, h] = jnp.concatenate([vt[h * ATT_DIM:(h + 1) * ATT_DIM, :].astype(BF16), ones_blk], axis=0)
    for h in range(IDX_HEADS):
        qih = zt[3 * ATT_WIDTH + h * IDX_DIM:3 * ATT_WIDTH + (h + 1) * IDX_DIM, :]
        qit_ref[0, :, h * tm:(h + 1) * tm] = (qih * (IDX_DIM ** -0.5)).astype(BF16)
    kiwt_ref[0] = zt[4 * ATT_WIDTH:4 * ATT_WIDTH + LANES, :]


def _proj(x1, wrow, wt, cos2, sin2, batch, tiles_per_seq):
    rows, d = x1.shape
    tm = ROW_TILE
    nt = rows // tm
    seq = tiles_per_seq * tm
    row = lambda w: pl.BlockSpec((tm, w), lambda i: (i, 0))
    tab = pl.BlockSpec((tm, LANES), lambda i: (i % tiles_per_seq, 0))
    tcol = lambda r: pl.BlockSpec((1, r, tm), lambda i: (i // tiles_per_seq, 0, i % tiles_per_seq))
    out_shape = (
        jax.ShapeDtypeStruct((rows, ATT_WIDTH), BF16),
        jax.ShapeDtypeStruct((rows, IDX_DIM), BF16),
        jax.ShapeDtypeStruct((rows, RET_WIDTH), BF16),
        jax.ShapeDtypeStruct((rows, RET_WIDTH), BF16),
        jax.ShapeDtypeStruct((rows, RET_WIDTH), BF16),
        jax.ShapeDtypeStruct((rows, RET_WIDTH), F32),
        jax.ShapeDtypeStruct((nt, ATT_HEADS, 2 * ATT_DIM, tm), BF16),
        jax.ShapeDtypeStruct((batch, ATT_WIDTH, seq), F32),
        jax.ShapeDtypeStruct((batch, ATT_WIDTH, seq), F32),
        jax.ShapeDtypeStruct((nt, ATT_HEADS, V_ROWS, tm), BF16),
        jax.ShapeDtypeStruct((nt, IDX_DIM, IDX_HEADS * tm), BF16),
        jax.ShapeDtypeStruct((batch, LANES, seq), F32),
    )
    out_specs = (
        row(ATT_WIDTH), row(IDX_DIM), row(RET_WIDTH), row(RET_WIDTH), row(RET_WIDTH), row(RET_WIDTH),
        pl.BlockSpec((1, ATT_HEADS, 2 * ATT_DIM, tm), lambda i: (i, 0, 0, 0)),
        tcol(ATT_WIDTH), tcol(ATT_WIDTH),
        pl.BlockSpec((1, ATT_HEADS, V_ROWS, tm), lambda i: (i, 0, 0, 0)),
        pl.BlockSpec((1, IDX_DIM, IDX_HEADS * tm), lambda i: (i, 0, 0)),
        tcol(LANES),
    )
    return pl.pallas_call(
        _proj_kernel,
        grid=(nt,),
        in_specs=[row(d), _const_spec((d, W_ROW_COLS)), _const_spec((W_T_ROWS, d)), tab, tab],
        out_specs=out_specs,
        out_shape=out_shape,
        compiler_params=pltpu.CompilerParams(
            dimension_semantics=("arbitrary",), vmem_limit_bytes=VMEM_LIMIT),
        name="proj",
    )(x1, wrow, wt, cos2, sin2)


def _pattern_to_float(u):
    key = u ^ INT_MIN
    bits = jnp.where(key >= 0, key, key ^ 0x7FFFFFFF)
    return lax.bitcast_convert_type(bits, F32)


def _kth_largest(count, topk, shape, tile_shape):
    def step(i, st):
        t_u, cnt_t = st
        cand = t_u | (1 << (31 - i))
        cand_f = jnp.broadcast_to(_pattern_to_float(cand), tile_shape)
        cnt = count(lambda s, base: jnp.where(s >= cand_f, 1, 0))
        take = cnt >= topk
        return jnp.where(take, cand, t_u), jnp.where(take, cnt, cnt_t)

    return lax.fori_loop(0, 32, step, (jnp.zeros(shape, I32), jnp.zeros(shape, I32)))


def _tie_break(count, thr, need, nbits, slot_axis, shape):
    def search(_):
        def step(i, p):
            cand = p | (1 << (nbits - 1 - i))
            cnt = count(lambda s, base: jnp.where(
                s == thr, jnp.where((base + lax.broadcasted_iota(I32, s.shape, slot_axis)) < cand, 1, 0), 0))
            return jnp.where(cnt < need, cand, p)
        return lax.fori_loop(0, nbits, step, jnp.zeros(shape, I32))

    return search


SPARE_CHUNKS = 2
COUNT_LANES = 4


def _prompt_att_kernel(qt_ref, qit_ref, wit_ref, k_ref, vt_ref, ki_ref, km_ref, vtm_ref, kim_ref, o_ref,
                       sc_ref, ot_ref, m_ref, s0_ref, s1_ref, c0_ref, c1_ref, *, topk, nq):
    j = pl.program_id(1)
    tq, tk = Q_TILE, K_TILE
    n_real = j + 1
    n_pairs = lax.shift_right_logical(n_real + 1, 1)
    col = lax.broadcasted_iota(I32, (1, tq), 1)
    rows_k = lax.broadcasted_iota(I32, (tk, tq), 0)
    rows_m = lax.broadcasted_iota(I32, (N_META, tq), 0)

    def slot_base(c):
        return pl.multiple_of(N_META + c * tk, N_META)

    def rows_of(c, n=1):
        return pl.ds(slot_base(c), n * tk)

    def stored(c):
        return jnp.minimum(c, nq - 1)

    def idx_scores(ki_chunk):
        acc = None
        for h in range(IDX_HEADS):
            s = _dot(ki_chunk, qit_ref[0, :, h * tq:(h + 1) * tq])
            t = jnp.maximum(s, 0.0) * wit_ref[0, h:h + 1, :]
            acc = t if acc is None else acc + t
        return acc

    sc_ref[0:N_META, :] = idx_scores(kim_ref[...])

    def p1(i, carry):
        for c in (2 * i, 2 * i + 1):
            last_row = j * tq + col - c * tk
            sc_ref[rows_of(c), :] = jnp.where(rows_k <= last_row, idx_scores(ki_ref[0, stored(c)]), -jnp.inf)
        return carry

    lax.fori_loop(0, n_pairs, p1, 0)
    sc_ref[rows_of(n_real, SPARE_CHUNKS), :] = jnp.full((SPARE_CHUNKS * tk, tq), -jnp.inf, F32)

    def count(ind_fn):
        def body(c2, part):
            base = slot_base(2 * c2)
            x = ind_fn(sc_ref[pl.ds(base, 2 * tk), :], base)
            return part + jnp.sum(x.reshape(-1, COUNT_LANES, 8, tq), axis=0)

        part = lax.fori_loop(0, n_pairs, body, jnp.zeros((COUNT_LANES, 8, tq), I32))
        meta = ind_fn(sc_ref[0:N_META, :], 0)
        return (jnp.sum(part.reshape(COUNT_LANES * 8, tq), axis=0, keepdims=True)
                + jnp.sum(meta, axis=0, keepdims=True))

    t_u, cnt_ge = _kth_largest(count, topk, (1, tq), (1, tq))
    short = t_u == 0
    thr = jnp.where(short, -jnp.inf, _pattern_to_float(t_u))
    cnt_gt = count(lambda s, base: jnp.where(s > thr, 1, 0))
    cnt_eq = cnt_ge - cnt_gt
    need = topk - cnt_gt
    nbits = int(N_META + tk * (nq + SPARE_CHUNKS) - 1).bit_length()
    has_ties = jnp.max(jnp.where(short, 0, cnt_eq - need)) > 0
    last = lax.cond(has_ties, _tie_break(count, thr, need, nbits, 0, (1, tq)),
                    lambda _: jnp.full((1, tq), (1 << nbits) - 1, I32), 0)
    last = jnp.where(short, -1, last)

    def bias_of(s, slot):
        keep_eq = jnp.where(slot <= last, 0.0, NEG_BIAS)
        return jnp.where(s > thr, 0.0, jnp.where(s == thr, keep_eq, NEG_BIAS))

    sc_ref[0:N_META, :] = bias_of(sc_ref[0:N_META, :], rows_m)

    def p3(c, carry):
        sc_ref[rows_of(c), :] = bias_of(sc_ref[rows_of(c), :], slot_base(c) + rows_k)
        return carry

    lax.fori_loop(0, n_real + SPARE_CHUNKS, p3, 0)

    def k_lanes(h):
        return slice((h // 2) * LANES, (h // 2 + 1) * LANES)

    rep = lambda x: jnp.broadcast_to(x, (8, tq))

    def logits_stage(c, h, s_ref, cmax_ref):
        s = _dot(k_ref[0, stored(c), :, k_lanes(h)], qt_ref[0, h]) + sc_ref[rows_of(c), :]
        s_ref[h] = s
        cmax_ref[h] = rep(jnp.max(s, axis=0, keepdims=True))

    def softmax_stage(c, h, s_ref, cmax_ref):
        m_old = m_ref[h]
        m_new = jnp.maximum(m_old, cmax_ref[h])
        a = jnp.exp2(m_old - m_new)
        p = jnp.exp2(s_ref[h] - m_new[0:1, :])
        m_ref[h] = m_new
        ot_ref[h] = a[0:1, :] * ot_ref[h] + _dot(vt_ref[0, stored(c), h], p.astype(BF16))

    pad = jnp.zeros((LANES - N_META, tq), BF16)
    for h in range(ATT_HEADS):
        s = _dot(km_ref[:, k_lanes(h)], qt_ref[0, h]) + sc_ref[0:N_META, :]
        m = jnp.max(s, axis=0, keepdims=True)
        p = jnp.exp2(s - m).astype(BF16)
        m_ref[h] = rep(m)
        ot_ref[h] = _dot(vtm_ref[h], jnp.concatenate([p, pad], axis=0))
    for h in range(ATT_HEADS):
        logits_stage(0, h, s0_ref, c0_ref)

    def p4(i, carry):
        for h in range(ATT_HEADS):
            logits_stage(2 * i + 1, h, s1_ref, c1_ref)
            softmax_stage(2 * i, h, s0_ref, c0_ref)
        for h in range(ATT_HEADS):
            logits_stage(2 * i + 2, h, s0_ref, c0_ref)
            softmax_stage(2 * i + 1, h, s1_ref, c1_ref)
        return carry

    lax.fori_loop(0, n_pairs, p4, 0)
    out_t = jnp.concatenate([ot_ref[h, 0:ATT_DIM, :] / ot_ref[h, ATT_DIM:ATT_DIM + 1, :]
                             for h in range(ATT_HEADS)], axis=0)
    o_ref[0] = out_t.T.astype(BF16)


def _prompt_att(qt, qit, wit, k4, vt4, ki4, km, vtm, kim, batch, nq, topk):
    tq = Q_TILE
    per_batch = lambda shape: pl.BlockSpec((1,) + shape, lambda b, j: (b,) + (0,) * len(shape))
    hs = (ATT_HEADS, K_TILE, tq)
    return pl.pallas_call(
        functools.partial(_prompt_att_kernel, topk=topk, nq=nq),
        grid=(batch, nq),
        in_specs=[
            pl.BlockSpec((1, ATT_HEADS, 2 * ATT_DIM, tq), lambda b, j: (b * nq + j, 0, 0, 0)),
            pl.BlockSpec((1, IDX_DIM, IDX_HEADS * tq), lambda b, j: (b * nq + j, 0, 0)),
            pl.BlockSpec((1, IDX_HEADS, tq), lambda b, j: (b, 0, j)),
            per_batch((nq, K_TILE, ATT_WIDTH)),
            per_batch((nq, ATT_HEADS, V_ROWS, K_TILE)),
            per_batch((nq, K_TILE, IDX_DIM)),
            _const_spec((N_META, ATT_WIDTH)), _const_spec((ATT_HEADS, V_ROWS, LANES)),
            _const_spec((N_META, IDX_DIM)),
        ],
        out_specs=pl.BlockSpec((1, tq, ATT_WIDTH), lambda b, j: (b * nq + j, 0, 0)),
        out_shape=jax.ShapeDtypeStruct((batch * nq, tq, ATT_WIDTH), BF16),
        scratch_shapes=[pltpu.VMEM((N_META + (nq + SPARE_CHUNKS) * K_TILE, tq), F32),
                        pltpu.VMEM((ATT_HEADS, V_ROWS, tq), F32),
                        pltpu.VMEM((ATT_HEADS, 8, tq), F32),
                        pltpu.VMEM(hs, F32), pltpu.VMEM(hs, F32),
                        pltpu.VMEM((ATT_HEADS, 8, tq), F32), pltpu.VMEM((ATT_HEADS, 8, tq), F32)],
        compiler_params=pltpu.CompilerParams(
            dimension_semantics=("arbitrary", "arbitrary"), vmem_limit_bytes=VMEM_LIMIT),
        name="prompt_att",
    )(qt, qit, wit, k4, vt4, ki4, km, vtm, kim)


def _ret_kernel(rq_ref, rk_ref, rv_ref, g_ref, s0_ref, decay_ref, xi_ref, zeta_ref, gl_ref,
                y_ref, sfin_ref, state_ref, *, shared_s0):
    t = pl.program_id(1)
    nseq = state_ref.shape[0]

    @pl.when(t == 0)
    def _():
        for r in range(nseq):
            state_ref[r] = s0_ref[0 if shared_s0 else r]

    for r in range(nseq):
        for h in range(RET_HEADS):
            sl = slice(h * RET_DK, (h + 1) * RET_DK)
            q = rq_ref[r, :, sl]
            k = rk_ref[r, :, sl]
            v = rv_ref[r, :, sl]
            sp = state_ref[r, h]
            inner = _dot_nt(q, k) * decay_ref[h]
            ret = _dot(inner.astype(BF16), v) + _dot(q, sp.astype(BF16)) * xi_ref[h]
            kz = (k.astype(F32) * zeta_ref[h]).T.astype(BF16)
            state_ref[r, h] = gl_ref[h] * sp + _dot(kz, v)
            mu = jnp.mean(ret, axis=-1, keepdims=True)
            d = ret - mu
            var = jnp.mean(d * d, axis=-1, keepdims=True)
            retn = d * lax.rsqrt(var + GN_EPS)
            g = g_ref[r, :, sl]
            y_ref[r, :, sl] = ((g * jax.nn.sigmoid(g)) * retn).astype(BF16)

    @pl.when(t == pl.num_programs(1) - 1)
    def _():
        sfin_ref[...] = state_ref[...]


def _retention(rq, rk, rv, g, s0, tables, shared_s0):
    batch, tokens, _ = rq.shape
    nchunk = tokens // RET_CHUNK
    nseq = RET_SEQS_PER_STEP if batch % RET_SEQS_PER_STEP == 0 else 1
    decay, xi, zeta, gl = tables
    tok = pl.BlockSpec((nseq, RET_CHUNK, RET_WIDTH), lambda b, t: (b, t, 0))
    hh = (RET_HEADS, RET_DK, RET_DV)
    s0_spec = (pl.BlockSpec((1,) + hh, lambda b, t: (0, 0, 0, 0)) if shared_s0 else
               pl.BlockSpec((nseq,) + hh, lambda b, t: (b, 0, 0, 0)))
    return pl.pallas_call(
        functools.partial(_ret_kernel, shared_s0=shared_s0),
        grid=(batch // nseq, nchunk),
        in_specs=[tok, tok, tok, tok, s0_spec,
                  _const_spec(hh), _const_spec(hh), _const_spec(hh), _const_spec((RET_HEADS, 1, RET_DV))],
        out_specs=(tok, pl.BlockSpec((nseq,) + hh, lambda b, t: (b, 0, 0, 0))),
        out_shape=(jax.ShapeDtypeStruct((batch, tokens, RET_WIDTH), BF16),
                   jax.ShapeDtypeStruct((batch,) + hh, F32)),
        scratch_shapes=[pltpu.VMEM((nseq,) + hh, F32)],
        compiler_params=pltpu.CompilerParams(dimension_semantics=("arbitrary", "arbitrary")),
        name="retention",
    )(rq, rk, rv, g, s0, decay, xi, zeta, gl)


def _ret_tables(length):
    lg = jnp.log(1.0 - 2.0 ** (-5.0 - jnp.arange(RET_HEADS, dtype=F32)))
    n = jnp.arange(RET_CHUNK, dtype=F32)
    live = n < length
    diff = n[:, None] - n[None, :]
    ok = (diff >= 0) & live[:, None] & live[None, :]
    decay = jnp.where(ok[None], jnp.exp(jnp.maximum(diff, 0.0)[None] * lg[:, None, None]), 0.0)
    xi = jnp.exp((n[None, :] + 1.0) * lg[:, None])
    zeta = jnp.where(live[None, :], jnp.exp((length - 1.0 - n)[None, :] * lg[:, None]), 0.0)
    bc = lambda a: jnp.broadcast_to(a[:, :, None], (RET_HEADS, RET_CHUNK, RET_DV)).astype(F32)
    gl = jnp.broadcast_to(jnp.exp(length * lg)[:, None, None], (RET_HEADS, 1, RET_DV)).astype(F32)
    return decay.astype(F32), bc(xi), bc(zeta), gl


def _sample_idx_kernel(pt_ref, qi_ref, wb_ref, kin_ref, *rest, n_steps, topk, dseq, pps):
    pages = rest[:pps]
    bias_ref = rest[pps]
    sc_ref = rest[pps + 1]
    pc = pl.program_id(1)
    n_chunks = n_steps * pps + 1
    qi = qi_ref[0]
    wb = wb_ref[0]

    def scores(ki_t):
        n = ki_t.shape[1]
        s = jnp.maximum(_dot(qi, ki_t), 0.0) * jnp.concatenate([wb] * (n // LANES), axis=1)
        return jnp.sum(s.reshape(IDX_HEADS, 8, n), axis=0)

    s_step = scores(jnp.concatenate([pages[p][0, 0].astype(BF16) for p in range(pps)], axis=1))
    for p in range(pps):
        sc_ref[pc * pps + p] = s_step[:, p * LANES:(p + 1) * LANES]

    @pl.when(pc == n_steps - 1)
    def _():
        row = lax.broadcasted_iota(I32, (8, LANES), 0)
        lane = lax.broadcasted_iota(I32, (8, LANES), 1)
        vis = (lane <= row) & (lane < dseq)
        sc_ref[n_chunks - 1] = jnp.where(vis, scores(kin_ref[0]), -jnp.inf)
        slot0 = lax.broadcasted_iota(I32, (n_chunks, 8, LANES), 0) * LANES

        def count(ind_fn):
            x = ind_fn(sc_ref[...], slot0).astype(F32)
            acc = [x[i] for i in range(COUNT_LANES)]
            for c in range(COUNT_LANES, n_chunks):
                acc[c % COUNT_LANES] = acc[c % COUNT_LANES] + x[c]
            per_lane = (acc[0] + acc[1]) + (acc[2] + acc[3])
            return jnp.sum(per_lane, axis=1, keepdims=True).astype(I32)

        real_query = lax.broadcasted_iota(I32, (8, 1), 0) < dseq
        t_u, cnt_ge = _kth_largest(count, topk, (8, 1), (8, LANES))
        short = t_u == 0
        thr = jnp.where(short, -jnp.inf, _pattern_to_float(t_u))
        cnt_gt = count(lambda s, base: jnp.where(s > thr, 1, 0))
        cnt_eq = cnt_ge - cnt_gt
        need = topk - cnt_gt
        nbits = int(n_chunks * LANES - 1).bit_length()
        has_ties = jnp.max(jnp.where(real_query, jnp.where(short, 0, cnt_eq - need), 0)) > 0
        last = lax.cond(has_ties, _tie_break(count, thr, need, nbits, 2, (8, 1)),
                        lambda _: jnp.full((8, 1), (1 << nbits) - 1, I32), 0)
        last = jnp.where(short, -1, last)
        s = sc_ref[...]
        slot = slot0 + lax.broadcasted_iota(I32, s.shape, 2)
        keep_eq = jnp.where(slot <= last, 0.0, NEG_BIAS)
        bias_ref[0] = jnp.where(s > thr, 0.0, jnp.where(s == thr, keep_eq, NEG_BIAS))


def _page_specs(pps, rows, page_table_cols):
    def spec(p):
        return pl.BlockSpec((1, 1, rows, PAGE_SIZE),
                            lambda b, pc, pt: (0, pt[b * page_table_cols + pc * pps + p], 0, 0))
    return [spec(p) for p in range(pps)]


def _sample_idx(pt_flat, qi_blk, wb, ki_new_t, cache_idx_t, n_pages, topk, dseq):
    nb = qi_blk.shape[0]
    pps = IDX_PAGES_PER_STEP
    n_steps = n_pages // pps
    n_chunks = n_pages + 1
    grid_spec = pltpu.PrefetchScalarGridSpec(
        num_scalar_prefetch=1,
        grid=(nb, n_steps),
        in_specs=[pl.BlockSpec((1, 64, IDX_DIM), lambda b, pc, pt: (b, 0, 0)),
                  pl.BlockSpec((1, 64, LANES), lambda b, pc, pt: (b, 0, 0)),
                  pl.BlockSpec((1, IDX_DIM, LANES), lambda b, pc, pt: (b, 0, 0)),
                  *_page_specs(pps, IDX_DIM, n_pages)],
        out_specs=pl.BlockSpec((1, n_chunks, 8, LANES), lambda b, pc, pt: (b, 0, 0, 0)),
        scratch_shapes=[pltpu.VMEM((n_chunks, 8, LANES), F32)],
    )
    return pl.pallas_call(
        functools.partial(_sample_idx_kernel, n_steps=n_steps, topk=topk, dseq=dseq, pps=pps),
        grid_spec=grid_spec,
        out_shape=jax.ShapeDtypeStruct((nb, n_chunks, 8, LANES), F32),
        compiler_params=pltpu.CompilerParams(dimension_semantics=("arbitrary", "arbitrary")),
        name="sample_idx",
    )(pt_flat, qi_blk, wb, ki_new_t, *([cache_idx_t] * pps))


def _sample_att_kernel(pt_ref, qb_ref, bias_ref, kn_ref, vn_ref, *rest, n_steps, pps):
    kpages = rest[:pps]
    vpages = rest[pps:2 * pps]
    o_ref, m_ref, l_ref, acc_ref, kbuf, vbuf = rest[2 * pps:]
    pc = pl.program_id(1)
    qb = qb_ref[0]
    n_chunks = n_steps * pps + 1

    def rows64(b8):
        return jnp.concatenate([b8] * ATT_HEADS, axis=0)

    @pl.when(pc == 0)
    def _():
        s = _dot(qb, kn_ref[0]) + rows64(bias_ref[0, n_chunks - 1])
        m = jnp.max(s, axis=1, keepdims=True)
        p = jnp.exp2(s - m)
        m_ref[...] = m
        l_ref[...] = jnp.sum(p, axis=1, keepdims=True)
        acc_ref[...] = _dot_nt(p.astype(BF16), vn_ref[0])

    for p in range(pps):
        kbuf[:, p * PAGE_SIZE:(p + 1) * PAGE_SIZE] = kpages[p][0, 0].astype(BF16)
        vbuf[:, p * PAGE_SIZE:(p + 1) * PAGE_SIZE] = vpages[p][0, 0].astype(BF16)
    bias = jnp.concatenate([bias_ref[0, pc * pps + p] for p in range(pps)], axis=1)
    s = _dot(qb, kbuf[...]) + rows64(bias)
    m_old = m_ref[...]
    m_new = jnp.maximum(m_old, jnp.max(s, axis=1, keepdims=True))
    a = jnp.exp2(m_old - m_new)
    p = jnp.exp2(s - m_new)
    m_ref[...] = m_new
    l_ref[...] = a * l_ref[...] + jnp.sum(p, axis=1, keepdims=True)
    acc_ref[...] = a * acc_ref[...] + _dot_nt(p.astype(BF16), vbuf[...])

    @pl.when(pc == n_steps - 1)
    def _():
        o_ref[0] = acc_ref[...] / l_ref[...]


def _sample_att(pt_flat, qb, bias, k_new_t, v_new_t, cache_k_t, cache_v_t, n_pages):
    nb = qb.shape[0]
    pps = ATT_PAGES_PER_STEP
    n_steps = n_pages // pps
    n_chunks = n_pages + 1
    rows = ATT_HEADS * 8
    grid_spec = pltpu.PrefetchScalarGridSpec(
        num_scalar_prefetch=1,
        grid=(nb, n_steps),
        in_specs=[pl.BlockSpec((1, rows, ATT_WIDTH), lambda b, pc, pt: (b, 0, 0)),
                  pl.BlockSpec((1, n_chunks, 8, LANES), lambda b, pc, pt: (b, 0, 0, 0)),
                  pl.BlockSpec((1, ATT_WIDTH, LANES), lambda b, pc, pt: (b, 0, 0)),
                  pl.BlockSpec((1, ATT_WIDTH, LANES), lambda b, pc, pt: (b, 0, 0)),
                  *_page_specs(pps, ATT_WIDTH, n_pages),
                  *_page_specs(pps, ATT_WIDTH, n_pages)],
        out_specs=pl.BlockSpec((1, rows, ATT_WIDTH), lambda b, pc, pt: (b, 0, 0)),
        scratch_shapes=[pltpu.VMEM((rows, 1), F32), pltpu.VMEM((rows, 1), F32),
                        pltpu.VMEM((rows, ATT_WIDTH), F32),
                        pltpu.VMEM((ATT_WIDTH, pps * PAGE_SIZE), BF16),
                        pltpu.VMEM((ATT_WIDTH, pps * PAGE_SIZE), BF16)],
    )
    return pl.pallas_call(
        functools.partial(_sample_att_kernel, n_steps=n_steps, pps=pps),
        grid_spec=grid_spec,
        out_shape=jax.ShapeDtypeStruct((nb, rows, ATT_WIDTH), F32),
        compiler_params=pltpu.CompilerParams(
            dimension_semantics=("arbitrary", "arbitrary"), vmem_limit_bytes=VMEM_LIMIT),
        name="sample_att",
    )(pt_flat, qb, bias, k_new_t, v_new_t, *([cache_k_t] * pps), *([cache_v_t] * pps))


def _rope_tables(pos):
    half = RET_DK // 2
    inv = ROPE_BASE ** (-jnp.arange(half, dtype=F32) / half)
    ang = pos.astype(F32)[:, None] * inv[None, :]
    cos, sin = jnp.cos(ang), jnp.sin(ang)
    return jnp.concatenate([cos, cos], axis=1), jnp.concatenate([-sin, sin], axis=1)


def _pad_rows(a, rows):
    return jnp.pad(a, ((0, rows - a.shape[0]),) + ((0, 0),) * (a.ndim - 1))


def _pad_last(a, n):
    return jnp.pad(a, ((0, 0),) * (a.ndim - 1) + ((0, n - a.shape[-1]),))


def kernel(x_prompt, x_sample, cache_k, cache_v, cache_idx_k, state_ret, page_table, meta_tokens,
           ffn1_w_gate, ffn1_w_up, ffn1_w_down, ln1_g, ln1_b, w_in, w_out, ln2_g, ln2_b,
           ffn2_w_gate, ffn2_w_up, ffn2_w_down, ln3_g, ln3_b):
    batch, seq, d = x_prompt.shape
    nb, dseq, _ = x_sample.shape
    n_pages = page_table.shape[1]
    n_pool = cache_k.shape[1]
    past = n_pages * PAGE_SIZE
    nq = seq // Q_TILE
    assert d == D_MODEL and seq % Q_TILE == 0 and Q_TILE == ROW_TILE == K_TILE
    assert dseq <= 8 and n_pages % IDX_PAGES_PER_STEP == 0 and n_pages % ATT_PAGES_PER_STEP == 0
    ns = nb * dseq
    n_small = ns + N_META
    assert n_small <= ROW_TILE
    meta = slice(ns, n_small)

    bf = lambda a: a.astype(BF16)
    l = 0
    f1 = (bf(ffn1_w_gate[l]), bf(ffn1_w_up[l]), bf(ffn1_w_down[l]))
    f2 = (bf(ffn2_w_gate[l]), bf(ffn2_w_up[l]), bf(ffn2_w_down[l]))
    vec = lambda a: a[l][None, :].astype(F32)
    w = w_in[l]
    offs = np.cumsum([0, 512, 512, 512, 512, 64, 8, 512, 512, 512, 512])
    wq, wk, wv, wqi, wki, wwi, wrq, wrk, wrv, wg = [w[:, offs[i]:offs[i + 1]] for i in range(10)]
    kiw = jnp.concatenate([wki, wwi, jnp.zeros((d, LANES - IDX_DIM - IDX_HEADS), w.dtype)], axis=1)
    wrow = bf(jnp.concatenate([wk, kiw, wrq, wrk, wrv, wg], axis=1))
    wt = bf(jnp.concatenate([wq, wk, wv, wqi, kiw], axis=1).T)
    woa, wor = bf(w_out[l][:ATT_WIDTH]), bf(w_out[l][ATT_WIDTH:])

    xs = _pad_rows(jnp.concatenate([x_sample.reshape(ns, d), meta_tokens.astype(x_prompt.dtype)], axis=0), ROW_TILE)
    xp = x_prompt.reshape(batch * seq, d)
    pos_small = jnp.concatenate([jnp.tile(past + jnp.arange(dseq, dtype=I32), nb),
                                 jnp.arange(N_META, dtype=I32),
                                 jnp.zeros((ROW_TILE - n_small,), I32)])
    cos_s, sin_s = _rope_tables(pos_small)
    cos_p, sin_p = _rope_tables(N_META + jnp.arange(seq, dtype=I32))

    x1p = _ffn_ln(xp, *f1, vec(ln1_g), vec(ln1_b))
    x1s = _ffn_ln(xs, *f1, vec(ln1_g), vec(ln1_b))
    (kb_p, kib_p, rq_p, rk_p, rv_p, g_p, qt_p, ktf_p, vtf_p, vt_p, qit_p, kiwt_p) = _proj(
        x1p, wrow, wt, cos_p, sin_p, batch, nq)
    (kb_s, kib_s, rq_s, rk_s, rv_s, g_s, qt_s, ktf_s, vtf_s, vt_s, qit_s, kiwt_s) = _proj(
        x1s, wrow, wt, cos_s, sin_s, 1, 1)
    ktf_s, vtf_s, kiwt_s = ktf_s[0], vtf_s[0], kiwt_s[0]

    topk_p = min(TOPK_MAX, seq // 4)
    att_p = _prompt_att(qt_p, qit_p, kiwt_p[:, IDX_DIM:IDX_DIM + IDX_HEADS, :],
                        kb_p.reshape(batch, nq, K_TILE, ATT_WIDTH),
                        vt_p.reshape(batch, nq, ATT_HEADS, V_ROWS, K_TILE),
                        kib_p.reshape(batch, nq, K_TILE, IDX_DIM),
                        kb_s[meta], _pad_last(vt_s[0][:, :, meta], LANES), kib_s[meta], batch, nq, topk_p)
    att_p = att_p.reshape(batch * seq, ATT_WIDTH)

    pad_tok = lambda a, n: jnp.pad(a, ((0, 0), (0, RET_CHUNK - n), (0, 0)))
    m3 = lambda a: pad_tok(a[meta][None], N_META)
    zero_state = jnp.zeros((1, RET_HEADS, RET_DK, RET_DV), F32)
    _, s_meta = _retention(m3(rq_s), m3(rk_s), m3(rv_s), m3(g_s), zero_state, _ret_tables(N_META), True)
    b3 = lambda a: a.reshape(batch, seq, RET_WIDTH)
    yret_p, ret_prompt = _retention(b3(rq_p), b3(rk_p), b3(rv_p), b3(g_p), s_meta, _ret_tables(RET_CHUNK), True)
    s3 = lambda a: pad_tok(a[:ns].reshape(nb, dseq, RET_WIDTH), dseq)
    yret_s, ret_sample = _retention(s3(rq_s), s3(rk_s), s3(rv_s), s3(g_s), state_ret[l], _ret_tables(dseq), False)

    topk_s = min(TOPK_MAX, (past + dseq) // 4)
    pt_flat = page_table.reshape(-1).astype(I32)
    cache_k_t = jnp.transpose(cache_k[l], (0, 2, 3, 1)).reshape(1, n_pool, ATT_WIDTH, PAGE_SIZE)
    cache_v_t = jnp.transpose(cache_v[l], (0, 2, 3, 1)).reshape(1, n_pool, ATT_WIDTH, PAGE_SIZE)
    cache_idx_t = jnp.transpose(cache_idx_k[l], (0, 2, 1))[None]
    q_rows = jnp.concatenate([qt_s[0, h, (h % 2) * ATT_DIM:(h % 2 + 1) * ATT_DIM, :] for h in range(ATT_HEADS)],
                             axis=0).T[:ns]
    qi_rows = qit_s[0].reshape(IDX_DIM, IDX_HEADS, ROW_TILE)[:, :, :ns]
    qi_blk = jnp.pad(qi_rows.transpose(2, 1, 0).reshape(nb, dseq, IDX_HEADS, IDX_DIM).transpose(0, 2, 1, 3),
                     ((0, 0), (0, 0), (0, 8 - dseq), (0, 0))).reshape(nb, IDX_HEADS * 8, IDX_DIM)
    wi_s = kiwt_s[IDX_DIM:IDX_DIM + IDX_HEADS, :ns].reshape(IDX_HEADS, nb, dseq)
    wb = jnp.pad(wi_s.transpose(1, 0, 2), ((0, 0), (0, 0), (0, 8 - dseq))).reshape(nb, IDX_HEADS * 8, 1)
    wb = jnp.broadcast_to(wb, (nb, IDX_HEADS * 8, LANES)).astype(F32)
    new_t = lambda a_t: _pad_last(a_t[:, :ns].reshape(a_t.shape[0], nb, dseq).transpose(1, 0, 2), LANES)
    bias_s = _sample_idx(pt_flat, qi_blk, wb, bf(new_t(kiwt_s[:IDX_DIM])), cache_idx_t, n_pages, topk_s, dseq)
    head_mask = (jnp.arange(ATT_WIDTH)[None, :] // ATT_DIM == jnp.arange(ATT_HEADS)[:, None])
    q3 = jnp.pad(q_rows.reshape(nb, dseq, ATT_WIDTH), ((0, 0), (0, 8 - dseq), (0, 0)))
    qb = jnp.where(head_mask[None, :, None, :], q3[:, None, :, :], 0).astype(BF16).reshape(nb, ATT_HEADS * 8, ATT_WIDTH)
    att_s64 = _sample_att(pt_flat, qb, bias_s, bf(new_t(ktf_s)), bf(new_t(vtf_s)), cache_k_t, cache_v_t, n_pages)
    a5 = att_s64.reshape(nb, ATT_HEADS, 8, ATT_HEADS, ATT_DIM)
    att_s = jnp.stack([a5[:, h, :dseq, h, :] for h in range(ATT_HEADS)], axis=2).reshape(ns, ATT_WIDTH)

    lnw = (vec(ln2_g), vec(ln2_b), *f2, vec(ln3_g), vec(ln3_b))
    y_p = _out_ffn(x1p, att_p, yret_p.reshape(batch * seq, RET_WIDTH), woa, wor, *lnw)
    att_small = _pad_rows(bf(att_s), ROW_TILE)
    yret_small = _pad_rows(yret_s[:, :dseq].reshape(ns, RET_WIDTH), ROW_TILE)
    y_s = _out_ffn(x1s, att_small, yret_small, woa, wor, *lnw)

    def with_meta(real_t, small_t, feat_shape):
        m = jnp.broadcast_to(small_t[None, :, meta], (batch, small_t.shape[0], N_META))
        full = jnp.concatenate([m, real_t], axis=2)
        full = full.reshape((batch,) + feat_shape + (seq + N_META,))
        return jnp.moveaxis(full, -1, 1)[None]

    k_prompt = with_meta(ktf_p, ktf_s, (ATT_HEADS, ATT_DIM))
    v_prompt = with_meta(vtf_p, vtf_s, (ATT_HEADS, ATT_DIM))
    idxk_prompt = with_meta(kiwt_p[:, :IDX_DIM], kiwt_s[:IDX_DIM], (IDX_DIM,))
    y_prompt = y_p.reshape(batch, seq, d)
    y_sample = y_s[:ns].reshape(nb, dseq, d)
    k_sample = ktf_s[:, :ns].T.reshape(1, nb, dseq, ATT_HEADS, ATT_DIM)
    v_sample = vtf_s[:, :ns].T.reshape(1, nb, dseq, ATT_HEADS, ATT_DIM)
    idxk_sample = kiwt_s[:IDX_DIM, :ns].T.reshape(1, nb, dseq, IDX_DIM)
    return (y_prompt, y_sample, k_prompt, v_prompt, idxk_prompt, ret_prompt[None],
            k_sample, v_sample, idxk_sample, ret_sample[None])
```

```python
import functools

import jax
import jax.numpy as jnp
import numpy as np
from jax import lax
from jax.experimental import pallas as pl
from jax.experimental.pallas import tpu as pltpu

F32 = jnp.float32
BF16 = jnp.bfloat16
I32 = jnp.int32

D_MODEL = 1024
N_META = 16
ATT_DIM = 64
ATT_HEADS = 8
ATT_WIDTH = ATT_HEADS * ATT_DIM
IDX_HEADS = 8
IDX_DIM = 64
TOPK_MAX = 256
RET_HEADS = 4
RET_DK = 128
RET_DV = 128
RET_WIDTH = RET_HEADS * RET_DV
PAGE_SIZE = 128
ROPE_BASE = 10000.0
LN_EPS = 1e-5
GN_EPS = 1e-5
DEPTH = 1
ALPHA = (2.0 * DEPTH) ** 0.25

LANES = 128
ROW_TILE = 256
FFN_TILE = 512
Q_TILE = 256
K_TILE = 256
RET_CHUNK = 128
RET_SEQS_PER_STEP = 4
IDX_PAGES_PER_STEP = 32
IDX_SEQS_PER_STEP = 2
ATT_PAGES_PER_STEP = 32
LOG2E = 1.4426950408889634
NEG_BIAS = -1e30
INT_MIN = -2147483648
VMEM_LIMIT = 56 * 1024 * 1024


def _dot(a, b):
    return jnp.dot(a, b, preferred_element_type=F32)


def _dot_nt(a, b):
    return lax.dot_general(a, b, (((1,), (1,)), ((), ())), preferred_element_type=F32)


def _layernorm(y, g, b):
    mu = jnp.mean(y, axis=-1, keepdims=True)
    d = y - mu
    var = jnp.mean(d * d, axis=-1, keepdims=True)
    return d * lax.rsqrt(var + LN_EPS) * g + b


def _swiglu(xb, wg_ref, wu_ref, wd_ref):
    hg = _dot(xb, wg_ref[...])
    hu = _dot(xb, wu_ref[...])
    act = (hg * jax.nn.sigmoid(hg)) * hu
    return _dot(act.astype(BF16), wd_ref[...])


def _ffn_ln_kernel(x_ref, wg_ref, wu_ref, wd_ref, g_ref, b_ref, o_ref):
    x = x_ref[...]
    y = ALPHA * x + 0.5 * _swiglu(x.astype(BF16), wg_ref, wu_ref, wd_ref)
    o_ref[...] = _layernorm(y, g_ref[...], b_ref[...])


def _const_spec(shape):
    return pl.BlockSpec(shape, lambda *_: (0,) * len(shape))


def _weight_spec(shape):
    return pl.BlockSpec(shape, lambda *_: (0,) * len(shape), pipeline_mode=pl.Buffered(1))


def _ffn_ln(x, wg, wu, wd, g, b):
    rows, d = x.shape
    dff = wg.shape[1]
    tm = min(FFN_TILE, rows)
    return pl.pallas_call(
        _ffn_ln_kernel,
        grid=(rows // tm,),
        in_specs=[pl.BlockSpec((tm, d), lambda i: (i, 0)),
                  _weight_spec((d, dff)), _weight_spec((d, dff)), _weight_spec((dff, d)),
                  _const_spec((1, d)), _const_spec((1, d))],
        out_specs=pl.BlockSpec((tm, d), lambda i: (i, 0)),
        out_shape=jax.ShapeDtypeStruct((rows, d), F32),
        compiler_params=pltpu.CompilerParams(
            dimension_semantics=("arbitrary",), vmem_limit_bytes=VMEM_LIMIT),
        name="ffn_ln",
    )(x, wg, wu, wd, g, b)


def _out_ffn_kernel(x1_ref, att_ref, yret_ref, woa_ref, wor_ref, g2_ref, b2_ref,
                    wg_ref, wu_ref, wd_ref, g3_ref, b3_ref, o_ref):
    m = _dot(att_ref[...], woa_ref[...]) + _dot(yret_ref[...], wor_ref[...])
    x2 = _layernorm(ALPHA * x1_ref[...] + m, g2_ref[...], b2_ref[...])
    y = ALPHA * x2 + 0.5 * _swiglu(x2.astype(BF16), wg_ref, wu_ref, wd_ref)
    o_ref[...] = _layernorm(y, g3_ref[...], b3_ref[...])


def _out_ffn(x1, att, yret, woa, wor, g2, b2, wg, wu, wd, g3, b3):
    rows, d = x1.shape
    dff = wg.shape[1]
    tm = min(FFN_TILE, rows)
    row = lambda w: pl.BlockSpec((tm, w), lambda i: (i, 0))
    return pl.pallas_call(
        _out_ffn_kernel,
        grid=(rows // tm,),
        in_specs=[row(d), row(ATT_WIDTH), row(RET_WIDTH),
                  _weight_spec((ATT_WIDTH, d)), _weight_spec((RET_WIDTH, d)),
                  _const_spec((1, d)), _const_spec((1, d)),
                  _weight_spec((d, dff)), _weight_spec((d, dff)), _weight_spec((dff, d)),
                  _const_spec((1, d)), _const_spec((1, d))],
        out_specs=row(d),
        out_shape=jax.ShapeDtypeStruct((rows, d), F32),
        compiler_params=pltpu.CompilerParams(
            dimension_semantics=("arbitrary",), vmem_limit_bytes=VMEM_LIMIT),
        name="out_ffn",
    )(x1, att, yret, woa, wor, g2, b2, wg, wu, wd, g3, b3)


W_ROW_COLS = ATT_WIDTH + LANES + 4 * RET_WIDTH
W_T_ROWS = 4 * ATT_WIDTH + LANES
V_ROWS = ATT_DIM + 16


def _rope(x, cos2, sin2):
    return x * cos2 + pltpu.roll(x, RET_DK // 2, 1) * sin2


def _proj_kernel(x1_ref, wrow_ref, wt_ref, cos_ref, sin_ref,
                 kb_ref, kib_ref, rq_ref, rk_ref, rv_ref, g_ref,
                 qt_ref, ktf_ref, vtf_ref, vt_ref, qit_ref, kiwt_ref):
    xb = x1_ref[...].astype(BF16)
    z = _dot(xb, wrow_ref[...])
    kb_ref[...] = z[:, 0:ATT_WIDTH].astype(BF16)
    kib_ref[...] = z[:, ATT_WIDTH:ATT_WIDTH + IDX_DIM].astype(BF16)
    o = ATT_WIDTH + LANES
    cos2 = cos_ref[...]
    sin2 = sin_ref[...]
    for h in range(RET_HEADS):
        sl = slice(h * RET_DK, (h + 1) * RET_DK)
        rq = z[:, o + h * RET_DK:o + (h + 1) * RET_DK]
        rk = z[:, o + RET_WIDTH + h * RET_DK:o + RET_WIDTH + (h + 1) * RET_DK]
        rq_ref[:, sl] = _rope(rq, cos2, sin2).astype(BF16)
        rk_ref[:, sl] = (_rope(rk, cos2, sin2) * (RET_DK ** -0.5)).astype(BF16)
    rv_ref[...] = z[:, o + 2 * RET_WIDTH:o + 3 * RET_WIDTH].astype(BF16)
    g_ref[...] = z[:, o + 3 * RET_WIDTH:o + 4 * RET_WIDTH]

    zt = _dot_nt(wt_ref[...], xb)
    tm = xb.shape[0]
    zero = jnp.zeros((ATT_DIM, tm), BF16)
    for h in range(ATT_HEADS):
        qh = (zt[h * ATT_DIM:(h + 1) * ATT_DIM, :] * (ATT_DIM ** -0.5 * LOG2E)).astype(BF16)
        qt_ref[0, h] = jnp.concatenate([qh, zero] if h % 2 == 0 else [zero, qh], axis=0)
    ktf_ref[0] = zt[ATT_WIDTH:2 * ATT_WIDTH, :]
    vt = zt[2 * ATT_WIDTH:3 * ATT_WIDTH, :]
    vtf_ref[0] = vt
    ones_blk = jnp.where(lax.broadcasted_iota(I32, (V_ROWS - ATT_DIM, tm), 0) == 0, 1.0, 0.0).astype(BF16)
    for h in range(ATT_HEADS):
        vt_ref[0, h] = jnp.concatenate([vt[h * ATT_DIM:(h + 1) * ATT_DIM, :].astype(BF16), ones_blk], axis=0)
    for h in range(IDX_HEADS):
        qih = zt[3 * ATT_WIDTH + h * IDX_DIM:3 * ATT_WIDTH + (h + 1) * IDX_DIM, :]
        qit_ref[0, :, h * tm:(h + 1) * tm] = (qih * (IDX_DIM ** -0.5)).astype(BF16)
    kiwt_ref[0] = zt[4 * ATT_WIDTH:4 * ATT_WIDTH + LANES, :]


def _proj(x1, wrow, wt, cos2, sin2, batch, tiles_per_seq):
    rows, d = x1.shape
    tm = ROW_TILE
    nt = rows // tm
    seq = tiles_per_seq * tm
    row = lambda w: pl.BlockSpec((tm, w), lambda i: (i, 0))
    tab = pl.BlockSpec((tm, LANES), lambda i: (i % tiles_per_seq, 0))
    tcol = lambda r: pl.BlockSpec((1, r, tm), lambda i: (i // tiles_per_seq, 0, i % tiles_per_seq))
    out_shape = (
        jax.ShapeDtypeStruct((rows, ATT_WIDTH), BF16),
        jax.ShapeDtypeStruct((rows, IDX_DIM), BF16),
        jax.ShapeDtypeStruct((rows, RET_WIDTH), BF16),
        jax.ShapeDtypeStruct((rows, RET_WIDTH), BF16),
        jax.ShapeDtypeStruct((rows, RET_WIDTH), BF16),
        jax.ShapeDtypeStruct((rows, RET_WIDTH), F32),
        jax.ShapeDtypeStruct((nt, ATT_HEADS, 2 * ATT_DIM, tm), BF16),
        jax.ShapeDtypeStruct((batch, ATT_WIDTH, seq), F32),
        jax.ShapeDtypeStruct((batch, ATT_WIDTH, seq), F32),
        jax.ShapeDtypeStruct((nt, ATT_HEADS, V_ROWS, tm), BF16),
        jax.ShapeDtypeStruct((nt, IDX_DIM, IDX_HEADS * tm), BF16),
        jax.ShapeDtypeStruct((batch, LANES, seq), F32),
    )
    out_specs = (
        row(ATT_WIDTH), row(IDX_DIM), row(RET_WIDTH), row(RET_WIDTH), row(RET_WIDTH), row(RET_WIDTH),
        pl.BlockSpec((1, ATT_HEADS, 2 * ATT_DIM, tm), lambda i: (i, 0, 0, 0)),
        tcol(ATT_WIDTH), tcol(ATT_WIDTH),
        pl.BlockSpec((1, ATT_HEADS, V_ROWS, tm), lambda i: (i, 0, 0, 0)),
        pl.BlockSpec((1, IDX_DIM, IDX_HEADS * tm), lambda i: (i, 0, 0)),
        tcol(LANES),
    )
    return pl.pallas_call(
        _proj_kernel,
        grid=(nt,),
        in_specs=[row(d), _const_spec((d, W_ROW_COLS)), _const_spec((W_T_ROWS, d)), tab, tab],
        out_specs=out_specs,
        out_shape=out_shape,
        compiler_params=pltpu.CompilerParams(
            dimension_semantics=("arbitrary",), vmem_limit_bytes=VMEM_LIMIT),
        name="proj",
    )(x1, wrow, wt, cos2, sin2)


def _pattern_to_float(u):
    key = u ^ INT_MIN
    bits = jnp.where(key >= 0, key, key ^ 0x7FFFFFFF)
    return lax.bitcast_convert_type(bits, F32)


def _kth_largest(count, topk, shape, tile_shape):
    def step(i, st):
        t_u, cnt_t = st
        cand = t_u | (1 << (31 - i))
        cand_f = jnp.broadcast_to(_pattern_to_float(cand), tile_shape)
        cnt = count(lambda s, base: jnp.where(s >= cand_f, 1, 0))
        take = cnt >= topk
        return jnp.where(take, cand, t_u), jnp.where(take, cnt, cnt_t)

    return lax.fori_loop(0, 32, step, (jnp.zeros(shape, I32), jnp.zeros(shape, I32)))


def _tie_break(count, thr, need, nbits, slot_axis, shape):
    def search(_):
        def step(i, p):
            cand = p | (1 << (nbits - 1 - i))
            cnt = count(lambda s, base: jnp.where(
                s == thr, jnp.where((base + lax.broadcasted_iota(I32, s.shape, slot_axis)) < cand, 1, 0), 0))
            return jnp.where(cnt < need, cand, p)
        return lax.fori_loop(0, nbits, step, jnp.zeros(shape, I32))

    return search


SPARE_CHUNKS = 2
COUNT_LANES = 4


def _prompt_att_kernel(qt_ref, qit_ref, wit_ref, k_ref, vt_ref, ki_ref, km_ref, vtm_ref, kim_ref, o_ref,
                       sc_ref, ot_ref, m_ref, s0_ref, s1_ref, c0_ref, c1_ref, *, topk, nq):
    j = pl.program_id(1)
    tq, tk = Q_TILE, K_TILE
    n_real = j + 1
    n_pairs = lax.shift_right_logical(n_real + 1, 1)
    col = lax.broadcasted_iota(I32, (1, tq), 1)
    rows_k = lax.broadcasted_iota(I32, (tk, tq), 0)
    rows_m = lax.broadcasted_iota(I32, (N_META, tq), 0)

    def slot_base(c):
        return pl.multiple_of(N_META + c * tk, N_META)

    def rows_of(c, n=1):
        return pl.ds(slot_base(c), n * tk)

    def stored(c):
        return jnp.minimum(c, nq - 1)

    def idx_scores(ki_chunk):
        acc = None
        for h in range(IDX_HEADS):
            s = _dot(ki_chunk, qit_ref[0, :, h * tq:(h + 1) * tq])
            t = jnp.maximum(s, 0.0) * wit_ref[0, h:h + 1, :]
            acc = t if acc is None else acc + t
        return acc

    sc_ref[0:N_META, :] = idx_scores(kim_ref[...])

    def p1(i, carry):
        for c in (2 * i, 2 * i + 1):
            last_row = j * tq + col - c * tk
            sc_ref[rows_of(c), :] = jnp.where(rows_k <= last_row, idx_scores(ki_ref[0, stored(c)]), -jnp.inf)
        return carry

    lax.fori_loop(0, n_pairs, p1, 0)
    sc_ref[rows_of(n_real, SPARE_CHUNKS), :] = jnp.full((SPARE_CHUNKS * tk, tq), -jnp.inf, F32)

    def count(ind_fn):
        def body(c2, part):
            base = slot_base(2 * c2)
            x = ind_fn(sc_ref[pl.ds(base, 2 * tk), :], base)
            return part + jnp.sum(x.reshape(-1, COUNT_LANES, 8, tq), axis=0)

        part = lax.fori_loop(0, n_pairs, body, jnp.zeros((COUNT_LANES, 8, tq), I32))
        meta = ind_fn(sc_ref[0:N_META, :], 0)
        return (jnp.sum(part.reshape(COUNT_LANES * 8, tq), axis=0, keepdims=True)
                + jnp.sum(meta, axis=0, keepdims=True))

    t_u, cnt_ge = _kth_largest(count, topk, (1, tq), (1, tq))
    short = t_u == 0
    thr = jnp.where(short, -jnp.inf, _pattern_to_float(t_u))
    cnt_gt = count(lambda s, base: jnp.where(s > thr, 1, 0))
    cnt_eq = cnt_ge - cnt_gt
    need = topk - cnt_gt
    nbits = int(N_META + tk * (nq + SPARE_CHUNKS) - 1).bit_length()
    has_ties = jnp.max(jnp.where(short, 0, cnt_eq - need)) > 0
    last = lax.cond(has_ties, _tie_break(count, thr, need, nbits, 0, (1, tq)),
                    lambda _: jnp.full((1, tq), (1 << nbits) - 1, I32), 0)
    last = jnp.where(short, -1, last)

    def bias_of(s, slot):
        keep_eq = jnp.where(slot <= last, 0.0, NEG_BIAS)
        return jnp.where(s > thr, 0.0, jnp.where(s == thr, keep_eq, NEG_BIAS))

    sc_ref[0:N_META, :] = bias_of(sc_ref[0:N_META, :], rows_m)

    def p3(c, carry):
        sc_ref[rows_of(c), :] = bias_of(sc_ref[rows_of(c), :], slot_base(c) + rows_k)
        return carry

    lax.fori_loop(0, n_real + SPARE_CHUNKS, p3, 0)

    def k_lanes(h):
        return slice((h // 2) * LANES, (h // 2 + 1) * LANES)

    rep = lambda x: jnp.broadcast_to(x, (8, tq))

    def logits_stage(c, h, s_ref, cmax_ref):
        s = _dot(k_ref[0, stored(c), :, k_lanes(h)], qt_ref[0, h]) + sc_ref[rows_of(c), :]
        s_ref[h] = s
        cmax_ref[h] = rep(jnp.max(s, axis=0, keepdims=True))

    def softmax_stage(c, h, s_ref, cmax_ref):
        m_old = m_ref[h]
        m_new = jnp.maximum(m_old, cmax_ref[h])
        a = jnp.exp2(m_old - m_new)
        p = jnp.exp2(s_ref[h] - m_new[0:1, :])
        m_ref[h] = m_new
        ot_ref[h] = a[0:1, :] * ot_ref[h] + _dot(vt_ref[0, stored(c), h], p.astype(BF16))

    pad = jnp.zeros((LANES - N_META, tq), BF16)
    for h in range(ATT_HEADS):
        s = _dot(km_ref[:, k_lanes(h)], qt_ref[0, h]) + sc_ref[0:N_META, :]
        m = jnp.max(s, axis=0, keepdims=True)
        p = jnp.exp2(s - m).astype(BF16)
        m_ref[h] = rep(m)
        ot_ref[h] = _dot(vtm_ref[h], jnp.concatenate([p, pad], axis=0))
    for h in range(ATT_HEADS):
        logits_stage(0, h, s0_ref, c0_ref)

    def p4(i, carry):
        for h in range(ATT_HEADS):
            logits_stage(2 * i + 1, h, s1_ref, c1_ref)
            softmax_stage(2 * i, h, s0_ref, c0_ref)
        for h in range(ATT_HEADS):
            logits_stage(2 * i + 2, h, s0_ref, c0_ref)
            softmax_stage(2 * i + 1, h, s1_ref, c1_ref)
        return carry

    lax.fori_loop(0, n_pairs, p4, 0)
    out_t = jnp.concatenate([ot_ref[h, 0:ATT_DIM, :] / ot_ref[h, ATT_DIM:ATT_DIM + 1, :]
                             for h in range(ATT_HEADS)], axis=0)
    o_ref[0] = out_t.T.astype(BF16)


def _prompt_att(qt, qit, wit, k4, vt4, ki4, km, vtm, kim, batch, nq, topk):
    tq = Q_TILE
    per_batch = lambda shape: pl.BlockSpec((1,) + shape, lambda b, j: (b,) + (0,) * len(shape))
    hs = (ATT_HEADS, K_TILE, tq)
    return pl.pallas_call(
        functools.partial(_prompt_att_kernel, topk=topk, nq=nq),
        grid=(batch, nq),
        in_specs=[
            pl.BlockSpec((1, ATT_HEADS, 2 * ATT_DIM, tq), lambda b, j: (b * nq + j, 0, 0, 0)),
            pl.BlockSpec((1, IDX_DIM, IDX_HEADS * tq), lambda b, j: (b * nq + j, 0, 0)),
            pl.BlockSpec((1, IDX_HEADS, tq), lambda b, j: (b, 0, j)),
            per_batch((nq, K_TILE, ATT_WIDTH)),
            per_batch((nq, ATT_HEADS, V_ROWS, K_TILE)),
            per_batch((nq, K_TILE, IDX_DIM)),
            _const_spec((N_META, ATT_WIDTH)), _const_spec((ATT_HEADS, V_ROWS, LANES)),
            _const_spec((N_META, IDX_DIM)),
        ],
        out_specs=pl.BlockSpec((1, tq, ATT_WIDTH), lambda b, j: (b * nq + j, 0, 0)),
        out_shape=jax.ShapeDtypeStruct((batch * nq, tq, ATT_WIDTH), BF16),
        scratch_shapes=[pltpu.VMEM((N_META + (nq + SPARE_CHUNKS) * K_TILE, tq), F32),
                        pltpu.VMEM((ATT_HEADS, V_ROWS, tq), F32),
                        pltpu.VMEM((ATT_HEADS, 8, tq), F32),
                        pltpu.VMEM(hs, F32), pltpu.VMEM(hs, F32),
                        pltpu.VMEM((ATT_HEADS, 8, tq), F32), pltpu.VMEM((ATT_HEADS, 8, tq), F32)],
        compiler_params=pltpu.CompilerParams(
            dimension_semantics=("arbitrary", "arbitrary"), vmem_limit_bytes=VMEM_LIMIT),
        name="prompt_att",
    )(qt, qit, wit, k4, vt4, ki4, km, vtm, kim)


def _ret_kernel(rq_ref, rk_ref, rv_ref, g_ref, s0_ref, decay_ref, xi_ref, zeta_ref, gl_ref,
                y_ref, sfin_ref, state_ref, *, shared_s0):
    t = pl.program_id(1)
    nseq = state_ref.shape[0]

    @pl.when(t == 0)
    def _():
        for r in range(nseq):
            state_ref[r] = s0_ref[0 if shared_s0 else r]

    for r in range(nseq):
        for h in range(RET_HEADS):
            sl = slice(h * RET_DK, (h + 1) * RET_DK)
            q = rq_ref[r, :, sl]
            k = rk_ref[r, :, sl]
            v = rv_ref[r, :, sl]
            sp = state_ref[r, h]
            inner = _dot_nt(q, k) * decay_ref[h]
            ret = _dot(inner.astype(BF16), v) + _dot(q, sp.astype(BF16)) * xi_ref[h]
            kz = (k.astype(F32) * zeta_ref[h]).T.astype(BF16)
            state_ref[r, h] = gl_ref[h] * sp + _dot(kz, v)
            mu = jnp.mean(ret, axis=-1, keepdims=True)
            d = ret - mu
            var = jnp.mean(d * d, axis=-1, keepdims=True)
            retn = d * lax.rsqrt(var + GN_EPS)
            g = g_ref[r, :, sl]
            y_ref[r, :, sl] = ((g * jax.nn.sigmoid(g)) * retn).astype(BF16)

    @pl.when(t == pl.num_programs(1) - 1)
    def _():
        sfin_ref[...] = state_ref[...]


def _retention(rq, rk, rv, g, s0, tables, shared_s0):
    batch, tokens, _ = rq.shape
    nchunk = tokens // RET_CHUNK
    nseq = RET_SEQS_PER_STEP if batch % RET_SEQS_PER_STEP == 0 else 1
    decay, xi, zeta, gl = tables
    tok = pl.BlockSpec((nseq, RET_CHUNK, RET_WIDTH), lambda b, t: (b, t, 0))
    hh = (RET_HEADS, RET_DK, RET_DV)
    s0_spec = (pl.BlockSpec((1,) + hh, lambda b, t: (0, 0, 0, 0)) if shared_s0 else
               pl.BlockSpec((nseq,) + hh, lambda b, t: (b, 0, 0, 0)))
    return pl.pallas_call(
        functools.partial(_ret_kernel, shared_s0=shared_s0),
        grid=(batch // nseq, nchunk),
        in_specs=[tok, tok, tok, tok, s0_spec,
                  _const_spec(hh), _const_spec(hh), _const_spec(hh), _const_spec((RET_HEADS, 1, RET_DV))],
        out_specs=(tok, pl.BlockSpec((nseq,) + hh, lambda b, t: (b, 0, 0, 0))),
        out_shape=(jax.ShapeDtypeStruct((batch, tokens, RET_WIDTH), BF16),
                   jax.ShapeDtypeStruct((batch,) + hh, F32)),
        scratch_shapes=[pltpu.VMEM((nseq,) + hh, F32)],
        compiler_params=pltpu.CompilerParams(dimension_semantics=("arbitrary", "arbitrary")),
        name="retention",
    )(rq, rk, rv, g, s0, decay, xi, zeta, gl)


def _ret_tables(length):
    lg = jnp.log(1.0 - 2.0 ** (-5.0 - jnp.arange(RET_HEADS, dtype=F32)))
    n = jnp.arange(RET_CHUNK, dtype=F32)
    live = n < length
    diff = n[:, None] - n[None, :]
    ok = (diff >= 0) & live[:, None] & live[None, :]
    decay = jnp.where(ok[None], jnp.exp(jnp.maximum(diff, 0.0)[None] * lg[:, None, None]), 0.0)
    xi = jnp.exp((n[None, :] + 1.0) * lg[:, None])
    zeta = jnp.where(live[None, :], jnp.exp((length - 1.0 - n)[None, :] * lg[:, None]), 0.0)
    bc = lambda a: jnp.broadcast_to(a[:, :, None], (RET_HEADS, RET_CHUNK, RET_DV)).astype(F32)
    gl = jnp.broadcast_to(jnp.exp(length * lg)[:, None, None], (RET_HEADS, 1, RET_DV)).astype(F32)
    return decay.astype(F32), bc(xi), bc(zeta), gl


def _sample_idx_kernel(pt_ref, qi_ref, wb_ref, kin_ref, *rest, n_steps, topk, dseq, pps):
    nseq = IDX_SEQS_PER_STEP
    pages = rest[:nseq * pps]
    bias_ref = rest[nseq * pps]
    sc_ref = rest[nseq * pps + 1]
    pc = pl.program_id(1)
    n_chunks = n_steps * pps + 1
    rows = nseq * 8

    def scores(r, ki_t):
        n = ki_t.shape[1]
        s = jnp.maximum(_dot(qi_ref[r], ki_t), 0.0) * jnp.concatenate([wb_ref[r]] * (n // LANES), axis=1)
        return jnp.sum(s.reshape(IDX_HEADS, 8, n), axis=0)

    s_step = jnp.concatenate(
        [scores(r, jnp.concatenate([pages[r * pps + p][0, 0].astype(BF16) for p in range(pps)], axis=1))
         for r in range(nseq)], axis=0)
    for p in range(pps):
        sc_ref[pc * pps + p] = s_step[:, p * LANES:(p + 1) * LANES]

    @pl.when(pc == n_steps - 1)
    def _():
        row = lax.broadcasted_iota(I32, (rows, LANES), 0) & 7
        lane = lax.broadcasted_iota(I32, (rows, LANES), 1)
        vis = (lane <= row) & (lane < dseq)
        s_new = jnp.concatenate([scores(r, kin_ref[r]) for r in range(nseq)], axis=0)
        sc_ref[n_chunks - 1] = jnp.where(vis, s_new, -jnp.inf)
        slot0 = lax.broadcasted_iota(I32, (n_chunks, rows, LANES), 0) * LANES

        def count(ind_fn):
            x = ind_fn(sc_ref[...], slot0).astype(F32)
            acc = [x[i] for i in range(COUNT_LANES)]
            for c in range(COUNT_LANES, n_chunks):
                acc[c % COUNT_LANES] = acc[c % COUNT_LANES] + x[c]
            per_lane = (acc[0] + acc[1]) + (acc[2] + acc[3])
            return jnp.sum(per_lane, axis=1, keepdims=True).astype(I32)

        real_query = (lax.broadcasted_iota(I32, (rows, 1), 0) & 7) < dseq
        t_u, cnt_ge = _kth_largest(count, topk, (rows, 1), (rows, LANES))
        short = t_u == 0
        thr = jnp.where(short, -jnp.inf, _pattern_to_float(t_u))
        cnt_gt = count(lambda s, base: jnp.where(s > thr, 1, 0))
        cnt_eq = cnt_ge - cnt_gt
        need = topk - cnt_gt
        nbits = int(n_chunks * LANES - 1).bit_length()
        has_ties = jnp.max(jnp.where(real_query, jnp.where(short, 0, cnt_eq - need), 0)) > 0
        last = lax.cond(has_ties, _tie_break(count, thr, need, nbits, 2, (rows, 1)),
                        lambda _: jnp.full((rows, 1), (1 << nbits) - 1, I32), 0)
        last = jnp.where(short, -1, last)
        s = sc_ref[...]
        slot = slot0 + lax.broadcasted_iota(I32, s.shape, 2)
        keep_eq = jnp.where(slot <= last, 0.0, NEG_BIAS)
        bias = jnp.where(s > thr, 0.0, jnp.where(s == thr, keep_eq, NEG_BIAS))
        for r in range(nseq):
            bias_ref[r] = bias[:, r * 8:(r + 1) * 8, :]


def _page_specs(pps, rows, page_table_cols, nseq=1, r=0):
    def spec(p):
        return pl.BlockSpec((1, 1, rows, PAGE_SIZE),
                            lambda b, pc, pt: (0, pt[(b * nseq + r) * page_table_cols + pc * pps + p], 0, 0))
    return [spec(p) for p in range(pps)]


def _sample_idx(pt_flat, qi_blk, wb, ki_new_t, cache_idx_t, n_pages, topk, dseq):
    nb = qi_blk.shape[0]
    pps = IDX_PAGES_PER_STEP
    nseq = IDX_SEQS_PER_STEP
    n_steps = n_pages // pps
    n_chunks = n_pages + 1
    grid_spec = pltpu.PrefetchScalarGridSpec(
        num_scalar_prefetch=1,
        grid=(nb // nseq, n_steps),
        in_specs=[pl.BlockSpec((nseq, 64, IDX_DIM), lambda b, pc, pt: (b, 0, 0)),
                  pl.BlockSpec((nseq, 64, LANES), lambda b, pc, pt: (b, 0, 0)),
                  pl.BlockSpec((nseq, IDX_DIM, LANES), lambda b, pc, pt: (b, 0, 0)),
                  *[s for r in range(nseq) for s in _page_specs(pps, IDX_DIM, n_pages, nseq, r)]],
        out_specs=pl.BlockSpec((nseq, n_chunks, 8, LANES), lambda b, pc, pt: (b, 0, 0, 0)),
        scratch_shapes=[pltpu.VMEM((n_chunks, nseq * 8, LANES), F32)],
    )
    return pl.pallas_call(
        functools.partial(_sample_idx_kernel, n_steps=n_steps, topk=topk, dseq=dseq, pps=pps),
        grid_spec=grid_spec,
        out_shape=jax.ShapeDtypeStruct((nb, n_chunks, 8, LANES), F32),
        compiler_params=pltpu.CompilerParams(dimension_semantics=("arbitrary", "arbitrary")),
        name="sample_idx",
    )(pt_flat, qi_blk, wb, ki_new_t, *([cache_idx_t] * (nseq * pps)))


def _sample_att_kernel(pt_ref, qb_ref, bias_ref, kn_ref, vn_ref, *rest, n_steps, pps):
    kpages = rest[:pps]
    vpages = rest[pps:2 * pps]
    o_ref, m_ref, l_ref, acc_ref, kbuf, vbuf = rest[2 * pps:]
    pc = pl.program_id(1)
    qb = qb_ref[0]
    n_chunks = n_steps * pps + 1

    def rows64(b8):
        return jnp.concatenate([b8] * ATT_HEADS, axis=0)

    @pl.when(pc == 0)
    def _():
        s = _dot(qb, kn_ref[0]) + rows64(bias_ref[0, n_chunks - 1])
        m = jnp.max(s, axis=1, keepdims=True)
        p = jnp.exp2(s - m)
        m_ref[...] = m
        l_ref[...] = jnp.sum(p, axis=1, keepdims=True)
        acc_ref[...] = _dot_nt(p.astype(BF16), vn_ref[0])

    for p in range(pps):
        kbuf[:, p * PAGE_SIZE:(p + 1) * PAGE_SIZE] = kpages[p][0, 0].astype(BF16)
        vbuf[:, p * PAGE_SIZE:(p + 1) * PAGE_SIZE] = vpages[p][0, 0].astype(BF16)
    bias = jnp.concatenate([bias_ref[0, pc * pps + p] for p in range(pps)], axis=1)
    s = _dot(qb, kbuf[...]) + rows64(bias)
    m_old = m_ref[...]
    m_new = jnp.maximum(m_old, jnp.max(s, axis=1, keepdims=True))
    a = jnp.exp2(m_old - m_new)
    p = jnp.exp2(s - m_new)
    m_ref[...] = m_new
    l_ref[...] = a * l_ref[...] + jnp.sum(p, axis=1, keepdims=True)
    acc_ref[...] = a * acc_ref[...] + _dot_nt(p.astype(BF16), vbuf[...])

    @pl.when(pc == n_steps - 1)
    def _():
        o_ref[0] = acc_ref[...] / l_ref[...]


def _sample_att(pt_flat, qb, bias, k_new_t, v_new_t, cache_k_t, cache_v_t, n_pages):
    nb = qb.shape[0]
    pps = ATT_PAGES_PER_STEP
    n_steps = n_pages // pps
    n_chunks = n_pages + 1
    rows = ATT_HEADS * 8
    grid_spec = pltpu.PrefetchScalarGridSpec(
        num_scalar_prefetch=1,
        grid=(nb, n_steps),
        in_specs=[pl.BlockSpec((1, rows, ATT_WIDTH), lambda b, pc, pt: (b, 0, 0)),
                  pl.BlockSpec((1, n_chunks, 8, LANES), lambda b, pc, pt: (b, 0, 0, 0)),
                  pl.BlockSpec((1, ATT_WIDTH, LANES), lambda b, pc, pt: (b, 0, 0)),
                  pl.BlockSpec((1, ATT_WIDTH, LANES), lambda b, pc, pt: (b, 0, 0)),
                  *_page_specs(pps, ATT_WIDTH, n_pages),
                  *_page_specs(pps, ATT_WIDTH, n_pages)],
        out_specs=pl.BlockSpec((1, rows, ATT_WIDTH), lambda b, pc, pt: (b, 0, 0)),
        scratch_shapes=[pltpu.VMEM((rows, 1), F32), pltpu.VMEM((rows, 1), F32),
                        pltpu.VMEM((rows, ATT_WIDTH), F32),
                        pltpu.VMEM((ATT_WIDTH, pps * PAGE_SIZE), BF16),
                        pltpu.VMEM((ATT_WIDTH, pps * PAGE_SIZE), BF16)],
    )
    return pl.pallas_call(
        functools.partial(_sample_att_kernel, n_steps=n_steps, pps=pps),
        grid_spec=grid_spec,
        out_shape=jax.ShapeDtypeStruct((nb, rows, ATT_WIDTH), F32),
        compiler_params=pltpu.CompilerParams(
            dimension_semantics=("arbitrary", "arbitrary"), vmem_limit_bytes=VMEM_LIMIT),
        name="sample_att",
    )(pt_flat, qb, bias, k_new_t, v_new_t, *([cache_k_t] * pps), *([cache_v_t] * pps))


def _rope_tables(pos):
    half = RET_DK // 2
    inv = ROPE_BASE ** (-jnp.arange(half, dtype=F32) / half)
    ang = pos.astype(F32)[:, None] * inv[None, :]
    cos, sin = jnp.cos(ang), jnp.sin(ang)
    return jnp.concatenate([cos, cos], axis=1), jnp.concatenate([-sin, sin], axis=1)


def _pad_rows(a, rows):
    return jnp.pad(a, ((0, rows - a.shape[0]),) + ((0, 0),) * (a.ndim - 1))


def _pad_last(a, n):
    return jnp.pad(a, ((0, 0),) * (a.ndim - 1) + ((0, n - a.shape[-1]),))


def kernel(x_prompt, x_sample, cache_k, cache_v, cache_idx_k, state_ret, page_table, meta_tokens,
           ffn1_w_gate, ffn1_w_up, ffn1_w_down, ln1_g, ln1_b, w_in, w_out, ln2_g, ln2_b,
           ffn2_w_gate, ffn2_w_up, ffn2_w_down, ln3_g, ln3_b):
    batch, seq, d = x_prompt.shape
    nb, dseq, _ = x_sample.shape
    n_pages = page_table.shape[1]
    n_pool = cache_k.shape[1]
    past = n_pages * PAGE_SIZE
    nq = seq // Q_TILE
    assert d == D_MODEL and seq % Q_TILE == 0 and Q_TILE == ROW_TILE == K_TILE
    assert dseq <= 8 and n_pages % IDX_PAGES_PER_STEP == 0 and n_pages % ATT_PAGES_PER_STEP == 0
    assert nb % IDX_SEQS_PER_STEP == 0
    ns = nb * dseq
    n_small = ns + N_META
    assert n_small <= ROW_TILE
    meta = slice(ns, n_small)

    bf = lambda a: a.astype(BF16)
    l = 0
    f1 = (bf(ffn1_w_gate[l]), bf(ffn1_w_up[l]), bf(ffn1_w_down[l]))
    f2 = (bf(ffn2_w_gate[l]), bf(ffn2_w_up[l]), bf(ffn2_w_down[l]))
    vec = lambda a: a[l][None, :].astype(F32)
    w = w_in[l]
    offs = np.cumsum([0, 512, 512, 512, 512, 64, 8, 512, 512, 512, 512])
    wq, wk, wv, wqi, wki, wwi, wrq, wrk, wrv, wg = [w[:, offs[i]:offs[i + 1]] for i in range(10)]
    kiw = jnp.concatenate([wki, wwi, jnp.zeros((d, LANES - IDX_DIM - IDX_HEADS), w.dtype)], axis=1)
    wrow = bf(jnp.concatenate([wk, kiw, wrq, wrk, wrv, wg], axis=1))
    wt = bf(jnp.concatenate([wq, wk, wv, wqi, kiw], axis=1).T)
    woa, wor = bf(w_out[l][:ATT_WIDTH]), bf(w_out[l][ATT_WIDTH:])

    xs = _pad_rows(jnp.concatenate([x_sample.reshape(ns, d), meta_tokens.astype(x_prompt.dtype)], axis=0), ROW_TILE)
    xp = x_prompt.reshape(batch * seq, d)
    pos_small = jnp.concatenate([jnp.tile(past + jnp.arange(dseq, dtype=I32), nb),
                                 jnp.arange(N_META, dtype=I32),
                                 jnp.zeros((ROW_TILE - n_small,), I32)])
    cos_s, sin_s = _rope_tables(pos_small)
    cos_p, sin_p = _rope_tables(N_META + jnp.arange(seq, dtype=I32))

    x1p = _ffn_ln(xp, *f1, vec(ln1_g), vec(ln1_b))
    x1s = _ffn_ln(xs, *f1, vec(ln1_g), vec(ln1_b))
    (kb_p, kib_p, rq_p, rk_p, rv_p, g_p, qt_p, ktf_p, vtf_p, vt_p, qit_p, kiwt_p) = _proj(
        x1p, wrow, wt, cos_p, sin_p, batch, nq)
    (kb_s, kib_s, rq_s, rk_s, rv_s, g_s, qt_s, ktf_s, vtf_s, vt_s, qit_s, kiwt_s) = _proj(
        x1s, wrow, wt, cos_s, sin_s, 1, 1)
    ktf_s, vtf_s, kiwt_s = ktf_s[0], vtf_s[0], kiwt_s[0]

    topk_p = min(TOPK_MAX, seq // 4)
    att_p = _prompt_att(qt_p, qit_p, kiwt_p[:, IDX_DIM:IDX_DIM + IDX_HEADS, :],
                        kb_p.reshape(batch, nq, K_TILE, ATT_WIDTH),
                        vt_p.reshape(batch, nq, ATT_HEADS, V_ROWS, K_TILE),
                        kib_p.reshape(batch, nq, K_TILE, IDX_DIM),
                        kb_s[meta], _pad_last(vt_s[0][:, :, meta], LANES), kib_s[meta], batch, nq, topk_p)
    att_p = att_p.reshape(batch * seq, ATT_WIDTH)

    pad_tok = lambda a, n: jnp.pad(a, ((0, 0), (0, RET_CHUNK - n), (0, 0)))
    m3 = lambda a: pad_tok(a[meta][None], N_META)
    zero_state = jnp.zeros((1, RET_HEADS, RET_DK, RET_DV), F32)
    _, s_meta = _retention(m3(rq_s), m3(rk_s), m3(rv_s), m3(g_s), zero_state, _ret_tables(N_META), True)
    b3 = lambda a: a.reshape(batch, seq, RET_WIDTH)
    yret_p, ret_prompt = _retention(b3(rq_p), b3(rk_p), b3(rv_p), b3(g_p), s_meta, _ret_tables(RET_CHUNK), True)
    s3 = lambda a: pad_tok(a[:ns].reshape(nb, dseq, RET_WIDTH), dseq)
    yret_s, ret_sample = _retention(s3(rq_s), s3(rk_s), s3(rv_s), s3(g_s), state_ret[l], _ret_tables(dseq), False)

    topk_s = min(TOPK_MAX, (past + dseq) // 4)
    pt_flat = page_table.reshape(-1).astype(I32)
    cache_k_t = jnp.transpose(cache_k[l], (0, 2, 3, 1)).reshape(1, n_pool, ATT_WIDTH, PAGE_SIZE)
    cache_v_t = jnp.transpose(cache_v[l], (0, 2, 3, 1)).reshape(1, n_pool, ATT_WIDTH, PAGE_SIZE)
    cache_idx_t = jnp.transpose(cache_idx_k[l], (0, 2, 1))[None]
    q_rows = jnp.concatenate([qt_s[0, h, (h % 2) * ATT_DIM:(h % 2 + 1) * ATT_DIM, :] for h in range(ATT_HEADS)],
                             axis=0).T[:ns]
    qi_rows = qit_s[0].reshape(IDX_DIM, IDX_HEADS, ROW_TILE)[:, :, :ns]
    qi_blk = jnp.pad(qi_rows.transpose(2, 1, 0).reshape(nb, dseq, IDX_HEADS, IDX_DIM).transpose(0, 2, 1, 3),
                     ((0, 0), (0, 0), (0, 8 - dseq), (0, 0))).reshape(nb, IDX_HEADS * 8, IDX_DIM)
    wi_s = kiwt_s[IDX_DIM:IDX_DIM + IDX_HEADS, :ns].reshape(IDX_HEADS, nb, dseq)
    wb = jnp.pad(wi_s.transpose(1, 0, 2), ((0, 0), (0, 0), (0, 8 - dseq))).reshape(nb, IDX_HEADS * 8, 1)
    wb = jnp.broadcast_to(wb, (nb, IDX_HEADS * 8, LANES)).astype(F32)
    new_t = lambda a_t: _pad_last(a_t[:, :ns].reshape(a_t.shape[0], nb, dseq).transpose(1, 0, 2), LANES)
    bias_s = _sample_idx(pt_flat, qi_blk, wb, bf(new_t(kiwt_s[:IDX_DIM])), cache_idx_t, n_pages, topk_s, dseq)
    head_mask = (jnp.arange(ATT_WIDTH)[None, :] // ATT_DIM == jnp.arange(ATT_HEADS)[:, None])
    q3 = jnp.pad(q_rows.reshape(nb, dseq, ATT_WIDTH), ((0, 0), (0, 8 - dseq), (0, 0)))
    qb = jnp.where(head_mask[None, :, None, :], q3[:, None, :, :], 0).astype(BF16).reshape(nb, ATT_HEADS * 8, ATT_WIDTH)
    att_s64 = _sample_att(pt_flat, qb, bias_s, bf(new_t(ktf_s)), bf(new_t(vtf_s)), cache_k_t, cache_v_t, n_pages)
    a5 = att_s64.reshape(nb, ATT_HEADS, 8, ATT_HEADS, ATT_DIM)
    att_s = jnp.stack([a5[:, h, :dseq, h, :] for h in range(ATT_HEADS)], axis=2).reshape(ns, ATT_WIDTH)

    lnw = (vec(ln2_g), vec(ln2_b), *f2, vec(ln3_g), vec(ln3_b))
    y_p = _out_ffn(x1p, att_p, yret_p.reshape(batch * seq, RET_WIDTH), woa, wor, *lnw)
    att_small = _pad_rows(bf(att_s), ROW_TILE)
    yret_small = _pad_rows(yret_s[:, :dseq].reshape(ns, RET_WIDTH), ROW_TILE)
    y_s = _out_ffn(x1s, att_small, yret_small, woa, wor, *lnw)

    def with_meta(real_t, small_t, feat_shape):
        m = jnp.broadcast_to(small_t[None, :, meta], (batch, small_t.shape[0], N_META))
        full = jnp.concatenate([m, real_t], axis=2)
        full = full.reshape((batch,) + feat_shape + (seq + N_META,))
        return jnp.moveaxis(full, -1, 1)[None]

    k_prompt = with_meta(ktf_p, ktf_s, (ATT_HEADS, ATT_DIM))
    v_prompt = with_meta(vtf_p, vtf_s, (ATT_HEADS, ATT_DIM))
    idxk_prompt = with_meta(kiwt_p[:, :IDX_DIM], kiwt_s[:IDX_DIM], (IDX_DIM,))
    y_prompt = y_p.reshape(batch, seq, d)
    y_sample = y_s[:ns].reshape(nb, dseq, d)
    k_sample = ktf_s[:, :ns].T.reshape(1, nb, dseq, ATT_HEADS, ATT_DIM)
    v_sample = vtf_s[:, :ns].T.reshape(1, nb, dseq, ATT_HEADS, ATT_DIM)
    idxk_sample = kiwt_s[:IDX_DIM, :ns].T.reshape(1, nb, dseq, IDX_DIM)
    return (y_prompt, y_sample, k_prompt, v_prompt, idxk_prompt, ret_prompt[None],
            k_sample, v_sample, idxk_sample, ret_sample[None])
```

```python
import functools

import jax
import jax.numpy as jnp
import numpy as np
from jax import lax
from jax.experimental import pallas as pl
from jax.experimental.pallas import tpu as pltpu

F32 = jnp.float32
BF16 = jnp.bfloat16
I32 = jnp.int32

D_MODEL = 1024
N_META = 16
ATT_DIM = 64
ATT_HEADS = 8
ATT_WIDTH = ATT_HEADS * ATT_DIM
IDX_HEADS = 8
IDX_DIM = 64
TOPK_MAX = 256
RET_HEADS = 4
RET_DK = 128
RET_DV = 128
RET_WIDTH = RET_HEADS * RET_DV
PAGE_SIZE = 128
ROPE_BASE = 10000.0
LN_EPS = 1e-5
GN_EPS = 1e-5
DEPTH = 1
ALPHA = (2.0 * DEPTH) ** 0.25

LANES = 128
ROW_TILE = 256
FFN_TILE = 512
Q_TILE = 256
K_TILE = 256
RET_CHUNK = 128
RET_SEQS_PER_STEP = 4
IDX_PAGES_PER_STEP = 32
IDX_SEQS_PER_STEP = 4
ATT_PAGES_PER_STEP = 32
LOG2E = 1.4426950408889634
NEG_BIAS = -1e30
INT_MIN = -2147483648
VMEM_LIMIT = 56 * 1024 * 1024


def _dot(a, b):
    return jnp.dot(a, b, preferred_element_type=F32)


def _dot_nt(a, b):
    return lax.dot_general(a, b, (((1,), (1,)), ((), ())), preferred_element_type=F32)


def _layernorm(y, g, b):
    mu = jnp.mean(y, axis=-1, keepdims=True)
    d = y - mu
    var = jnp.mean(d * d, axis=-1, keepdims=True)
    return d * lax.rsqrt(var + LN_EPS) * g + b


def _swiglu(xb, wg_ref, wu_ref, wd_ref):
    hg = _dot(xb, wg_ref[...])
    hu = _dot(xb, wu_ref[...])
    act = (hg * jax.nn.sigmoid(hg)) * hu
    return _dot(act.astype(BF16), wd_ref[...])


def _ffn_ln_kernel(x_ref, wg_ref, wu_ref, wd_ref, g_ref, b_ref, o_ref):
    x = x_ref[...]
    y = ALPHA * x + 0.5 * _swiglu(x.astype(BF16), wg_ref, wu_ref, wd_ref)
    o_ref[...] = _layernorm(y, g_ref[...], b_ref[...])


def _const_spec(shape):
    return pl.BlockSpec(shape, lambda *_: (0,) * len(shape))


def _weight_spec(shape):
    return pl.BlockSpec(shape, lambda *_: (0,) * len(shape), pipeline_mode=pl.Buffered(1))


def _ffn_ln(x, wg, wu, wd, g, b):
    rows, d = x.shape
    dff = wg.shape[1]
    tm = min(FFN_TILE, rows)
    return pl.pallas_call(
        _ffn_ln_kernel,
        grid=(rows // tm,),
        in_specs=[pl.BlockSpec((tm, d), lambda i: (i, 0)),
                  _weight_spec((d, dff)), _weight_spec((d, dff)), _weight_spec((dff, d)),
                  _const_spec((1, d)), _const_spec((1, d))],
        out_specs=pl.BlockSpec((tm, d), lambda i: (i, 0)),
        out_shape=jax.ShapeDtypeStruct((rows, d), F32),
        compiler_params=pltpu.CompilerParams(
            dimension_semantics=("arbitrary",), vmem_limit_bytes=VMEM_LIMIT),
        name="ffn_ln",
    )(x, wg, wu, wd, g, b)


def _out_ffn_kernel(x1_ref, att_ref, yret_ref, woa_ref, wor_ref, g2_ref, b2_ref,
                    wg_ref, wu_ref, wd_ref, g3_ref, b3_ref, o_ref):
    m = _dot(att_ref[...], woa_ref[...]) + _dot(yret_ref[...], wor_ref[...])
    x2 = _layernorm(ALPHA * x1_ref[...] + m, g2_ref[...], b2_ref[...])
    y = ALPHA * x2 + 0.5 * _swiglu(x2.astype(BF16), wg_ref, wu_ref, wd_ref)
    o_ref[...] = _layernorm(y, g3_ref[...], b3_ref[...])


def _out_ffn(x1, att, yret, woa, wor, g2, b2, wg, wu, wd, g3, b3):
    rows, d = x1.shape
    dff = wg.shape[1]
    tm = min(FFN_TILE, rows)
    row = lambda w: pl.BlockSpec((tm, w), lambda i: (i, 0))
    return pl.pallas_call(
        _out_ffn_kernel,
        grid=(rows // tm,),
        in_specs=[row(d), row(ATT_WIDTH), row(RET_WIDTH),
                  _weight_spec((ATT_WIDTH, d)), _weight_spec((RET_WIDTH, d)),
                  _const_spec((1, d)), _const_spec((1, d)),
                  _weight_spec((d, dff)), _weight_spec((d, dff)), _weight_spec((dff, d)),
                  _const_spec((1, d)), _const_spec((1, d))],
        out_specs=row(d),
        out_shape=jax.ShapeDtypeStruct((rows, d), F32),
        compiler_params=pltpu.CompilerParams(
            dimension_semantics=("arbitrary",), vmem_limit_bytes=VMEM_LIMIT),
        name="out_ffn",
    )(x1, att, yret, woa, wor, g2, b2, wg, wu, wd, g3, b3)


W_ROW_COLS = ATT_WIDTH + LANES + 4 * RET_WIDTH
W_T_ROWS = 4 * ATT_WIDTH + LANES
V_ROWS = ATT_DIM + 16


def _rope(x, cos2, sin2):
    return x * cos2 + pltpu.roll(x, RET_DK // 2, 1) * sin2


def _proj_kernel(x1_ref, wrow_ref, wt_ref, cos_ref, sin_ref,
                 kb_ref, kib_ref, rq_ref, rk_ref, rv_ref, g_ref,
                 qt_ref, ktf_ref, vtf_ref, vt_ref, qit_ref, kiwt_ref):
    xb = x1_ref[...].astype(BF16)
    z = _dot(xb, wrow_ref[...])
    kb_ref[...] = z[:, 0:ATT_WIDTH].astype(BF16)
    kib_ref[...] = z[:, ATT_WIDTH:ATT_WIDTH + IDX_DIM].astype(BF16)
    o = ATT_WIDTH + LANES
    cos2 = cos_ref[...]
    sin2 = sin_ref[...]
    for h in range(RET_HEADS):
        sl = slice(h * RET_DK, (h + 1) * RET_DK)
        rq = z[:, o + h * RET_DK:o + (h + 1) * RET_DK]
        rk = z[:, o + RET_WIDTH + h * RET_DK:o + RET_WIDTH + (h + 1) * RET_DK]
        rq_ref[:, sl] = _rope(rq, cos2, sin2).astype(BF16)
        rk_ref[:, sl] = (_rope(rk, cos2, sin2) * (RET_DK ** -0.5)).astype(BF16)
    rv_ref[...] = z[:, o + 2 * RET_WIDTH:o + 3 * RET_WIDTH].astype(BF16)
    g_ref[...] = z[:, o + 3 * RET_WIDTH:o + 4 * RET_WIDTH]

    zt = _dot_nt(wt_ref[...], xb)
    tm = xb.shape[0]
    zero = jnp.zeros((ATT_DIM, tm), BF16)
    for h in range(ATT_HEADS):
        qh = (zt[h * ATT_DIM:(h + 1) * ATT_DIM, :] * (ATT_DIM ** -0.5 * LOG2E)).astype(BF16)
        qt_ref[0, h] = jnp.concatenate([qh, zero] if h % 2 == 0 else [zero, qh], axis=0)
    ktf_ref[0] = zt[ATT_WIDTH:2 * ATT_WIDTH, :]
    vt = zt[2 * ATT_WIDTH:3 * ATT_WIDTH, :]
    vtf_ref[0] = vt
    ones_blk = jnp.where(lax.broadcasted_iota(I32, (V_ROWS - ATT_DIM, tm), 0) == 0, 1.0, 0.0).astype(BF16)
    for h in range(ATT_HEADS):
        vt_ref[0, h] = jnp.concatenate([vt[h * ATT_DIM:(h + 1) * ATT_DIM, :].astype(BF16), ones_blk], axis=0)
    for h in range(IDX_HEADS):
        qih = zt[3 * ATT_WIDTH + h * IDX_DIM:3 * ATT_WIDTH + (h + 1) * IDX_DIM, :]
        qit_ref[0, :, h * tm:(h + 1) * tm] = (qih * (IDX_DIM ** -0.5)).astype(BF16)
    kiwt_ref[0] = zt[4 * ATT_WIDTH:4 * ATT_WIDTH + LANES, :]


def _proj(x1, wrow, wt, cos2, sin2, batch, tiles_per_seq):
    rows, d = x1.shape
    tm = ROW_TILE
    nt = rows // tm
    seq = tiles_per_seq * tm
    row = lambda w: pl.BlockSpec((tm, w), lambda i: (i, 0))
    tab = pl.BlockSpec((tm, LANES), lambda i: (i % tiles_per_seq, 0))
    tcol = lambda r: pl.BlockSpec((1, r, tm), lambda i: (i // tiles_per_seq, 0, i % tiles_per_seq))
    out_shape = (
        jax.ShapeDtypeStruct((rows, ATT_WIDTH), BF16),
        jax.ShapeDtypeStruct((rows, IDX_DIM), BF16),
        jax.ShapeDtypeStruct((rows, RET_WIDTH), BF16),
        jax.ShapeDtypeStruct((rows, RET_WIDTH), BF16),
        jax.ShapeDtypeStruct((rows, RET_WIDTH), BF16),
        jax.ShapeDtypeStruct((rows, RET_WIDTH), F32),
        jax.ShapeDtypeStruct((nt, ATT_HEADS, 2 * ATT_DIM, tm), BF16),
        jax.ShapeDtypeStruct((batch, ATT_WIDTH, seq), F32),
        jax.ShapeDtypeStruct((batch, ATT_WIDTH, seq), F32),
        jax.ShapeDtypeStruct((nt, ATT_HEADS, V_ROWS, tm), BF16),
        jax.ShapeDtypeStruct((nt, IDX_DIM, IDX_HEADS * tm), BF16),
        jax.ShapeDtypeStruct((batch, LANES, seq), F32),
    )
    out_specs = (
        row(ATT_WIDTH), row(IDX_DIM), row(RET_WIDTH), row(RET_WIDTH), row(RET_WIDTH), row(RET_WIDTH),
        pl.BlockSpec((1, ATT_HEADS, 2 * ATT_DIM, tm), lambda i: (i, 0, 0, 0)),
        tcol(ATT_WIDTH), tcol(ATT_WIDTH),
        pl.BlockSpec((1, ATT_HEADS, V_ROWS, tm), lambda i: (i, 0, 0, 0)),
        pl.BlockSpec((1, IDX_DIM, IDX_HEADS * tm), lambda i: (i, 0, 0)),
        tcol(LANES),
    )
    return pl.pallas_call(
        _proj_kernel,
        grid=(nt,),
        in_specs=[row(d), _const_spec((d, W_ROW_COLS)), _const_spec((W_T_ROWS, d)), tab, tab],
        out_specs=out_specs,
        out_shape=out_shape,
        compiler_params=pltpu.CompilerParams(
            dimension_semantics=("arbitrary",), vmem_limit_bytes=VMEM_LIMIT),
        name="proj",
    )(x1, wrow, wt, cos2, sin2)


def _pattern_to_float(u):
    key = u ^ INT_MIN
    bits = jnp.where(key >= 0, key, key ^ 0x7FFFFFFF)
    return lax.bitcast_convert_type(bits, F32)


def _kth_largest(count, topk, shape, tile_shape):
    def step(i, st):
        t_u, cnt_t = st
        cand = t_u | (1 << (31 - i))
        cand_f = jnp.broadcast_to(_pattern_to_float(cand), tile_shape)
        cnt = count(lambda s, base: jnp.where(s >= cand_f, 1, 0))
        take = cnt >= topk
        return jnp.where(take, cand, t_u), jnp.where(take, cnt, cnt_t)

    return lax.fori_loop(0, 32, step, (jnp.zeros(shape, I32), jnp.zeros(shape, I32)))


def _tie_break(count, thr, need, nbits, slot_axis, shape):
    def search(_):
        def step(i, p):
            cand = p | (1 << (nbits - 1 - i))
            cnt = count(lambda s, base: jnp.where(
                s == thr, jnp.where((base + lax.broadcasted_iota(I32, s.shape, slot_axis)) < cand, 1, 0), 0))
            return jnp.where(cnt < need, cand, p)
        return lax.fori_loop(0, nbits, step, jnp.zeros(shape, I32))

    return search


SPARE_CHUNKS = 2
COUNT_LANES = 4


def _prompt_att_kernel(qt_ref, qit_ref, wit_ref, k_ref, vt_ref, ki_ref, km_ref, vtm_ref, kim_ref, o_ref,
                       sc_ref, ot_ref, m_ref, s0_ref, s1_ref, c0_ref, c1_ref, *, topk, nq):
    j = pl.program_id(1)
    tq, tk = Q_TILE, K_TILE
    n_real = j + 1
    n_pairs = lax.shift_right_logical(n_real + 1, 1)
    col = lax.broadcasted_iota(I32, (1, tq), 1)
    rows_k = lax.broadcasted_iota(I32, (tk, tq), 0)
    rows_m = lax.broadcasted_iota(I32, (N_META, tq), 0)

    def slot_base(c):
        return pl.multiple_of(N_META + c * tk, N_META)

    def rows_of(c, n=1):
        return pl.ds(slot_base(c), n * tk)

    def stored(c):
        return jnp.minimum(c, nq - 1)

    def idx_scores(ki_chunk):
        acc = None
        for h in range(IDX_HEADS):
            s = _dot(ki_chunk, qit_ref[0, :, h * tq:(h + 1) * tq])
            t = jnp.maximum(s, 0.0) * wit_ref[0, h:h + 1, :]
            acc = t if acc is None else acc + t
        return acc

    sc_ref[0:N_META, :] = idx_scores(kim_ref[...])

    def p1(i, carry):
        for c in (2 * i, 2 * i + 1):
            last_row = j * tq + col - c * tk
            sc_ref[rows_of(c), :] = jnp.where(rows_k <= last_row, idx_scores(ki_ref[0, stored(c)]), -jnp.inf)
        return carry

    lax.fori_loop(0, n_pairs, p1, 0)
    sc_ref[rows_of(n_real, SPARE_CHUNKS), :] = jnp.full((SPARE_CHUNKS * tk, tq), -jnp.inf, F32)

    def count(ind_fn):
        def body(c2, part):
            base = slot_base(2 * c2)
            x = ind_fn(sc_ref[pl.ds(base, 2 * tk), :], base)
            return part + jnp.sum(x.reshape(-1, COUNT_LANES, 8, tq), axis=0)

        part = lax.fori_loop(0, n_pairs, body, jnp.zeros((COUNT_LANES, 8, tq), I32))
        meta = ind_fn(sc_ref[0:N_META, :], 0)
        return (jnp.sum(part.reshape(COUNT_LANES * 8, tq), axis=0, keepdims=True)
                + jnp.sum(meta, axis=0, keepdims=True))

    t_u, cnt_ge = _kth_largest(count, topk, (1, tq), (1, tq))
    short = t_u == 0
    thr = jnp.where(short, -jnp.inf, _pattern_to_float(t_u))
    cnt_gt = count(lambda s, base: jnp.where(s > thr, 1, 0))
    cnt_eq = cnt_ge - cnt_gt
    need = topk - cnt_gt
    nbits = int(N_META + tk * (nq + SPARE_CHUNKS) - 1).bit_length()
    has_ties = jnp.max(jnp.where(short, 0, cnt_eq - need)) > 0
    last = lax.cond(has_ties, _tie_break(count, thr, need, nbits, 0, (1, tq)),
                    lambda _: jnp.full((1, tq), (1 << nbits) - 1, I32), 0)
    last = jnp.where(short, -1, last)

    def bias_of(s, slot):
        keep_eq = jnp.where(slot <= last, 0.0, NEG_BIAS)
        return jnp.where(s > thr, 0.0, jnp.where(s == thr, keep_eq, NEG_BIAS))

    sc_ref[0:N_META, :] = bias_of(sc_ref[0:N_META, :], rows_m)

    def p3(c, carry):
        sc_ref[rows_of(c), :] = bias_of(sc_ref[rows_of(c), :], slot_base(c) + rows_k)
        return carry

    lax.fori_loop(0, n_real + SPARE_CHUNKS, p3, 0)

    def k_lanes(h):
        return slice((h // 2) * LANES, (h // 2 + 1) * LANES)

    rep = lambda x: jnp.broadcast_to(x, (8, tq))

    def logits_stage(c, h, s_ref, cmax_ref):
        s = _dot(k_ref[0, stored(c), :, k_lanes(h)], qt_ref[0, h]) + sc_ref[rows_of(c), :]
        s_ref[h] = s
        cmax_ref[h] = rep(jnp.max(s, axis=0, keepdims=True))

    def softmax_stage(c, h, s_ref, cmax_ref):
        m_old = m_ref[h]
        m_new = jnp.maximum(m_old, cmax_ref[h])
        a = jnp.exp2(m_old - m_new)
        p = jnp.exp2(s_ref[h] - m_new[0:1, :])
        m_ref[h] = m_new
        ot_ref[h] = a[0:1, :] * ot_ref[h] + _dot(vt_ref[0, stored(c), h], p.astype(BF16))

    pad = jnp.zeros((LANES - N_META, tq), BF16)
    for h in range(ATT_HEADS):
        s = _dot(km_ref[:, k_lanes(h)], qt_ref[0, h]) + sc_ref[0:N_META, :]
        m = jnp.max(s, axis=0, keepdims=True)
        p = jnp.exp2(s - m).astype(BF16)
        m_ref[h] = rep(m)
        ot_ref[h] = _dot(vtm_ref[h], jnp.concatenate([p, pad], axis=0))
    for h in range(ATT_HEADS):
        logits_stage(0, h, s0_ref, c0_ref)

    def p4(i, carry):
        for h in range(ATT_HEADS):
            logits_stage(2 * i + 1, h, s1_ref, c1_ref)
            softmax_stage(2 * i, h, s0_ref, c0_ref)
        for h in range(ATT_HEADS):
            logits_stage(2 * i + 2, h, s0_ref, c0_ref)
            softmax_stage(2 * i + 1, h, s1_ref, c1_ref)
        return carry

    lax.fori_loop(0, n_pairs, p4, 0)
    out_t = jnp.concatenate([ot_ref[h, 0:ATT_DIM, :] / ot_ref[h, ATT_DIM:ATT_DIM + 1, :]
                             for h in range(ATT_HEADS)], axis=0)
    o_ref[0] = out_t.T.astype(BF16)


def _prompt_att(qt, qit, wit, k4, vt4, ki4, km, vtm, kim, batch, nq, topk):
    tq = Q_TILE
    per_batch = lambda shape: pl.BlockSpec((1,) + shape, lambda b, j: (b,) + (0,) * len(shape))
    hs = (ATT_HEADS, K_TILE, tq)
    return pl.pallas_call(
        functools.partial(_prompt_att_kernel, topk=topk, nq=nq),
        grid=(batch, nq),
        in_specs=[
            pl.BlockSpec((1, ATT_HEADS, 2 * ATT_DIM, tq), lambda b, j: (b * nq + j, 0, 0, 0)),
            pl.BlockSpec((1, IDX_DIM, IDX_HEADS * tq), lambda b, j: (b * nq + j, 0, 0)),
            pl.BlockSpec((1, IDX_HEADS, tq), lambda b, j: (b, 0, j)),
            per_batch((nq, K_TILE, ATT_WIDTH)),
            per_batch((nq, ATT_HEADS, V_ROWS, K_TILE)),
            per_batch((nq, K_TILE, IDX_DIM)),
            _const_spec((N_META, ATT_WIDTH)), _const_spec((ATT_HEADS, V_ROWS, LANES)),
            _const_spec((N_META, IDX_DIM)),
        ],
        out_specs=pl.BlockSpec((1, tq, ATT_WIDTH), lambda b, j: (b * nq + j, 0, 0)),
        out_shape=jax.ShapeDtypeStruct((batch * nq, tq, ATT_WIDTH), BF16),
        scratch_shapes=[pltpu.VMEM((N_META + (nq + SPARE_CHUNKS) * K_TILE, tq), F32),
                        pltpu.VMEM((ATT_HEADS, V_ROWS, tq), F32),
                        pltpu.VMEM((ATT_HEADS, 8, tq), F32),
                        pltpu.VMEM(hs, F32), pltpu.VMEM(hs, F32),
                        pltpu.VMEM((ATT_HEADS, 8, tq), F32), pltpu.VMEM((ATT_HEADS, 8, tq), F32)],
        compiler_params=pltpu.CompilerParams(
            dimension_semantics=("arbitrary", "arbitrary"), vmem_limit_bytes=VMEM_LIMIT),
        name="prompt_att",
    )(qt, qit, wit, k4, vt4, ki4, km, vtm, kim)


def _ret_kernel(rq_ref, rk_ref, rv_ref, g_ref, s0_ref, decay_ref, xi_ref, zeta_ref, gl_ref,
                y_ref, sfin_ref, state_ref, *, shared_s0):
    t = pl.program_id(1)
    nseq = state_ref.shape[0]

    @pl.when(t == 0)
    def _():
        for r in range(nseq):
            state_ref[r] = s0_ref[0 if shared_s0 else r]

    for r in range(nseq):
        for h in range(RET_HEADS):
            sl = slice(h * RET_DK, (h + 1) * RET_DK)
            q = rq_ref[r, :, sl]
            k = rk_ref[r, :, sl]
            v = rv_ref[r, :, sl]
            sp = state_ref[r, h]
            inner = _dot_nt(q, k) * decay_ref[h]
            ret = _dot(inner.astype(BF16), v) + _dot(q, sp.astype(BF16)) * xi_ref[h]
            kz = (k.astype(F32) * zeta_ref[h]).T.astype(BF16)
            state_ref[r, h] = gl_ref[h] * sp + _dot(kz, v)
            mu = jnp.mean(ret, axis=-1, keepdims=True)
            d = ret - mu
            var = jnp.mean(d * d, axis=-1, keepdims=True)
            retn = d * lax.rsqrt(var + GN_EPS)
            g = g_ref[r, :, sl]
            y_ref[r, :, sl] = ((g * jax.nn.sigmoid(g)) * retn).astype(BF16)

    @pl.when(t == pl.num_programs(1) - 1)
    def _():
        sfin_ref[...] = state_ref[...]


def _retention(rq, rk, rv, g, s0, tables, shared_s0):
    batch, tokens, _ = rq.shape
    nchunk = tokens // RET_CHUNK
    nseq = RET_SEQS_PER_STEP if batch % RET_SEQS_PER_STEP == 0 else 1
    decay, xi, zeta, gl = tables
    tok = pl.BlockSpec((nseq, RET_CHUNK, RET_WIDTH), lambda b, t: (b, t, 0))
    hh = (RET_HEADS, RET_DK, RET_DV)
    s0_spec = (pl.BlockSpec((1,) + hh, lambda b, t: (0, 0, 0, 0)) if shared_s0 else
               pl.BlockSpec((nseq,) + hh, lambda b, t: (b, 0, 0, 0)))
    return pl.pallas_call(
        functools.partial(_ret_kernel, shared_s0=shared_s0),
        grid=(batch // nseq, nchunk),
        in_specs=[tok, tok, tok, tok, s0_spec,
                  _const_spec(hh), _const_spec(hh), _const_spec(hh), _const_spec((RET_HEADS, 1, RET_DV))],
        out_specs=(tok, pl.BlockSpec((nseq,) + hh, lambda b, t: (b, 0, 0, 0))),
        out_shape=(jax.ShapeDtypeStruct((batch, tokens, RET_WIDTH), BF16),
                   jax.ShapeDtypeStruct((batch,) + hh, F32)),
        scratch_shapes=[pltpu.VMEM((nseq,) + hh, F32)],
        compiler_params=pltpu.CompilerParams(dimension_semantics=("arbitrary", "arbitrary")),
        name="retention",
    )(rq, rk, rv, g, s0, decay, xi, zeta, gl)


def _ret_tables(length):
    lg = jnp.log(1.0 - 2.0 ** (-5.0 - jnp.arange(RET_HEADS, dtype=F32)))
    n = jnp.arange(RET_CHUNK, dtype=F32)
    live = n < length
    diff = n[:, None] - n[None, :]
    ok = (diff >= 0) & live[:, None] & live[None, :]
    decay = jnp.where(ok[None], jnp.exp(jnp.maximum(diff, 0.0)[None] * lg[:, None, None]), 0.0)
    xi = jnp.exp((n[None, :] + 1.0) * lg[:, None])
    zeta = jnp.where(live[None, :], jnp.exp((length - 1.0 - n)[None, :] * lg[:, None]), 0.0)
    bc = lambda a: jnp.broadcast_to(a[:, :, None], (RET_HEADS, RET_CHUNK, RET_DV)).astype(F32)
    gl = jnp.broadcast_to(jnp.exp(length * lg)[:, None, None], (RET_HEADS, 1, RET_DV)).astype(F32)
    return decay.astype(F32), bc(xi), bc(zeta), gl


def _sample_idx_kernel(pt_ref, qi_ref, wb_ref, kin_ref, *rest, n_steps, topk, dseq, pps):
    nseq = IDX_SEQS_PER_STEP
    pages = rest[:nseq * pps]
    bias_ref = rest[nseq * pps]
    sc_ref = rest[nseq * pps + 1]
    pc = pl.program_id(1)
    n_chunks = n_steps * pps + 1
    rows = nseq * 8

    def scores(r, ki_t):
        n = ki_t.shape[1]
        s = jnp.maximum(_dot(qi_ref[r], ki_t), 0.0) * jnp.concatenate([wb_ref[r]] * (n // LANES), axis=1)
        return jnp.sum(s.reshape(IDX_HEADS, 8, n), axis=0)

    s_step = jnp.concatenate(
        [scores(r, jnp.concatenate([pages[r * pps + p][0, 0].astype(BF16) for p in range(pps)], axis=1))
         for r in range(nseq)], axis=0)
    for p in range(pps):
        sc_ref[pc * pps + p] = s_step[:, p * LANES:(p + 1) * LANES]

    @pl.when(pc == n_steps - 1)
    def _():
        row = lax.broadcasted_iota(I32, (rows, LANES), 0) & 7
        lane = lax.broadcasted_iota(I32, (rows, LANES), 1)
        vis = (lane <= row) & (lane < dseq)
        s_new = jnp.concatenate([scores(r, kin_ref[r]) for r in range(nseq)], axis=0)
        sc_ref[n_chunks - 1] = jnp.where(vis, s_new, -jnp.inf)
        slot0 = lax.broadcasted_iota(I32, (n_chunks, rows, LANES), 0) * LANES

        def count(ind_fn):
            x = ind_fn(sc_ref[...], slot0).astype(F32)
            acc = [x[i] for i in range(COUNT_LANES)]
            for c in range(COUNT_LANES, n_chunks):
                acc[c % COUNT_LANES] = acc[c % COUNT_LANES] + x[c]
            per_lane = (acc[0] + acc[1]) + (acc[2] + acc[3])
            return jnp.sum(per_lane, axis=1, keepdims=True).astype(I32)

        real_query = (lax.broadcasted_iota(I32, (rows, 1), 0) & 7) < dseq
        t_u, cnt_ge = _kth_largest(count, topk, (rows, 1), (rows, LANES))
        short = t_u == 0
        thr = jnp.where(short, -jnp.inf, _pattern_to_float(t_u))
        cnt_gt = count(lambda s, base: jnp.where(s > thr, 1, 0))
        cnt_eq = cnt_ge - cnt_gt
        need = topk - cnt_gt
        nbits = int(n_chunks * LANES - 1).bit_length()
        has_ties = jnp.max(jnp.where(real_query, jnp.where(short, 0, cnt_eq - need), 0)) > 0
        last = lax.cond(has_ties, _tie_break(count, thr, need, nbits, 2, (rows, 1)),
                        lambda _: jnp.full((rows, 1), (1 << nbits) - 1, I32), 0)
        last = jnp.where(short, -1, last)
        s = sc_ref[...]
        slot = slot0 + lax.broadcasted_iota(I32, s.shape, 2)
        keep_eq = jnp.where(slot <= last, 0.0, NEG_BIAS)
        bias = jnp.where(s > thr, 0.0, jnp.where(s == thr, keep_eq, NEG_BIAS))
        for r in range(nseq):
            bias_ref[r] = bias[:, r * 8:(r + 1) * 8, :]


def _page_specs(pps, rows, page_table_cols, nseq=1, r=0):
    def spec(p):
        return pl.BlockSpec((1, 1, rows, PAGE_SIZE),
                            lambda b, pc, pt: (0, pt[(b * nseq + r) * page_table_cols + pc * pps + p], 0, 0))
    return [spec(p) for p in range(pps)]


def _sample_idx(pt_flat, qi_blk, wb, ki_new_t, cache_idx_t, n_pages, topk, dseq):
    nb = qi_blk.shape[0]
    pps = IDX_PAGES_PER_STEP
    nseq = IDX_SEQS_PER_STEP
    n_steps = n_pages // pps
    n_chunks = n_pages + 1
    grid_spec = pltpu.PrefetchScalarGridSpec(
        num_scalar_prefetch=1,
        grid=(nb // nseq, n_steps),
        in_specs=[pl.BlockSpec((nseq, 64, IDX_DIM), lambda b, pc, pt: (b, 0, 0)),
                  pl.BlockSpec((nseq, 64, LANES), lambda b, pc, pt: (b, 0, 0)),
                  pl.BlockSpec((nseq, IDX_DIM, LANES), lambda b, pc, pt: (b, 0, 0)),
                  *[s for r in range(nseq) for s in _page_specs(pps, IDX_DIM, n_pages, nseq, r)]],
        out_specs=pl.BlockSpec((nseq, n_chunks, 8, LANES), lambda b, pc, pt: (b, 0, 0, 0)),
        scratch_shapes=[pltpu.VMEM((n_chunks, nseq * 8, LANES), F32)],
    )
    return pl.pallas_call(
        functools.partial(_sample_idx_kernel, n_steps=n_steps, topk=topk, dseq=dseq, pps=pps),
        grid_spec=grid_spec,
        out_shape=jax.ShapeDtypeStruct((nb, n_chunks, 8, LANES), F32),
        compiler_params=pltpu.CompilerParams(dimension_semantics=("arbitrary", "arbitrary")),
        name="sample_idx",
    )(pt_flat, qi_blk, wb, ki_new_t, *([cache_idx_t] * (nseq * pps)))


def _sample_att_kernel(pt_ref, qb_ref, bias_ref, kn_ref, vn_ref, *rest, n_steps, pps):
    kpages = rest[:pps]
    vpages = rest[pps:2 * pps]
    o_ref, m_ref, l_ref, acc_ref, kbuf, vbuf = rest[2 * pps:]
    pc = pl.program_id(1)
    qb = qb_ref[0]
    n_chunks = n_steps * pps + 1

    def rows64(b8):
        return jnp.concatenate([b8] * ATT_HEADS, axis=0)

    @pl.when(pc == 0)
    def _():
        s = _dot(qb, kn_ref[0]) + rows64(bias_ref[0, n_chunks - 1])
        m = jnp.max(s, axis=1, keepdims=True)
        p = jnp.exp2(s - m)
        m_ref[...] = m
        l_ref[...] = jnp.sum(p, axis=1, keepdims=True)
        acc_ref[...] = _dot_nt(p.astype(BF16), vn_ref[0])

    for p in range(pps):
        kbuf[:, p * PAGE_SIZE:(p + 1) * PAGE_SIZE] = kpages[p][0, 0].astype(BF16)
        vbuf[:, p * PAGE_SIZE:(p + 1) * PAGE_SIZE] = vpages[p][0, 0].astype(BF16)
    bias = jnp.concatenate([bias_ref[0, pc * pps + p] for p in range(pps)], axis=1)
    s = _dot(qb, kbuf[...]) + rows64(bias)
    m_old = m_ref[...]
    m_new = jnp.maximum(m_old, jnp.max(s, axis=1, keepdims=True))
    a = jnp.exp2(m_old - m_new)
    p = jnp.exp2(s - m_new)
    m_ref[...] = m_new
    l_ref[...] = a * l_ref[...] + jnp.sum(p, axis=1, keepdims=True)
    acc_ref[...] = a * acc_ref[...] + _dot_nt(p.astype(BF16), vbuf[...])

    @pl.when(pc == n_steps - 1)
    def _():
        o_ref[0] = acc_ref[...] / l_ref[...]


def _sample_att(pt_flat, qb, bias, k_new_t, v_new_t, cache_k_t, cache_v_t, n_pages):
    nb = qb.shape[0]
    pps = ATT_PAGES_PER_STEP
    n_steps = n_pages // pps
    n_chunks = n_pages + 1
    rows = ATT_HEADS * 8
    grid_spec = pltpu.PrefetchScalarGridSpec(
        num_scalar_prefetch=1,
        grid=(nb, n_steps),
        in_specs=[pl.BlockSpec((1, rows, ATT_WIDTH), lambda b, pc, pt: (b, 0, 0)),
                  pl.BlockSpec((1, n_chunks, 8, LANES), lambda b, pc, pt: (b, 0, 0, 0)),
                  pl.BlockSpec((1, ATT_WIDTH, LANES), lambda b, pc, pt: (b, 0, 0)),
                  pl.BlockSpec((1, ATT_WIDTH, LANES), lambda b, pc, pt: (b, 0, 0)),
                  *_page_specs(pps, ATT_WIDTH, n_pages),
                  *_page_specs(pps, ATT_WIDTH, n_pages)],
        out_specs=pl.BlockSpec((1, rows, ATT_WIDTH), lambda b, pc, pt: (b, 0, 0)),
        scratch_shapes=[pltpu.VMEM((rows, 1), F32), pltpu.VMEM((rows, 1), F32),
                        pltpu.VMEM((rows, ATT_WIDTH), F32),
                        pltpu.VMEM((ATT_WIDTH, pps * PAGE_SIZE), BF16),
                        pltpu.VMEM((ATT_WIDTH, pps * PAGE_SIZE), BF16)],
    )
    return pl.pallas_call(
        functools.partial(_sample_att_kernel, n_steps=n_steps, pps=pps),
        grid_spec=grid_spec,
        out_shape=jax.ShapeDtypeStruct((nb, rows, ATT_WIDTH), F32),
        compiler_params=pltpu.CompilerParams(
            dimension_semantics=("arbitrary", "arbitrary"), vmem_limit_bytes=VMEM_LIMIT),
        name="sample_att",
    )(pt_flat, qb, bias, k_new_t, v_new_t, *([cache_k_t] * pps), *([cache_v_t] * pps))


def _rope_tables(pos):
    half = RET_DK // 2
    inv = ROPE_BASE ** (-jnp.arange(half, dtype=F32) / half)
    ang = pos.astype(F32)[:, None] * inv[None, :]
    cos, sin = jnp.cos(ang), jnp.sin(ang)
    return jnp.concatenate([cos, cos], axis=1), jnp.concatenate([-sin, sin], axis=1)


def _pad_rows(a, rows):
    return jnp.pad(a, ((0, rows - a.shape[0]),) + ((0, 0),) * (a.ndim - 1))


def _pad_last(a, n):
    return jnp.pad(a, ((0, 0),) * (a.ndim - 1) + ((0, n - a.shape[-1]),))


def kernel(x_prompt, x_sample, cache_k, cache_v, cache_idx_k, state_ret, page_table, meta_tokens,
           ffn1_w_gate, ffn1_w_up, ffn1_w_down, ln1_g, ln1_b, w_in, w_out, ln2_g, ln2_b,
           ffn2_w_gate, ffn2_w_up, ffn2_w_down, ln3_g, ln3_b):
    batch, seq, d = x_prompt.shape
    nb, dseq, _ = x_sample.shape
    n_pages = page_table.shape[1]
    n_pool = cache_k.shape[1]
    past = n_pages * PAGE_SIZE
    nq = seq // Q_TILE
    assert d == D_MODEL and seq % Q_TILE == 0 and Q_TILE == ROW_TILE == K_TILE
    assert dseq <= 8 and n_pages % IDX_PAGES_PER_STEP == 0 and n_pages % ATT_PAGES_PER_STEP == 0
    assert nb % IDX_SEQS_PER_STEP == 0
    ns = nb * dseq
    n_small = ns + N_META
    assert n_small <= ROW_TILE
    meta = slice(ns, n_small)

    bf = lambda a: a.astype(BF16)
    l = 0
    f1 = (bf(ffn1_w_gate[l]), bf(ffn1_w_up[l]), bf(ffn1_w_down[l]))
    f2 = (bf(ffn2_w_gate[l]), bf(ffn2_w_up[l]), bf(ffn2_w_down[l]))
    vec = lambda a: a[l][None, :].astype(F32)
    w = w_in[l]
    offs = np.cumsum([0, 512, 512, 512, 512, 64, 8, 512, 512, 512, 512])
    wq, wk, wv, wqi, wki, wwi, wrq, wrk, wrv, wg = [w[:, offs[i]:offs[i + 1]] for i in range(10)]
    kiw = jnp.concatenate([wki, wwi, jnp.zeros((d, LANES - IDX_DIM - IDX_HEADS), w.dtype)], axis=1)
    wrow = bf(jnp.concatenate([wk, kiw, wrq, wrk, wrv, wg], axis=1))
    wt = bf(jnp.concatenate([wq, wk, wv, wqi, kiw], axis=1).T)
    woa, wor = bf(w_out[l][:ATT_WIDTH]), bf(w_out[l][ATT_WIDTH:])

    xs = _pad_rows(jnp.concatenate([x_sample.reshape(ns, d), meta_tokens.astype(x_prompt.dtype)], axis=0), ROW_TILE)
    xp = x_prompt.reshape(batch * seq, d)
    pos_small = jnp.concatenate([jnp.tile(past + jnp.arange(dseq, dtype=I32), nb),
                                 jnp.arange(N_META, dtype=I32),
                                 jnp.zeros((ROW_TILE - n_small,), I32)])
    cos_s, sin_s = _rope_tables(pos_small)
    cos_p, sin_p = _rope_tables(N_META + jnp.arange(seq, dtype=I32))

    x1p = _ffn_ln(xp, *f1, vec(ln1_g), vec(ln1_b))
    x1s = _ffn_ln(xs, *f1, vec(ln1_g), vec(ln1_b))
    (kb_p, kib_p, rq_p, rk_p, rv_p, g_p, qt_p, ktf_p, vtf_p, vt_p, qit_p, kiwt_p) = _proj(
        x1p, wrow, wt, cos_p, sin_p, batch, nq)
    (kb_s, kib_s, rq_s, rk_s, rv_s, g_s, qt_s, ktf_s, vtf_s, vt_s, qit_s, kiwt_s) = _proj(
        x1s, wrow, wt, cos_s, sin_s, 1, 1)
    ktf_s, vtf_s, kiwt_s = ktf_s[0], vtf_s[0], kiwt_s[0]

    topk_p = min(TOPK_MAX, seq // 4)
    att_p = _prompt_att(qt_p, qit_p, kiwt_p[:, IDX_DIM:IDX_DIM + IDX_HEADS, :],
                        kb_p.reshape(batch, nq, K_TILE, ATT_WIDTH),
                        vt_p.reshape(batch, nq, ATT_HEADS, V_ROWS, K_TILE),
                        kib_p.reshape(batch, nq, K_TILE, IDX_DIM),
                        kb_s[meta], _pad_last(vt_s[0][:, :, meta], LANES), kib_s[meta], batch, nq, topk_p)
    att_p = att_p.reshape(batch * seq, ATT_WIDTH)

    pad_tok = lambda a, n: jnp.pad(a, ((0, 0), (0, RET_CHUNK - n), (0, 0)))
    m3 = lambda a: pad_tok(a[meta][None], N_META)
    zero_state = jnp.zeros((1, RET_HEADS, RET_DK, RET_DV), F32)
    _, s_meta = _retention(m3(rq_s), m3(rk_s), m3(rv_s), m3(g_s), zero_state, _ret_tables(N_META), True)
    b3 = lambda a: a.reshape(batch, seq, RET_WIDTH)
    yret_p, ret_prompt = _retention(b3(rq_p), b3(rk_p), b3(rv_p), b3(g_p), s_meta, _ret_tables(RET_CHUNK), True)
    s3 = lambda a: pad_tok(a[:ns].reshape(nb, dseq, RET_WIDTH), dseq)
    yret_s, ret_sample = _retention(s3(rq_s), s3(rk_s), s3(rv_s), s3(g_s), state_ret[l], _ret_tables(dseq), False)

    topk_s = min(TOPK_MAX, (past + dseq) // 4)
    pt_flat = page_table.reshape(-1).astype(I32)
    cache_k_t = jnp.transpose(cache_k[l], (0, 2, 3, 1)).reshape(1, n_pool, ATT_WIDTH, PAGE_SIZE)
    cache_v_t = jnp.transpose(cache_v[l], (0, 2, 3, 1)).reshape(1, n_pool, ATT_WIDTH, PAGE_SIZE)
    cache_idx_t = jnp.transpose(cache_idx_k[l], (0, 2, 1))[None]
    q_rows = jnp.concatenate([qt_s[0, h, (h % 2) * ATT_DIM:(h % 2 + 1) * ATT_DIM, :] for h in range(ATT_HEADS)],
                             axis=0).T[:ns]
    qi_rows = qit_s[0].reshape(IDX_DIM, IDX_HEADS, ROW_TILE)[:, :, :ns]
    qi_blk = jnp.pad(qi_rows.transpose(2, 1, 0).reshape(nb, dseq, IDX_HEADS, IDX_DIM).transpose(0, 2, 1, 3),
                     ((0, 0), (0, 0), (0, 8 - dseq), (0, 0))).reshape(nb, IDX_HEADS * 8, IDX_DIM)
    wi_s = kiwt_s[IDX_DIM:IDX_DIM + IDX_HEADS, :ns].reshape(IDX_HEADS, nb, dseq)
    wb = jnp.pad(wi_s.transpose(1, 0, 2), ((0, 0), (0, 0), (0, 8 - dseq))).reshape(nb, IDX_HEADS * 8, 1)
    wb = jnp.broadcast_to(wb, (nb, IDX_HEADS * 8, LANES)).astype(F32)
    new_t = lambda a_t: _pad_last(a_t[:, :ns].reshape(a_t.shape[0], nb, dseq).transpose(1, 0, 2), LANES)
    bias_s = _sample_idx(pt_flat, qi_blk, wb, bf(new_t(kiwt_s[:IDX_DIM])), cache_idx_t, n_pages, topk_s, dseq)
    head_mask = (jnp.arange(ATT_WIDTH)[None, :] // ATT_DIM == jnp.arange(ATT_HEADS)[:, None])
    q3 = jnp.pad(q_rows.reshape(nb, dseq, ATT_WIDTH), ((0, 0), (0, 8 - dseq), (0, 0)))
    qb = jnp.where(head_mask[None, :, None, :], q3[:, None, :, :], 0).astype(BF16).reshape(nb, ATT_HEADS * 8, ATT_WIDTH)
    att_s64 = _sample_att(pt_flat, qb, bias_s, bf(new_t(ktf_s)), bf(new_t(vtf_s)), cache_k_t, cache_v_t, n_pages)
    a5 = att_s64.reshape(nb, ATT_HEADS, 8, ATT_HEADS, ATT_DIM)
    att_s = jnp.stack([a5[:, h, :dseq, h, :] for h in range(ATT_HEADS)], axis=2).reshape(ns, ATT_WIDTH)

    lnw = (vec(ln2_g), vec(ln2_b), *f2, vec(ln3_g), vec(ln3_b))
    y_p = _out_ffn(x1p, att_p, yret_p.reshape(batch * seq, RET_WIDTH), woa, wor, *lnw)
    att_small = _pad_rows(bf(att_s), ROW_TILE)
    yret_small = _pad_rows(yret_s[:, :dseq].reshape(ns, RET_WIDTH), ROW_TILE)
    y_s = _out_ffn(x1s, att_small, yret_small, woa, wor, *lnw)

    def with_meta(real_t, small_t, feat_shape):
        m = jnp.broadcast_to(small_t[None, :, meta], (batch, small_t.shape[0], N_META))
        full = jnp.concatenate([m, real_t], axis=2)
        full = full.reshape((batch,) + feat_shape + (seq + N_META,))
        return jnp.moveaxis(full, -1, 1)[None]

    k_prompt = with_meta(ktf_p, ktf_s, (ATT_HEADS, ATT_DIM))
    v_prompt = with_meta(vtf_p, vtf_s, (ATT_HEADS, ATT_DIM))
    idxk_prompt = with_meta(kiwt_p[:, :IDX_DIM], kiwt_s[:IDX_DIM], (IDX_DIM,))
    y_prompt = y_p.reshape(batch, seq, d)
    y_sample = y_s[:ns].reshape(nb, dseq, d)
    k_sample = ktf_s[:, :ns].T.reshape(1, nb, dseq, ATT_HEADS, ATT_DIM)
    v_sample = vtf_s[:, :ns].T.reshape(1, nb, dseq, ATT_HEADS, ATT_DIM)
    idxk_sample = kiwt_s[:IDX_DIM, :ns].T.reshape(1, nb, dseq, IDX_DIM)
    return (y_prompt, y_sample, k_prompt, v_prompt, idxk_prompt, ret_prompt[None],
            k_sample, v_sample, idxk_sample, ret_sample[None])
```
